```python
import math
import jax, jax.numpy as jnp
from jax import lax
import numpy as np

D_MODEL = 1024
BATCH = 4
SEQ = 8192
DEPTH = 1
DEC_BATCH = 8
DEC_SEQ = 8192
PAST_LEN = 128

H_A = 8
NOPE_DIM = 64
ROPE_DIM = 32
QK_DIM = NOPE_DIM + ROPE_DIM
V_DIM = 64
Q_LORA = 384
KV_LORA = 256
ROPE_THETA = 10000.0
Q_BLOCK = 128
H_R = 8
HEAD_N = 64
C_R = H_R * HEAD_N
W_LORA = 64
A_LORA = 64
G_LORA = 128
GN_EPS = 64e-5
D_MIX = H_A * V_DIM + C_R
RWKV_IN = 3 * C_R + W_LORA + A_LORA + G_LORA
OFF_Q = 0
OFF_KV = OFF_Q + Q_LORA
OFF_KR = OFF_KV + KV_LORA
OFF_RWKV = OFF_KR + ROPE_DIM
IN_COLS = OFF_RWKV + RWKV_IN
N_KEYS = 128
N_EXPERTS = N_KEYS * N_KEYS
P_HEADS = 8
P_TOPK = 16
D_KEY = 256
HALF_KEY = D_KEY // 2
TOKEN_BLOCK = 128
NORM_EPS = 1e-6

kernel_name = "hybrid_mla_rwkv7_peer_encoder"


def rms_norm(x, g):
    xf = x.astype(jnp.float32)
    y = xf * lax.rsqrt(jnp.mean(xf * xf, axis=-1, keepdims=True) + NORM_EPS)
    return (y * g.astype(jnp.float32)).astype(x.dtype)


def apply_rope(x, pos):
    half = x.shape[-1] // 2
    inv = 1.0 / (ROPE_THETA ** (jnp.arange(half, dtype=jnp.float32) / half))
    ang = pos[:, None] * inv[None, :]
    cos = jnp.cos(ang)[None, :, None, :].astype(x.dtype)
    sin = jnp.sin(ang)[None, :, None, :].astype(x.dtype)
    x1, x2 = x[..., :half], x[..., half:]
    return jnp.concatenate([x1 * cos - x2 * sin, x1 * sin + x2 * cos], axis=-1)


def block_attention(q, k, v):
    B, S, H, Dq = q.shape
    nb = S // Q_BLOCK
    qb = q.reshape(B, nb, Q_BLOCK, H, Dq).transpose(1, 0, 2, 3, 4)
    scale = Dq ** -0.5

    def one(qi):
        s = jnp.einsum('bqhd,bkhd->bhqk', qi, k).astype(jnp.float32) * scale
        p = jax.nn.softmax(s, axis=-1).astype(v.dtype)
        return jnp.einsum('bhqk,bkhd->bqhd', p, v)

    o = lax.map(one, qb)
    return o.transpose(1, 0, 2, 3, 4).reshape(B, S, H, v.shape[-1])


def wkv_scan(r, k, v, decay, kk, a, reverse):
    B, S, H, N = r.shape
    seq = tuple(t.transpose(1, 0, 2, 3) for t in (r, k, v, decay, kk, a))

    def step(state, inp):
        r_t, k_t, v_t, w_t, kk_t, a_t = inp
        sa = jnp.einsum('bhvk,bhk->bhv', state, -kk_t)
        state = (state * w_t[:, :, None, :]
                 + sa[..., None] * (kk_t * a_t)[:, :, None, :]
                 + v_t[..., None] * k_t[:, :, None, :])
        return state, jnp.einsum('bhvk,bhk->bhv', state, r_t)

    s0 = jnp.zeros((B, H, N, N), jnp.float32)
    _, ys = lax.scan(step, s0, seq, reverse=reverse)
    return ys.transpose(1, 0, 2, 3)


def peer_ffn(xn, w_pq, sub_keys, expert_u, expert_v):
    B, S, D = xn.shape
    T = B * S
    dt = xn.dtype
    xt = xn.reshape(T, D)
    q = (xt @ w_pq).astype(jnp.float32).reshape(T, P_HEADS, 2, HALF_KEY)
    sk = sub_keys.astype(jnp.float32)
    s1 = jnp.einsum('thc,nc->thn', q[:, :, 0], sk[0])
    s2 = jnp.einsum('thc,nc->thn', q[:, :, 1], sk[1])
    v1, i1 = lax.top_k(s1, P_TOPK)
    v2, i2 = lax.top_k(s2, P_TOPK)
    cand_s = (v1[..., :, None] + v2[..., None, :]).reshape(T, P_HEADS, P_TOPK * P_TOPK)
    cand_i = (i1[..., :, None] * N_KEYS + i2[..., None, :]).reshape(T, P_HEADS, P_TOPK * P_TOPK)
    best_s, best_pos = lax.top_k(cand_s, P_TOPK)
    idx = jnp.take_along_axis(cand_i, best_pos, axis=-1)
    gate = jax.nn.softmax(best_s, axis=-1)
    PK = P_HEADS * P_TOPK
    nb = T // TOKEN_BLOCK

    def expert_block(args):
        xb, ib, gb = args
        u = expert_u[ib]
        act = jax.nn.gelu(jnp.einsum('td,tkd->tk', xb, u).astype(jnp.float32), approximate=False)
        return jnp.einsum('tk,tkd->td', (gb * act).astype(dt), expert_v[ib])

    out = lax.map(expert_block, (xt.reshape(nb, TOKEN_BLOCK, D),
                                 idx.reshape(nb, TOKEN_BLOCK, PK),
                                 gate.reshape(nb, TOKEN_BLOCK, PK)))
    return out.reshape(B, S, D)


def encoder_layer(x, norm1_g, w_in, q_lat_g, w_uq, kv_lat_g, w_ukv, q_norm_g, k_norm_g,
                  attn_out_g, mu_prev, mu_next, w0, w_up, a0, a_up, g_up, k_k, k_a, r_k,
                  ln_x_g, ln_x_b, w_out, norm2_g, w_pq, sub_keys, expert_u, expert_v):
    B, S, _ = x.shape
    dt = x.dtype
    f32 = jnp.float32
    xn = rms_norm(x, norm1_g)
    proj = xn @ w_in

    q_lat = rms_norm(proj[..., OFF_Q:OFF_KV], q_lat_g)
    kv_lat = rms_norm(proj[..., OFF_KV:OFF_KR], kv_lat_g)
    k_pe = proj[..., OFF_KR:OFF_RWKV]
    q = (q_lat @ w_uq).reshape(B, S, H_A, QK_DIM)
    kv = (kv_lat @ w_ukv).reshape(B, S, H_A, NOPE_DIM + V_DIM)
    k = jnp.concatenate([kv[..., :NOPE_DIM],
                         jnp.broadcast_to(k_pe[:, :, None, :], (B, S, H_A, ROPE_DIM))], axis=-1)
    v = kv[..., NOPE_DIM:]
    q = rms_norm(q, q_norm_g)
    k = rms_norm(k, k_norm_g)
    pos = jnp.arange(S, dtype=f32)
    q = jnp.concatenate([q[..., :NOPE_DIM], apply_rope(q[..., NOPE_DIM:], pos)], axis=-1)
    k = jnp.concatenate([k[..., :NOPE_DIM], apply_rope(k[..., NOPE_DIM:], pos)], axis=-1)
    attn = block_attention(q, k, v).reshape(B, S, H_A * V_DIM)
    attn = rms_norm(attn, attn_out_g)

    z = proj[..., OFF_RWKV:]
    z_prev = jnp.pad(z[:, :-1], ((0, 0), (1, 0), (0, 0)))
    z_next = jnp.pad(z[:, 1:], ((0, 0), (0, 1), (0, 0)))
    z = z + mu_prev * (z_prev - z) + mu_next * (z_next - z)
    o1, o2, o3 = C_R, 2 * C_R, 3 * C_R
    o4, o5 = o3 + W_LORA, o3 + W_LORA + A_LORA
    hs = (B, S, H_R, HEAD_N)
    r = z[..., :o1].astype(f32).reshape(hs)
    kr = z[..., o1:o2].astype(f32).reshape(hs)
    vr = z[..., o2:o3].astype(f32).reshape(hs)
    zw = jnp.tanh(z[..., o3:o4].astype(f32))
    za = z[..., o4:o5].astype(f32)
    g = jax.nn.sigmoid(z[..., o5:]) @ g_up
    kk = kr * k_k.astype(f32).reshape(H_R, HEAD_N)
    kk = kk / jnp.maximum(jnp.sqrt(jnp.sum(kk * kk, axis=-1, keepdims=True)), 1e-12)
    k_a_h = k_a.astype(f32).reshape(H_R, HEAD_N)

    def direction(d, reverse):
        w = -jax.nn.softplus(-(w0[d].astype(f32) + zw @ w_up[d].astype(f32))) - 0.5
        decay = jnp.exp(-jnp.exp(w)).reshape(hs)
        a = jax.nn.sigmoid(a0[d].astype(f32) + za @ a_up[d].astype(f32)).reshape(hs)
        k_d = kr * (1.0 + (a - 1.0) * k_a_h)
        return wkv_scan(r, k_d, vr, decay, kk, a, reverse), k_d

    y_f, k_f = direction(0, False)
    y_b, k_b = direction(1, True)
    y = y_f + y_b
    mu = jnp.mean(y, axis=-1, keepdims=True)
    var = jnp.mean(jnp.square(y - mu), axis=-1, keepdims=True)
    yn = ((y - mu) * lax.rsqrt(var + GN_EPS)).reshape(B, S, C_R)
    yn = yn * ln_x_g.astype(f32) + ln_x_b.astype(f32)
    k_mean = 0.5 * (k_f + k_b)
    bonus = jnp.sum(r * k_mean * r_k.astype(f32).reshape(H_R, HEAD_N), axis=-1, keepdims=True) * vr
    rw = ((yn + bonus.reshape(B, S, C_R)) * g.astype(f32)).astype(dt)

    h = x + jnp.concatenate([attn, rw], axis=-1) @ w_out

    hn = rms_norm(h, norm2_g)
    return h + peer_ffn(hn, w_pq, sub_keys, expert_u, expert_v)


def setup_inputs(seed: int = 0) -> dict:
    key = jax.random.key(seed)
    ks = jax.random.split(key, 32)
    L = DEPTH
    f32 = jnp.float32

    def nrm(k, shape, scale):
        return jax.random.normal(k, shape, f32) * scale

    def gain(k, shape):
        return 1.0 + 0.01 * jax.random.normal(k, shape, f32)

    return {
        "x_prompt": jax.random.normal(ks[0], (BATCH, SEQ, D_MODEL), f32),
        "x_sample": jax.random.normal(ks[1], (DEC_BATCH, DEC_SEQ, D_MODEL), f32),
        "norm1_g": gain(ks[2], (L, D_MODEL)),
        "w_in": nrm(ks[3], (L, D_MODEL, IN_COLS), D_MODEL ** -0.5),
        "q_lat_g": gain(ks[4], (L, Q_LORA)),
        "w_uq": nrm(ks[5], (L, Q_LORA, H_A * QK_DIM), Q_LORA ** -0.5),
        "kv_lat_g": gain(ks[6], (L, KV_LORA)),
        "w_ukv": nrm(ks[7], (L, KV_LORA, H_A * (NOPE_DIM + V_DIM)), KV_LORA ** -0.5),
        "q_norm_g": gain(ks[8], (L, QK_DIM)),
        "k_norm_g": gain(ks[9], (L, QK_DIM)),
        "attn_out_g": gain(ks[10], (L, H_A * V_DIM)),
        "mu_prev": jax.random.uniform(ks[11], (L, RWKV_IN), f32, 0.0, 0.5),
        "mu_next": jax.random.uniform(ks[12], (L, RWKV_IN), f32, 0.0, 0.5),
        "w0": jax.random.uniform(ks[13], (L, 2, C_R), f32, -6.0, 1.0),
        "w_up": nrm(ks[14], (L, 2, W_LORA, C_R), 0.1 * W_LORA ** -0.5),
        "a0": nrm(ks[15], (L, 2, C_R), 0.5),
        "a_up": nrm(ks[16], (L, 2, A_LORA, C_R), 0.5 * A_LORA ** -0.5),
        "g_up": nrm(ks[17], (L, G_LORA, C_R), G_LORA ** -0.5),
        "k_k": 0.85 + 0.05 * jax.random.normal(ks[18], (L, C_R), f32),
        "k_a": 1.0 + 0.05 * jax.random.normal(ks[19], (L, C_R), f32),
        "r_k": nrm(ks[20], (L, C_R), 0.1),
        "ln_x_g": gain(ks[21], (L, C_R)),
        "ln_x_b": nrm(ks[22], (L, C_R), 0.01),
        "w_out": nrm(ks[23], (L, D_MIX, D_MODEL), D_MIX ** -0.5),
        "norm2_g": gain(ks[24], (L, D_MODEL)),
        "w_pq": nrm(ks[25], (L, D_MODEL, P_HEADS * D_KEY), D_MODEL ** -0.5),
        "sub_keys": nrm(ks[26], (L, 2, N_KEYS, HALF_KEY), HALF_KEY ** -0.5),
        "expert_u": nrm(ks[27], (L, N_EXPERTS, D_MODEL), D_MODEL ** -0.5),
        "expert_v": nrm(ks[28], (L, N_EXPERTS, D_MODEL), (P_HEADS * P_TOPK) ** -0.5),
    }


def reference(x_prompt, x_sample, norm1_g, w_in, q_lat_g, w_uq, kv_lat_g, w_ukv, q_norm_g,
              k_norm_g, attn_out_g, mu_prev, mu_next, w0, w_up, a0, a_up, g_up, k_k, k_a, r_k,
              ln_x_g, ln_x_b, w_out, norm2_g, w_pq, sub_keys, expert_u, expert_v):
    layer_weights = (norm1_g, w_in, q_lat_g, w_uq, kv_lat_g, w_ukv, q_norm_g, k_norm_g,
                     attn_out_g, mu_prev, mu_next, w0, w_up, a0, a_up, g_up, k_k, k_a, r_k,
                     ln_x_g, ln_x_b, w_out, norm2_g, w_pq, sub_keys, expert_u, expert_v)

    def trunk(x):
        for l in range(DEPTH):
            x = encoder_layer(x, *(w[l] for w in layer_weights))
        return x

    y_prompt = trunk(x_prompt)
    y_sample = trunk(x_sample)
    return (y_prompt, y_sample)
```

```python
import functools
import math

import jax
import jax.numpy as jnp
from jax import lax
from jax.experimental import pallas as pl
from jax.experimental.pallas import tpu as pltpu
from jax.experimental.pallas import tpu_sc as plsc

D_MODEL = 1024
H_A = 8
NOPE_DIM = 64
ROPE_DIM = 32
QK_DIM = NOPE_DIM + ROPE_DIM
V_DIM = 64
Q_LORA = 384
KV_LORA = 256
ROPE_THETA = 10000.0
H_R = 8
HEAD_N = 64
C_R = H_R * HEAD_N
W_LORA = 64
A_LORA = 64
G_LORA = 128
GN_EPS = 64e-5
RWKV_IN = 3 * C_R + W_LORA + A_LORA + G_LORA
OFF_Q = 0
OFF_KV = OFF_Q + Q_LORA
OFF_KR = OFF_KV + KV_LORA
OFF_RWKV = OFF_KR + ROPE_DIM
N_KEYS = 128
N_EXPERTS = N_KEYS * N_KEYS
P_HEADS = 8
P_TOPK = 16
D_KEY = 256
HALF_KEY = D_KEY // 2
NORM_EPS = 1e-6

LANES = 128
HEAD_PAD = LANES
Z_COLS = RWKV_IN + C_R
IN_PAD = Q_LORA + KV_LORA + Z_COLS + 2 * LANES
VMEM_LIMIT = 56 * 1024 * 1024

f32 = jnp.float32
bf16 = jnp.bfloat16


def _rms(x, g):
    return x * lax.rsqrt(jnp.mean(x * x, axis=-1, keepdims=True) + NORM_EPS) * g


def _dot(a, b):
    return jnp.dot(a, b, preferred_element_type=f32)


def _dot_nt(a, b):
    return lax.dot_general(a, b, (((1,), (1,)), ((), ())), preferred_element_type=f32)


def _split_dot(x, j):
    hi = x.astype(bf16)
    lo = (x - hi.astype(f32)).astype(bf16)
    return _dot(hi, j) + _dot(lo, j)


def _inproj_kernel(x_ref, g1_ref, win_ref, qlg_ref, wqa_ref, wqb_ref, kvlg_ref, wkn_ref, wv_ref,
                   cos_ref, sin_ref, gq_ref, gk_ref,
                   q_ref, k_ref, v_ref, z_ref):
    x = x_ref[...]
    xn = _rms(x, g1_ref[...]).astype(bf16)
    proj = _dot(xn, win_ref[...])
    o_kv, o_z, o_ra, o_rb = Q_LORA, Q_LORA + KV_LORA, Q_LORA + KV_LORA + Z_COLS, Q_LORA + KV_LORA + Z_COLS + LANES
    z_ref[...] = proj[:, o_z:o_ra]
    cos = cos_ref[...]
    sin = sin_ref[...]

    ql = _rms(proj[:, :o_kv], qlg_ref[...]).astype(bf16)
    qa = _dot(ql, wqa_ref[...])
    qb = _dot(ql, wqb_ref[...])
    ga, gb = gq_ref[0:1, :], gq_ref[1:2, :]
    scale = QK_DIM ** -0.5
    for h in range(H_A):
        a = qa[:, h * LANES:(h + 1) * LANES]
        b = qb[:, h * LANES:(h + 1) * LANES]
        s = lax.rsqrt(jnp.sum(a * a, axis=-1, keepdims=True) * (1.0 / QK_DIM) + NORM_EPS) * scale
        q_ref[0, h] = (s * (a * ga * cos + b * gb * sin)).astype(bf16)

    kvl = _rms(proj[:, o_kv:o_z], kvlg_ref[...]).astype(bf16)
    kn = _dot(kvl, wkn_ref[...])
    vv = _dot(kvl, wv_ref[...])
    for p in range(H_A // 2):
        v_ref[0, p] = vv[:, p * LANES:(p + 1) * LANES].astype(bf16)
    kra = proj[:, o_ra:o_rb]
    krb = proj[:, o_rb:o_rb + LANES]
    gkn, gka, gkb = gk_ref[0:1, :], gk_ref[1:2, :], gk_ref[2:3, :]
    kr = kra * gka * cos + krb * gkb * sin
    ssr = jnp.sum(kra * kra, axis=-1, keepdims=True)
    for h in range(H_A):
        a = kn[:, h * LANES:(h + 1) * LANES]
        s = lax.rsqrt((jnp.sum(a * a, axis=-1, keepdims=True) + ssr) * (1.0 / QK_DIM) + NORM_EPS)
        k_ref[0, h] = (s * (a * gkn + kr)).astype(bf16)


def _inproj(x, g1, win, qlg, wqa, wqb, kvlg, wkn, wv, cos, sin, gq, gk, *, B, S, tm):
    nS = S // tm
    const = lambda shape: pl.BlockSpec(shape, lambda b, s: (0,) * len(shape))
    return pl.pallas_call(
        _inproj_kernel,
        grid=(B, nS),
        in_specs=[
            pl.BlockSpec((tm, D_MODEL), lambda b, s: (b * nS + s, 0)),
            const((1, D_MODEL)), const(win.shape), const((1, Q_LORA)), const(wqa.shape), const(wqb.shape),
            const((1, KV_LORA)), const(wkn.shape), const(wv.shape),
            pl.BlockSpec((tm, LANES), lambda b, s: (s, 0)),
            pl.BlockSpec((tm, LANES), lambda b, s: (s, 0)),
            const(gq.shape), const(gk.shape),
        ],
        out_specs=[
            pl.BlockSpec((1, H_A, tm, HEAD_PAD), lambda b, s: (b, 0, s, 0)),
            pl.BlockSpec((1, H_A, tm, HEAD_PAD), lambda b, s: (b, 0, s, 0)),
            pl.BlockSpec((1, H_A // 2, tm, LANES), lambda b, s: (b, 0, s, 0)),
            pl.BlockSpec((tm, Z_COLS), lambda b, s: (b * nS + s, 0)),
        ],
        out_shape=[
            jax.ShapeDtypeStruct((B, H_A, S, HEAD_PAD), bf16),
            jax.ShapeDtypeStruct((B, H_A, S, HEAD_PAD), bf16),
            jax.ShapeDtypeStruct((B, H_A // 2, S, LANES), bf16),
            jax.ShapeDtypeStruct((B * S, Z_COLS), f32),
        ],
        compiler_params=pltpu.CompilerParams(
            dimension_semantics=("parallel", "parallel"), vmem_limit_bytes=VMEM_LIMIT),
        name="in_proj",
    )(x, g1, win, qlg, wqa, wqb, kvlg, wkn, wv, cos, sin, gq, gk)


def _attn_kernel(q_ref, k_ref, v_ref, o_ref, *, tk, nk):
    tq = q_ref.shape[2]
    outs = []
    for h in range(2):
        q = q_ref[0, h]

        def body(j, carry, h=h, q=q):
            m, l, acc = carry
            off = pl.multiple_of(j * tk, tk)
            kt = k_ref[0, h, pl.ds(off, tk), :]
            vt = v_ref[0, 0, pl.ds(off, tk), :]
            s = _dot_nt(q, kt)
            m_new = jnp.maximum(m, jnp.max(s, axis=-1, keepdims=True))
            alpha = jnp.exp(m - m_new)
            p = jnp.exp(s - m_new)
            l = alpha * l + jnp.sum(p, axis=-1, keepdims=True)
            acc = alpha * acc + _dot(p.astype(bf16), vt)
            return m_new, l, acc

        m0 = jnp.full((tq, 1), -jnp.inf, f32)
        l0 = jnp.zeros((tq, 1), f32)
        a0 = jnp.zeros((tq, LANES), f32)
        m, l, acc = lax.fori_loop(0, nk, body, (m0, l0, a0))
        outs.append(acc / l)
    lane = lax.broadcasted_iota(jnp.int32, (tq, LANES), 1)
    o_ref[0, 0] = jnp.where(lane < V_DIM, outs[0], outs[1])


def _attention(q, k, v, *, B, S, tq, tk):
    return pl.pallas_call(
        functools.partial(_attn_kernel, tk=tk, nk=S // tk),
        grid=(B, H_A // 2, S // tq),
        in_specs=[
            pl.BlockSpec((1, 2, tq, HEAD_PAD), lambda b, p, i: (b, p, i, 0)),
            pl.BlockSpec((1, 2, S, HEAD_PAD), lambda b, p, i: (b, p, 0, 0)),
            pl.BlockSpec((1, 1, S, LANES), lambda b, p, i: (b, p, 0, 0)),
        ],
        out_specs=pl.BlockSpec((1, 1, tq, LANES), lambda b, p, i: (b, p, i, 0)),
        out_shape=jax.ShapeDtypeStruct((B, H_A // 2, S, LANES), f32),
        compiler_params=pltpu.CompilerParams(
            dimension_semantics=("parallel", "parallel", "parallel"), vmem_limit_bytes=VMEM_LIMIT),
        name="attention",
    )(q, k, v)


def _softplus(x):
    return jnp.maximum(x, 0.0) + jnp.log1p(jnp.exp(-jnp.abs(x)))


def _prep_kernel(z_ref, zp_ref, zn_ref, mup_ref, mun_ref, wl_ref, w0_ref, a0_ref, gup_ref, kk_ref, ka_ref, jseg_ref,
                 r_ref, v_ref, vp_ref, nkk_ref, g_ref,
                 decf_ref, kkaf_ref, kf_ref, decb_ref, kkab_ref, kb_ref):
    si = pl.program_id(1)
    ns = pl.num_programs(1)
    z = z_ref[...]
    tm = z.shape[0]
    row = lax.broadcasted_iota(jnp.int32, (tm, 1), 0)
    zp_row = jnp.where(si == 0, 0.0, zp_ref[7:8, :])
    zn_row = jnp.where(si == ns - 1, 0.0, zn_ref[0:1, :])
    z_prev = jnp.where(row == 0, zp_row, pltpu.roll(z, 1, 0))
    z_next = jnp.where(row == tm - 1, zn_row, pltpu.roll(z, tm - 1, 0))
    zm = z + mup_ref[...] * (z_prev - z) + mun_ref[...] * (z_next - z)

    o1, o2, o3 = C_R, 2 * C_R, 3 * C_R
    o4 = o3 + W_LORA + A_LORA
    o5 = o4 + G_LORA
    r = zm[:, :o1]
    kr = zm[:, o1:o2]
    vr = zm[:, o2:o3]
    lw = zm[:, o3:o4]
    zg = zm[:, o4:o5]
    r_ref[...] = r
    v_ref[...] = vr
    vp_ref[...] = zm[:, o5:]
    g_ref[...] = _dot(jax.nn.sigmoid(zg).astype(bf16), gup_ref[...])

    kk = kr * kk_ref[...]
    ssk = _split_dot(kk * kk, jseg_ref[...])
    kk = kk / jnp.maximum(jnp.sqrt(ssk), 1e-12)
    nkk_ref[...] = -kk

    lane = lax.broadcasted_iota(jnp.int32, lw.shape, 1)
    lin = jnp.where(lane < W_LORA, jnp.tanh(lw), lw).astype(bf16)
    ka = ka_ref[...]
    for d, (dec_ref, kka_ref, kd_ref) in enumerate(((decf_ref, kkaf_ref, kf_ref), (decb_ref, kkab_ref, kb_ref))):
        lo = _dot(lin, wl_ref[d])
        w = -_softplus(-(w0_ref[d:d + 1, :] + lo[:, :C_R])) - 0.5
        dec_ref[...] = jnp.exp(-jnp.exp(w))
        a = jax.nn.sigmoid(a0_ref[d:d + 1, :] + lo[:, C_R:])
        kd_ref[...] = kr * (1.0 + (a - 1.0) * ka)
        kka_ref[...] = kk * a


def _rwkv_prep(z, mup, mun, wl, w0, a0, gup, kk, ka, jseg, *, B, S, tm):
    nS = S // tm
    T = B * S
    nb8 = tm // 8
    const = lambda shape: pl.BlockSpec(shape, lambda b, s: (0,) * len(shape))
    row_spec = pl.BlockSpec((tm, C_R), lambda b, s: (b * nS + s, 0))
    return pl.pallas_call(
        _prep_kernel,
        grid=(B, nS),
        in_specs=[
            pl.BlockSpec((tm, Z_COLS), lambda b, s: (b * nS + s, 0)),
            pl.BlockSpec((8, Z_COLS), lambda b, s: (jnp.maximum((b * nS + s) * nb8 - 1, 0), 0)),
            pl.BlockSpec((8, Z_COLS), lambda b, s: (jnp.minimum((b * nS + s + 1) * nb8, T // 8 - 1), 0)),
            const((1, Z_COLS)), const((1, Z_COLS)), const(wl.shape), const(w0.shape), const(a0.shape),
            const(gup.shape), const((1, C_R)), const((1, C_R)), const(jseg.shape),
        ],
        out_specs=[row_spec] * 11,
        out_shape=[jax.ShapeDtypeStruct((T, C_R), f32)] * 11,
        compiler_params=pltpu.CompilerParams(
            dimension_semantics=("parallel", "parallel"), vmem_limit_bytes=VMEM_LIMIT),
        name="rwkv_prep",
    )(z, z, z, mup, mun, wl, w0, a0, gup, kk, ka, jseg)


N_PAIRS = H_R // 2


def _scan_kernel(rf_ref, nkkf_ref, vpf_ref, decf_ref, kkaf_ref, kf_ref,
                 rb_ref, nkkb_ref, vpb_ref, decb_ref, kkab_ref, kb_ref,
                 jsa_ref, jy_ref, yf_ref, yb_ref, st_ref, yacc_ref, *, tc):
    @pl.when(pl.program_id(1) == 0)
    def _():
        st_ref[...] = jnp.zeros_like(st_ref)

    jsa = jsa_ref[...]
    jy = jy_ref[...]
    lane = lax.broadcasted_iota(jnp.int32, (HEAD_N, LANES), 1)
    dirs = ((rf_ref, nkkf_ref, vpf_ref, decf_ref, kkaf_ref, kf_ref),
            (rb_ref, nkkb_ref, vpb_ref, decb_ref, kkab_ref, kb_ref))

    def group(gi, carry):
        for d, refs in enumerate(dirs):
            base = pl.multiple_of((gi if d == 0 else tc // 8 - 1 - gi) * 8, 8)
            r8, nkk8, vp8, dec8, kka8, k8 = (ref[pl.ds(base, 8), :] for ref in refs)
            for jj in range(8):
                j = jj if d == 0 else 7 - jj
                row = lambda blk, lo: blk[j:j + 1, lo:lo + LANES]
                hit = lane == base + j
                for g in range(N_PAIRS // 2):
                    a = jnp.concatenate([
                        jnp.broadcast_to(row(vp8, (2 * g) * LANES), (HEAD_N, LANES)),
                        jnp.broadcast_to(row(vp8, (2 * g + 1) * LANES), (HEAD_N, LANES)),
                    ], axis=0)
                    at = a.T
                    for q in range(2):
                        p = 2 * g + q
                        c = d * N_PAIRS + p
                        lo = p * LANES
                        vb = at[q * HEAD_N:(q + 1) * HEAD_N, :]
                        s = st_ref[c]
                        sa = _split_dot(s * row(nkk8, lo), jsa)
                        s = s * row(dec8, lo) + sa * row(kka8, lo) + vb * row(k8, lo)
                        st_ref[c] = s
                        yb = _split_dot(s * row(r8, lo), jy)
                        yacc_ref[c, 0:HEAD_N, :] = jnp.where(hit, yb[:, :LANES], yacc_ref[c, 0:HEAD_N, :])
                        yacc_ref[c, HEAD_N:, :] = jnp.where(hit, yb[:, LANES:], yacc_ref[c, HEAD_N:, :])
        return carry

    lax.fori_loop(0, tc // 8, group, 0)
    for d, y_ref in enumerate((yf_ref, yb_ref)):
        for p in range(N_PAIRS):
            y_ref[:, p * LANES:(p + 1) * LANES] = yacc_ref[d * N_PAIRS + p].T


def _wkv_scan(r, nkk, vp, decf, kkaf, kf, decb, kkab, kb, jsa, jy, *, B, S, tc):
    nT = S // tc
    T = B * S
    fwd = pl.BlockSpec((tc, C_R), lambda b, i: (b * nT + i, 0))
    bwd = pl.BlockSpec((tc, C_R), lambda b, i: (b * nT + nT - 1 - i, 0))
    const = lambda shape: pl.BlockSpec(shape, lambda b, i: (0,) * len(shape))
    return pl.pallas_call(
        functools.partial(_scan_kernel, tc=tc),
        grid=(B, nT),
        in_specs=[fwd] * 6 + [bwd] * 6 + [const(jsa.shape), const(jy.shape)],
        out_specs=[fwd, bwd],
        out_shape=[jax.ShapeDtypeStruct((T, C_R), f32)] * 2,
        scratch_shapes=[pltpu.VMEM((2 * N_PAIRS, HEAD_N, LANES), f32),
                        pltpu.VMEM((2 * N_PAIRS, LANES, LANES), f32)],
        compiler_params=pltpu.CompilerParams(
            dimension_semantics=("parallel", "arbitrary"), vmem_limit_bytes=VMEM_LIMIT),
        name="wkv_scan",
    )(r, nkk, vp, decf, kkaf, kf, r, nkk, vp, decb, kkab, kb, jsa, jy)


def _mix_kernel(x_ref, attn_ref, yf_ref, yb_ref, r_ref, kf_ref, kb_ref, v_ref, g_ref,
                lng_ref, lnb_ref, rk_ref, aog_ref, wout_ref, jseg_ref, h_ref):
    jseg = jseg_ref[...]
    y = yf_ref[...] + yb_ref[...]
    mu = _split_dot(y, jseg) * (1.0 / HEAD_N)
    dlt = y - mu
    var = _split_dot(dlt * dlt, jseg) * (1.0 / HEAD_N)
    yn = dlt * lax.rsqrt(var + GN_EPS) * lng_ref[...] + lnb_ref[...]
    r = r_ref[...]
    k_mean = 0.5 * (kf_ref[...] + kb_ref[...])
    bonus = _split_dot(r * k_mean * rk_ref[...], jseg) * v_ref[...]
    rw = ((yn + bonus) * g_ref[...]).astype(bf16)
    attn = jnp.concatenate([attn_ref[0, p] for p in range(H_A // 2)], axis=-1)
    an = _rms(attn, aog_ref[...]).astype(bf16)
    ha = H_A * V_DIM
    h_ref[...] = x_ref[...] + _dot(an, wout_ref[0:ha, :]) + _dot(rw, wout_ref[ha:, :])


def _mix(x, attn, yf, yb, r, kf, kb, v, g, lng, lnb, rk, aog, wout, jseg, *, B, S, tm):
    nS = S // tm
    const = lambda shape: pl.BlockSpec(shape, lambda b, s: (0,) * len(shape))
    row = lambda c: pl.BlockSpec((tm, c), lambda b, s: (b * nS + s, 0))
    return pl.pallas_call(
        _mix_kernel,
        grid=(B, nS),
        in_specs=[row(D_MODEL), pl.BlockSpec((1, H_A // 2, tm, LANES), lambda b, s: (b, 0, s, 0))]
                 + [row(C_R)] * 7
                 + [const((1, C_R))] * 3 + [const((1, H_A * V_DIM)), const(wout.shape), const(jseg.shape)],
        out_specs=row(D_MODEL),
        out_shape=jax.ShapeDtypeStruct((B * S, D_MODEL), f32),
        compiler_params=pltpu.CompilerParams(
            dimension_semantics=("parallel", "parallel"), vmem_limit_bytes=VMEM_LIMIT),
        name="mix",
    )(x, attn, yf, yb, r, kf, kb, v, g, lng, lnb, rk, aog, wout, jseg)


def _take_max(s, ids):
    m = jnp.max(s, axis=0, keepdims=True)
    pick = jnp.min(jnp.where(s == m, ids, jnp.int32(2 ** 30)), axis=0, keepdims=True)
    return m, pick, jnp.where(ids == pick, -jnp.inf, s)


def _route_kernel(h_ref, g2_ref, wpq_ref, sk_ref, idx_ref, gate_ref, v1_ref, i1_ref, v2_ref, i2_ref, cs_ref, ci_ref,
                  bs_ref, bi_ref):
    tm = h_ref.shape[0]
    hn = _rms(h_ref[...], g2_ref[...]).astype(bf16)
    pq = _dot(hn, wpq_ref[...])
    key_ids = lax.broadcasted_iota(jnp.int32, (N_KEYS, tm), 0)
    pos_ids = lax.broadcasted_iota(jnp.int32, (P_TOPK * P_TOPK, tm), 0)
    for p in range(P_HEADS):
        for side, (vs_ref, is_ref) in enumerate(((v1_ref, i1_ref), (v2_ref, i2_ref))):
            qh = pq[:, p * D_KEY + side * HALF_KEY:p * D_KEY + (side + 1) * HALF_KEY].astype(bf16)
            s = _dot_nt(sk_ref[side], qh)
            for j in range(P_TOPK):
                m, pick, s = _take_max(s, key_ids)
                vs_ref[j:j + 1, :] = m
                is_ref[j:j + 1, :] = pick
        v2 = v2_ref[...]
        i2 = i2_ref[...]
        for a in range(P_TOPK):
            cs_ref[a * P_TOPK:(a + 1) * P_TOPK, :] = v1_ref[a:a + 1, :] + v2
            ci_ref[a * P_TOPK:(a + 1) * P_TOPK, :] = i1_ref[a:a + 1, :] * N_KEYS + i2
        s = cs_ref[...]
        ci = ci_ref[...]
        for j in range(P_TOPK):
            m, pick, s = _take_max(s, pos_ids)
            bs_ref[j:j + 1, :] = m
            bi_ref[p * P_TOPK + j:p * P_TOPK + j + 1, :] = jnp.max(
                jnp.where(pos_ids == pick, ci, -1), axis=0, keepdims=True)
        bs = bs_ref[...]
        e = jnp.exp(bs - bs[0:1, :])
        gate_ref[:, p * P_TOPK:(p + 1) * P_TOPK] = (e / jnp.sum(e, axis=0, keepdims=True)).T
    idx_ref[...] = bi_ref[...].T


def _route(h, g2, wpq, sk, *, T, tm):
    const = lambda shape: pl.BlockSpec(shape, lambda i: (0,) * len(shape))
    pk = P_HEADS * P_TOPK
    return pl.pallas_call(
        _route_kernel,
        grid=(T // tm,),
        in_specs=[pl.BlockSpec((tm, D_MODEL), lambda i: (i, 0)), const((1, D_MODEL)), const(wpq.shape),
                  const(sk.shape)],
        out_specs=[pl.BlockSpec((tm, pk), lambda i: (i, 0))] * 2,
        out_shape=[jax.ShapeDtypeStruct((T, pk), jnp.int32), jax.ShapeDtypeStruct((T, pk), f32)],
        scratch_shapes=[pltpu.VMEM((P_TOPK, tm), f32), pltpu.VMEM((P_TOPK, tm), jnp.int32),
                        pltpu.VMEM((P_TOPK, tm), f32), pltpu.VMEM((P_TOPK, tm), jnp.int32),
                        pltpu.VMEM((P_TOPK * P_TOPK, tm), f32), pltpu.VMEM((P_TOPK * P_TOPK, tm), jnp.int32),
                        pltpu.VMEM((P_TOPK, tm), f32), pltpu.VMEM((pk, tm), jnp.int32)],
        compiler_params=pltpu.CompilerParams(dimension_semantics=("parallel",), vmem_limit_bytes=VMEM_LIMIT),
        name="route",
    )(h, g2, wpq, sk)


GATHER_WINDOW = 32
SC_CORES = 2
SC_SUBCORES = 16
SC_WORKERS = SC_CORES * SC_SUBCORES


def _gather_rows(table, idx):
    n = idx.shape[0]
    d = table.shape[1]
    win = GATHER_WINDOW
    n_it = n // (SC_WORKERS * win)
    assert n_it * SC_WORKERS * win == n and n_it % 2 == 0, (n, n_it)
    mesh = plsc.VectorSubcoreMesh(core_axis_name="core", subcore_axis_name="subcore")

    @functools.partial(
        pl.kernel, out_type=jax.ShapeDtypeStruct((n, d), table.dtype), mesh=mesh,
        scratch_types=[pltpu.VMEM((n_it, win), jnp.int32), pltpu.VMEM((2, win, d), table.dtype),
                       pltpu.SemaphoreType.DMA, pltpu.SemaphoreType.DMA,
                       pltpu.SemaphoreType.DMA, pltpu.SemaphoreType.DMA])
    def gather(tab_hbm, idx_hbm, out_hbm, idx_v, rows_v, gsem0, gsem1, wsem0, wsem1):
        wid = lax.axis_index("subcore") * SC_CORES + lax.axis_index("core")
        base = wid * (n_it * win)
        gsem = (gsem0, gsem1)
        wsem = (wsem0, wsem1)
        pltpu.sync_copy(idx_hbm.at[wid], idx_v)

        def fetch(j, b):
            return pltpu.make_async_copy(tab_hbm.at[idx_v.at[j]], rows_v.at[b], gsem[b])

        def flush(j, b):
            return pltpu.make_async_copy(rows_v.at[b], out_hbm.at[pl.ds(base + j * win, win)], wsem[b])

        fetch(0, 0).start()

        @pl.loop(0, n_it, step=2)
        def _(j0):
            for b in range(2):
                j = j0 + b

                @pl.when(j >= 1)
                def _():
                    flush(j - 1, 1 - b).wait()

                @pl.when(j + 1 < n_it)
                def _():
                    fetch(j + 1, 1 - b).start()

                fetch(j, b).wait()
                flush(j, b).start()

        flush(n_it - 1, 1).wait()

    return gather(table, idx.reshape(SC_WORKERS, n_it, win))


def _expert_kernel(h_ref, g2_ref, gate_ref, gu_ref, gv_ref, o_ref):
    h = h_ref[...]
    tb = h.shape[0]
    pk = P_HEADS * P_TOPK
    hn = _rms(h, g2_ref[...]).astype(bf16)
    row = lax.broadcasted_iota(jnp.int32, (8, pk), 0)
    for grp in range(tb // 8):
        hn8 = hn[grp * 8:(grp + 1) * 8, :]
        act = jnp.zeros((8, pk), f32)
        for j in range(8):
            t = grp * 8 + j
            u = gu_ref[t * pk:(t + 1) * pk, :].astype(bf16)
            act = jnp.where(row == j, _dot_nt(hn8, u), act)
        gelu = 0.5 * act * (1.0 + lax.erf(act * (2.0 ** -0.5)))
        w = (gate_ref[grp * 8:(grp + 1) * 8, :] * gelu).astype(bf16)
        out = jnp.zeros((8, D_MODEL), f32)
        for j in range(8):
            t = grp * 8 + j
            v = gv_ref[t * pk:(t + 1) * pk, :].astype(bf16)
            out = out + _dot(jnp.where(row == j, w, jnp.zeros_like(w)), v)
        o_ref[grp * 8:(grp + 1) * 8, :] = h[grp * 8:(grp + 1) * 8, :] + out


def _experts(h, g2, gate, gu, gv, *, tb):
    n = h.shape[0]
    pk = P_HEADS * P_TOPK
    const = lambda shape: pl.BlockSpec(shape, lambda i: (0,) * len(shape))
    return pl.pallas_call(
        _expert_kernel,
        grid=(n // tb,),
        in_specs=[pl.BlockSpec((tb, D_MODEL), lambda i: (i, 0)), const((1, D_MODEL)),
                  pl.BlockSpec((tb, pk), lambda i: (i, 0)),
                  pl.BlockSpec((tb * pk, D_MODEL), lambda i: (i, 0)),
                  pl.BlockSpec((tb * pk, D_MODEL), lambda i: (i, 0))],
        out_specs=pl.BlockSpec((tb, D_MODEL), lambda i: (i, 0)),
        out_shape=jax.ShapeDtypeStruct((n, D_MODEL), f32),
        compiler_params=pltpu.CompilerParams(dimension_semantics=("parallel",), vmem_limit_bytes=VMEM_LIMIT),
        name="experts",
    )(h, g2, gate, gu, gv)


def _place(cols, width, offset):
    return jnp.pad(cols, ((0, 0), (offset, width - offset - cols.shape[1])))


def _rope_partner():
    half = ROPE_DIM // 2
    return jnp.concatenate([jnp.arange(half, ROPE_DIM), jnp.arange(0, half)])


def _prepare(norm1_g, w_in, q_lat_g, w_uq, kv_lat_g, w_ukv, q_norm_g, k_norm_g, attn_out_g, mu_prev, mu_next,
             w0, w_up, a0, a_up, g_up, k_k, k_a, r_k, ln_x_g, ln_x_b, w_out, norm2_g, w_pq, sub_keys, S):
    partner = _rope_partner()
    vperm_heads = jnp.array([0, 2, 1, 3, 4, 6, 5, 7])
    vcols = (vperm_heads[:, None] * HEAD_N + jnp.arange(HEAD_N)[None, :]).reshape(-1) + 2 * C_R

    def with_vperm(m):
        return jnp.concatenate([m, m[..., vcols]], axis=-1)

    wz = with_vperm(w_in[:, OFF_RWKV:])
    wkr = w_in[:, OFF_KR:OFF_RWKV]
    win = jnp.concatenate([
        w_in[:, OFF_Q:OFF_KV], w_in[:, OFF_KV:OFF_KR], wz,
        _place(wkr, LANES, NOPE_DIM), _place(wkr[:, partner], LANES, NOPE_DIM)], axis=1).astype(bf16)

    wq = w_uq.reshape(Q_LORA, H_A, QK_DIM)
    wqa = jnp.pad(wq, ((0, 0), (0, 0), (0, HEAD_PAD - QK_DIM))).reshape(Q_LORA, H_A * HEAD_PAD).astype(bf16)
    wqb = jnp.pad(wq[:, :, NOPE_DIM:][:, :, partner],
                  ((0, 0), (0, 0), (NOPE_DIM, HEAD_PAD - QK_DIM))).reshape(Q_LORA, H_A * HEAD_PAD).astype(bf16)
    wkv = w_ukv.reshape(KV_LORA, H_A, NOPE_DIM + V_DIM)
    wkn = jnp.pad(wkv[:, :, :NOPE_DIM], ((0, 0), (0, 0), (0, HEAD_PAD - NOPE_DIM))).reshape(
        KV_LORA, H_A * HEAD_PAD).astype(bf16)
    wv = wkv[:, :, NOPE_DIM:].reshape(KV_LORA, H_A * V_DIM).astype(bf16)

    def gain_rows(g):
        ga = _place(g[None, :], LANES, 0)
        gb = _place(g[None, NOPE_DIM:][:, partner], LANES, NOPE_DIM)
        return ga, gb

    gqa, gqb = gain_rows(q_norm_g)
    gq = jnp.concatenate([gqa, gqb, jnp.zeros((6, LANES), f32)], axis=0)
    gkn = _place(k_norm_g[None, :NOPE_DIM], LANES, 0)
    gka = _place(k_norm_g[None, NOPE_DIM:], LANES, NOPE_DIM)
    gkb = _place(k_norm_g[None, NOPE_DIM:][:, partner], LANES, NOPE_DIM)
    gk = jnp.concatenate([gkn, gka, gkb, jnp.zeros((5, LANES), f32)], axis=0)

    half = ROPE_DIM // 2
    inv = 1.0 / (ROPE_THETA ** (jnp.arange(half, dtype=f32) / half))
    ang = jnp.arange(S, dtype=f32)[:, None] * inv[None, :]
    c, s = jnp.cos(ang), jnp.sin(ang)
    cos = jnp.concatenate([jnp.ones((S, NOPE_DIM), f32), c, c, jnp.zeros((S, HEAD_PAD - QK_DIM), f32)], axis=1)
    sin = jnp.concatenate([jnp.zeros((S, NOPE_DIM), f32), -s, s, jnp.zeros((S, HEAD_PAD - QK_DIM), f32)], axis=1)

    zeros = jnp.zeros((W_LORA, C_R), f32)
    wl = jnp.stack([jnp.concatenate([jnp.concatenate([w_up[d], zeros], axis=1),
                                     jnp.concatenate([zeros, a_up[d]], axis=1)], axis=0) for d in range(2)]).astype(bf16)
    seg = jnp.arange(C_R) // HEAD_N
    jseg = (seg[:, None] == seg[None, :]).astype(bf16)
    lane_head = jnp.arange(LANES) // HEAD_N
    jsa = (lane_head[:, None] == lane_head[None, :]).astype(bf16)
    jy = (lane_head[:, None] == (jnp.arange(2 * LANES) // LANES)[None, :]).astype(bf16)

    return dict(
        g1=norm1_g[None, :], win=win, qlg=q_lat_g[None, :], wqa=wqa, wqb=wqb, kvlg=kv_lat_g[None, :], wkn=wkn, wv=wv,
        cos=cos, sin=sin, gq=gq, gk=gk,
        mup=with_vperm(mu_prev)[None, :], mun=with_vperm(mu_next)[None, :], wl=wl, w0=w0, a0=a0,
        gup=g_up.astype(bf16), kk=k_k[None, :], ka=k_a[None, :], jseg=jseg, jsa=jsa, jy=jy,
        lng=ln_x_g[None, :], lnb=ln_x_b[None, :], rk=r_k[None, :], aog=attn_out_g[None, :],
        wout=w_out.astype(bf16), g2=norm2_g[None, :], wpq=w_pq.astype(bf16), sk=sub_keys.astype(bf16),
    )


def _tile(n, pref):
    t = min(n, pref)
    assert n % t == 0, (n, t)
    return t


PEER_CHUNK = 2048


def _layer(x, w, expert_u, expert_v):
    B, S, _ = x.shape
    T = B * S
    xt = x.reshape(T, D_MODEL)
    tm = _tile(S, 256)
    q, k, v, z = _inproj(xt, w["g1"], w["win"], w["qlg"], w["wqa"], w["wqb"], w["kvlg"], w["wkn"], w["wv"],
                         w["cos"], w["sin"], w["gq"], w["gk"], B=B, S=S, tm=tm)
    attn = _attention(q, k, v, B=B, S=S, tq=_tile(S, 512), tk=_tile(S, 512))
    r, vr, vp, nkk, g, decf, kkaf, kf, decb, kkab, kb = _rwkv_prep(
        z, w["mup"], w["mun"], w["wl"], w["w0"], w["a0"], w["gup"], w["kk"], w["ka"], w["jseg"], B=B, S=S, tm=tm)
    yf, yb = _wkv_scan(r, nkk, vp, decf, kkaf, kf, decb, kkab, kb, w["jsa"], w["jy"], B=B, S=S, tc=LANES)
    h = _mix(xt, attn, yf, yb, r, kf, kb, vr, g, w["lng"], w["lnb"], w["rk"], w["aog"], w["wout"], w["jseg"],
             B=B, S=S, tm=tm)
    idx, gate = _route(h, w["g2"], w["wpq"], w["sk"], T=T, tm=_tile(T, 256))
    pk = P_HEADS * P_TOPK
    tc = _tile(T, PEER_CHUNK)
    outs = []
    for c in range(T // tc):
        sl = slice(c * tc, (c + 1) * tc)
        flat = idx[sl].reshape(tc * pk)
        gu = _gather_rows(expert_u, flat)
        gv = _gather_rows(expert_v, flat)
        outs.append(_experts(h[sl], w["g2"], gate[sl], gu, gv, tb=_tile(tc, 16)))
    return jnp.concatenate(outs, axis=0).reshape(B, S, D_MODEL)


def kernel(x_prompt, x_sample, norm1_g, w_in, q_lat_g, w_uq, kv_lat_g, w_ukv, q_norm_g, k_norm_g, attn_out_g, mu_prev, mu_next, w0, w_up, a0, a_up, g_up, k_k, k_a, r_k, ln_x_g, ln_x_b, w_out, norm2_g, w_pq, sub_keys, expert_u, expert_v):
    assert x_prompt.shape[1] == x_sample.shape[1]
    S = x_prompt.shape[1]
    nb = x_prompt.shape[0]
    x = jnp.concatenate([x_prompt, x_sample], axis=0)
    for l in range(norm1_g.shape[0]):
        w = _prepare(norm1_g[l], w_in[l], q_lat_g[l], w_uq[l], kv_lat_g[l], w_ukv[l], q_norm_g[l], k_norm_g[l],
                     attn_out_g[l], mu_prev[l], mu_next[l], w0[l], w_up[l], a0[l], a_up[l], g_up[l], k_k[l], k_a[l],
                     r_k[l], ln_x_g[l], ln_x_b[l], w_out[l], norm2_g[l], w_pq[l], sub_keys[l], S)
        x = _layer(x, w, expert_u[l], expert_v[l])
    return x[:nb], x[nb:]
```

```python
import functools
import math

import jax
import jax.numpy as jnp
from jax import lax
from jax.experimental import pallas as pl
from jax.experimental.pallas import tpu as pltpu
from jax.experimental.pallas import tpu_sc as plsc

D_MODEL = 1024
H_A = 8
NOPE_DIM = 64
ROPE_DIM = 32
QK_DIM = NOPE_DIM + ROPE_DIM
V_DIM = 64
Q_LORA = 384
KV_LORA = 256
ROPE_THETA = 10000.0
H_R = 8
HEAD_N = 64
C_R = H_R * HEAD_N
W_LORA = 64
A_LORA = 64
G_LORA = 128
GN_EPS = 64e-5
RWKV_IN = 3 * C_R + W_LORA + A_LORA + G_LORA
OFF_Q = 0
OFF_KV = OFF_Q + Q_LORA
OFF_KR = OFF_KV + KV_LORA
OFF_RWKV = OFF_KR + ROPE_DIM
N_KEYS = 128
N_EXPERTS = N_KEYS * N_KEYS
P_HEADS = 8
P_TOPK = 16
D_KEY = 256
HALF_KEY = D_KEY // 2
NORM_EPS = 1e-6

LANES = 128
HEAD_PAD = LANES
Z_COLS = RWKV_IN + C_R
IN_PAD = Q_LORA + KV_LORA + Z_COLS + 2 * LANES
VMEM_LIMIT = 56 * 1024 * 1024

f32 = jnp.float32
bf16 = jnp.bfloat16


def _rms(x, g):
    return x * lax.rsqrt(jnp.mean(x * x, axis=-1, keepdims=True) + NORM_EPS) * g


def _dot(a, b):
    return jnp.dot(a, b, preferred_element_type=f32)


def _dot_nt(a, b):
    return lax.dot_general(a, b, (((1,), (1,)), ((), ())), preferred_element_type=f32)


def _split_dot(x, j):
    hi = x.astype(bf16)
    lo = (x - hi.astype(f32)).astype(bf16)
    return _dot(hi, j) + _dot(lo, j)


def _inproj_kernel(x_ref, g1_ref, win_ref, qlg_ref, wqa_ref, wqb_ref, kvlg_ref, wkn_ref, wv_ref,
                   cos_ref, sin_ref, gq_ref, gk_ref,
                   q_ref, k_ref, v_ref, z_ref):
    x = x_ref[...]
    xn = _rms(x, g1_ref[...]).astype(bf16)
    proj = _dot(xn, win_ref[...])
    o_kv, o_z, o_ra, o_rb = Q_LORA, Q_LORA + KV_LORA, Q_LORA + KV_LORA + Z_COLS, Q_LORA + KV_LORA + Z_COLS + LANES
    z_ref[...] = proj[:, o_z:o_ra]
    cos = cos_ref[...]
    sin = sin_ref[...]

    ql = _rms(proj[:, :o_kv], qlg_ref[...]).astype(bf16)
    qa = _dot(ql, wqa_ref[...])
    qb = _dot(ql, wqb_ref[...])
    ga, gb = gq_ref[0:1, :], gq_ref[1:2, :]
    scale = QK_DIM ** -0.5 * math.log2(math.e)
    for h in range(H_A):
        a = qa[:, h * LANES:(h + 1) * LANES]
        b = qb[:, h * LANES:(h + 1) * LANES]
        s = lax.rsqrt(jnp.sum(a * a, axis=-1, keepdims=True) * (1.0 / QK_DIM) + NORM_EPS) * scale
        q_ref[0, h] = (s * (a * ga * cos + b * gb * sin)).astype(bf16)

    kvl = _rms(proj[:, o_kv:o_z], kvlg_ref[...]).astype(bf16)
    kn = _dot(kvl, wkn_ref[...])
    vv = _dot(kvl, wv_ref[...])
    one = (lax.broadcasted_iota(jnp.int32, (1, LANES), 1) == V_DIM).astype(f32)
    for h in range(H_A):
        v_ref[0, h] = (vv[:, h * LANES:(h + 1) * LANES] + one).astype(bf16)
    kra = proj[:, o_ra:o_rb]
    krb = proj[:, o_rb:o_rb + LANES]
    gkn, gka, gkb = gk_ref[0:1, :], gk_ref[1:2, :], gk_ref[2:3, :]
    kr = kra * gka * cos + krb * gkb * sin
    ssr = jnp.sum(kra * kra, axis=-1, keepdims=True)
    for h in range(H_A):
        a = kn[:, h * LANES:(h + 1) * LANES]
        s = lax.rsqrt((jnp.sum(a * a, axis=-1, keepdims=True) + ssr) * (1.0 / QK_DIM) + NORM_EPS)
        k_ref[0, h] = (s * (a * gkn + kr)).astype(bf16)


def _inproj(x, g1, win, qlg, wqa, wqb, kvlg, wkn, wv, cos, sin, gq, gk, *, B, S, tm):
    nS = S // tm
    const = lambda shape: pl.BlockSpec(shape, lambda b, s: (0,) * len(shape))
    return pl.pallas_call(
        _inproj_kernel,
        grid=(B, nS),
        in_specs=[
            pl.BlockSpec((tm, D_MODEL), lambda b, s: (b * nS + s, 0)),
            const((1, D_MODEL)), const(win.shape), const((1, Q_LORA)), const(wqa.shape), const(wqb.shape),
            const((1, KV_LORA)), const(wkn.shape), const(wv.shape),
            pl.BlockSpec((tm, LANES), lambda b, s: (s, 0)),
            pl.BlockSpec((tm, LANES), lambda b, s: (s, 0)),
            const(gq.shape), const(gk.shape),
        ],
        out_specs=[
            pl.BlockSpec((1, H_A, tm, HEAD_PAD), lambda b, s: (b, 0, s, 0)),
            pl.BlockSpec((1, H_A, tm, HEAD_PAD), lambda b, s: (b, 0, s, 0)),
            pl.BlockSpec((1, H_A, tm, LANES), lambda b, s: (b, 0, s, 0)),
            pl.BlockSpec((tm, Z_COLS), lambda b, s: (b * nS + s, 0)),
        ],
        out_shape=[
            jax.ShapeDtypeStruct((B, H_A, S, HEAD_PAD), bf16),
            jax.ShapeDtypeStruct((B, H_A, S, HEAD_PAD), bf16),
            jax.ShapeDtypeStruct((B, H_A, S, LANES), bf16),
            jax.ShapeDtypeStruct((B * S, Z_COLS), f32),
        ],
        compiler_params=pltpu.CompilerParams(
            dimension_semantics=("parallel", "parallel"), vmem_limit_bytes=VMEM_LIMIT),
        name="in_proj",
    )(x, g1, win, qlg, wqa, wqb, kvlg, wkn, wv, cos, sin, gq, gk)


ATTN_ROWS = 32
ATTN_UNROLL = 16


def _attn_kernel(q_ref, k_ref, v_ref, o_ref, s_ref, p_ref, m_ref, al_ref, acc_ref, *, tk, nk):
    tq = q_ref.shape[2]
    nlt = tk // LANES

    def scores(h, q, j, buf):
        off = pl.multiple_of(jnp.minimum(j, nk - 1) * tk, tk)
        s_ref[buf] = _dot_nt(q, k_ref[0, h, pl.ds(off, tk), :])

    def softmax(buf):
        def chunk(c, carry):
            rows = pl.ds(pl.multiple_of(c * ATTN_ROWS, ATTN_ROWS), ATTN_ROWS)
            tiles = [s_ref[buf, rows, lt * LANES:(lt + 1) * LANES] for lt in range(nlt)]
            best = functools.reduce(jnp.maximum, tiles)
            m_old = m_ref[rows, :]
            m_new = jnp.maximum(m_old, jnp.broadcast_to(jnp.max(best, axis=-1, keepdims=True), m_old.shape))
            for lt in range(nlt):
                p_ref[buf, rows, lt * LANES:(lt + 1) * LANES] = jnp.exp2(tiles[lt] - m_new).astype(bf16)
            al_ref[rows, :] = jnp.exp2(m_old - m_new)
            m_ref[rows, :] = m_new
            return carry

        lax.fori_loop(0, tq // ATTN_ROWS, chunk, 0, unroll=ATTN_UNROLL)

    def accumulate(h, j, buf):
        off = pl.multiple_of(j * tk, tk)
        acc_ref[...] = al_ref[...] * acc_ref[...] + _dot(p_ref[buf], v_ref[0, h, pl.ds(off, tk), :])

    outs = []
    for h in range(2):
        q = q_ref[0, h]
        m_ref[...] = jnp.full(m_ref.shape, -jnp.inf, f32)
        acc_ref[...] = jnp.zeros(acc_ref.shape, f32)
        scores(h, q, 0, 0)

        def body(jj, carry, h=h, q=q):
            j = 2 * jj
            scores(h, q, j + 1, 1)
            softmax(0)
            accumulate(h, j, 0)
            scores(h, q, j + 2, 0)
            softmax(1)
            accumulate(h, j + 1, 1)
            return carry

        lax.fori_loop(0, nk // 2, body, 0)
        acc = acc_ref[...]
        outs.append(acc / acc[:, V_DIM:V_DIM + 1])
    lane = lax.broadcasted_iota(jnp.int32, (tq, LANES), 1)
    o_ref[0, 0] = jnp.where(lane < V_DIM, outs[0], pltpu.roll(outs[1], V_DIM, 1))


def _attention(q, k, v, *, B, S, tq, tk):
    return pl.pallas_call(
        functools.partial(_attn_kernel, tk=tk, nk=S // tk),
        grid=(B, H_A // 2, S // tq),
        in_specs=[
            pl.BlockSpec((1, 2, tq, HEAD_PAD), lambda b, p, i: (b, p, i, 0)),
            pl.BlockSpec((1, 2, S, HEAD_PAD), lambda b, p, i: (b, p, 0, 0)),
            pl.BlockSpec((1, 2, S, LANES), lambda b, p, i: (b, p, 0, 0)),
        ],
        out_specs=pl.BlockSpec((1, 1, tq, LANES), lambda b, p, i: (b, p, i, 0)),
        out_shape=jax.ShapeDtypeStruct((B, H_A // 2, S, LANES), f32),
        scratch_shapes=[pltpu.VMEM((2, tq, tk), f32), pltpu.VMEM((2, tq, tk), bf16), pltpu.VMEM((tq, LANES), f32),
                        pltpu.VMEM((tq, LANES), f32), pltpu.VMEM((tq, LANES), f32)],
        compiler_params=pltpu.CompilerParams(
            dimension_semantics=("parallel", "parallel", "parallel"), vmem_limit_bytes=VMEM_LIMIT),
        name="attention",
    )(q, k, v)


def _softplus(x):
    return jnp.maximum(x, 0.0) + jnp.log1p(jnp.exp(-jnp.abs(x)))


def _prep_kernel(z_ref, zp_ref, zn_ref, mup_ref, mun_ref, wl_ref, w0_ref, a0_ref, gup_ref, kk_ref, ka_ref, jseg_ref,
                 r_ref, v_ref, vp_ref, nkk_ref, g_ref,
                 decf_ref, kkaf_ref, kf_ref, decb_ref, kkab_ref, kb_ref):
    si = pl.program_id(1)
    ns = pl.num_programs(1)
    z = z_ref[...]
    tm = z.shape[0]
    row = lax.broadcasted_iota(jnp.int32, (tm, 1), 0)
    zp_row = jnp.where(si == 0, 0.0, zp_ref[7:8, :])
    zn_row = jnp.where(si == ns - 1, 0.0, zn_ref[0:1, :])
    z_prev = jnp.where(row == 0, zp_row, pltpu.roll(z, 1, 0))
    z_next = jnp.where(row == tm - 1, zn_row, pltpu.roll(z, tm - 1, 0))
    zm = z + mup_ref[...] * (z_prev - z) + mun_ref[...] * (z_next - z)

    o1, o2, o3 = C_R, 2 * C_R, 3 * C_R
    o4 = o3 + W_LORA + A_LORA
    o5 = o4 + G_LORA
    r = zm[:, :o1]
    kr = zm[:, o1:o2]
    vr = zm[:, o2:o3]
    lw = zm[:, o3:o4]
    zg = zm[:, o4:o5]
    r_ref[...] = r
    v_ref[...] = vr
    vp_ref[...] = zm[:, o5:]
    g_ref[...] = _dot(jax.nn.sigmoid(zg).astype(bf16), gup_ref[...])

    kk = kr * kk_ref[...]
    ssk = _split_dot(kk * kk, jseg_ref[...])
    kk = kk / jnp.maximum(jnp.sqrt(ssk), 1e-12)
    nkk_ref[...] = -kk

    lane = lax.broadcasted_iota(jnp.int32, lw.shape, 1)
    lin = jnp.where(lane < W_LORA, jnp.tanh(lw), lw).astype(bf16)
    ka = ka_ref[...]
    for d, (dec_ref, kka_ref, kd_ref) in enumerate(((decf_ref, kkaf_ref, kf_ref), (decb_ref, kkab_ref, kb_ref))):
        lo = _dot(lin, wl_ref[d])
        w = -_softplus(-(w0_ref[d:d + 1, :] + lo[:, :C_R])) - 0.5
        dec_ref[...] = jnp.exp(-jnp.exp(w))
        a = jax.nn.sigmoid(a0_ref[d:d + 1, :] + lo[:, C_R:])
        kd_ref[...] = kr * (1.0 + (a - 1.0) * ka)
        kka_ref[...] = kk * a


def _rwkv_prep(z, mup, mun, wl, w0, a0, gup, kk, ka, jseg, *, B, S, tm):
    nS = S // tm
    T = B * S
    nb8 = tm // 8
    const = lambda shape: pl.BlockSpec(shape, lambda b, s: (0,) * len(shape))
    row_spec = pl.BlockSpec((tm, C_R), lambda b, s: (b * nS + s, 0))
    return pl.pallas_call(
        _prep_kernel,
        grid=(B, nS),
        in_specs=[
            pl.BlockSpec((tm, Z_COLS), lambda b, s: (b * nS + s, 0)),
            pl.BlockSpec((8, Z_COLS), lambda b, s: (jnp.maximum((b * nS + s) * nb8 - 1, 0), 0)),
            pl.BlockSpec((8, Z_COLS), lambda b, s: (jnp.minimum((b * nS + s + 1) * nb8, T // 8 - 1), 0)),
            const((1, Z_COLS)), const((1, Z_COLS)), const(wl.shape), const(w0.shape), const(a0.shape),
            const(gup.shape), const((1, C_R)), const((1, C_R)), const(jseg.shape),
        ],
        out_specs=[row_spec] * 11,
        out_shape=[jax.ShapeDtypeStruct((T, C_R), f32)] * 11,
        compiler_params=pltpu.CompilerParams(
            dimension_semantics=("parallel", "parallel"), vmem_limit_bytes=VMEM_LIMIT),
        name="rwkv_prep",
    )(z, z, z, mup, mun, wl, w0, a0, gup, kk, ka, jseg)


N_PAIRS = H_R // 2


def _scan_kernel(rf_ref, nkkf_ref, vpf_ref, decf_ref, kkaf_ref, kf_ref,
                 rb_ref, nkkb_ref, vpb_ref, decb_ref, kkab_ref, kb_ref,
                 jsa_ref, jy_ref, yf_ref, yb_ref, st_ref, yacc_ref, *, tc):
    @pl.when(pl.program_id(1) == 0)
    def _():
        st_ref[...] = jnp.zeros_like(st_ref)

    jsa = jsa_ref[...]
    jy = jy_ref[...]
    lane = lax.broadcasted_iota(jnp.int32, (HEAD_N, LANES), 1)
    dirs = ((rf_ref, nkkf_ref, vpf_ref, decf_ref, kkaf_ref, kf_ref),
            (rb_ref, nkkb_ref, vpb_ref, decb_ref, kkab_ref, kb_ref))

    chains = [(d, p) for d in range(2) for p in range(N_PAIRS)]

    def group(gi, carry):
        bases = (pl.multiple_of(gi * 8, 8), pl.multiple_of((tc // 8 - 1 - gi) * 8, 8))
        blocks = [[ref[pl.ds(bases[d], 8), :] for ref in dirs[d]] for d in range(2)]
        for jj in range(8):
            js = (jj, 7 - jj)

            def row(d, p, which):
                return blocks[d][which][js[d]:js[d] + 1, p * LANES:(p + 1) * LANES]

            vbs = {}
            for d in range(2):
                for g in range(N_PAIRS // 2):
                    vp8 = blocks[d][2]
                    a = jnp.concatenate([
                        jnp.broadcast_to(vp8[js[d]:js[d] + 1, (2 * g) * LANES:(2 * g + 1) * LANES], (HEAD_N, LANES)),
                        jnp.broadcast_to(vp8[js[d]:js[d] + 1, (2 * g + 1) * LANES:(2 * g + 2) * LANES], (HEAD_N, LANES)),
                    ], axis=0).T
                    vbs[(d, 2 * g)] = a[:HEAD_N, :]
                    vbs[(d, 2 * g + 1)] = a[HEAD_N:, :]
            sa = _dot(jnp.concatenate(
                [st_ref[c] * row(d, p, 1) for c, (d, p) in enumerate(chains)], axis=0).astype(bf16), jsa)
            new = []
            for c, (d, p) in enumerate(chains):
                s = (st_ref[c] * row(d, p, 3) + sa[c * HEAD_N:(c + 1) * HEAD_N, :] * row(d, p, 4)
                     + vbs[(d, p)] * row(d, p, 5))
                st_ref[c] = s
                new.append(s * row(d, p, 0))
            yb = _dot(jnp.concatenate(new, axis=0).astype(bf16), jy)
            for c, (d, p) in enumerate(chains):
                hit = lane == bases[d] + js[d]
                ybc = yb[c * HEAD_N:(c + 1) * HEAD_N, :]
                yacc_ref[c, 0:HEAD_N, :] = jnp.where(hit, ybc[:, :LANES], yacc_ref[c, 0:HEAD_N, :])
                yacc_ref[c, HEAD_N:, :] = jnp.where(hit, ybc[:, LANES:], yacc_ref[c, HEAD_N:, :])
        return carry

    lax.fori_loop(0, tc // 8, group, 0)
    for d, y_ref in enumerate((yf_ref, yb_ref)):
        for p in range(N_PAIRS):
            y_ref[:, p * LANES:(p + 1) * LANES] = yacc_ref[d * N_PAIRS + p].T


def _wkv_scan(r, nkk, vp, decf, kkaf, kf, decb, kkab, kb, jsa, jy, *, B, S, tc):
    nT = S // tc
    T = B * S
    fwd = pl.BlockSpec((tc, C_R), lambda b, i: (b * nT + i, 0))
    bwd = pl.BlockSpec((tc, C_R), lambda b, i: (b * nT + nT - 1 - i, 0))
    const = lambda shape: pl.BlockSpec(shape, lambda b, i: (0,) * len(shape))
    return pl.pallas_call(
        functools.partial(_scan_kernel, tc=tc),
        grid=(B, nT),
        in_specs=[fwd] * 6 + [bwd] * 6 + [const(jsa.shape), const(jy.shape)],
        out_specs=[fwd, bwd],
        out_shape=[jax.ShapeDtypeStruct((T, C_R), f32)] * 2,
        scratch_shapes=[pltpu.VMEM((2 * N_PAIRS, HEAD_N, LANES), f32),
                        pltpu.VMEM((2 * N_PAIRS, LANES, LANES), f32)],
        compiler_params=pltpu.CompilerParams(
            dimension_semantics=("parallel", "arbitrary"), vmem_limit_bytes=VMEM_LIMIT),
        name="wkv_scan",
    )(r, nkk, vp, decf, kkaf, kf, r, nkk, vp, decb, kkab, kb, jsa, jy)


def _mix_kernel(x_ref, attn_ref, yf_ref, yb_ref, r_ref, kf_ref, kb_ref, v_ref, g_ref,
                lng_ref, lnb_ref, rk_ref, aog_ref, wout_ref, jseg_ref, h_ref):
    jseg = jseg_ref[...]
    y = yf_ref[...] + yb_ref[...]
    mu = _split_dot(y, jseg) * (1.0 / HEAD_N)
    dlt = y - mu
    var = _split_dot(dlt * dlt, jseg) * (1.0 / HEAD_N)
    yn = dlt * lax.rsqrt(var + GN_EPS) * lng_ref[...] + lnb_ref[...]
    r = r_ref[...]
    k_mean = 0.5 * (kf_ref[...] + kb_ref[...])
    bonus = _split_dot(r * k_mean * rk_ref[...], jseg) * v_ref[...]
    rw = ((yn + bonus) * g_ref[...]).astype(bf16)
    attn = jnp.concatenate([attn_ref[0, p] for p in range(H_A // 2)], axis=-1)
    an = _rms(attn, aog_ref[...]).astype(bf16)
    ha = H_A * V_DIM
    h_ref[...] = x_ref[...] + _dot(an, wout_ref[0:ha, :]) + _dot(rw, wout_ref[ha:, :])


def _mix(x, attn, yf, yb, r, kf, kb, v, g, lng, lnb, rk, aog, wout, jseg, *, B, S, tm):
    nS = S // tm
    const = lambda shape: pl.BlockSpec(shape, lambda b, s: (0,) * len(shape))
    row = lambda c: pl.BlockSpec((tm, c), lambda b, s: (b * nS + s, 0))
    return pl.pallas_call(
        _mix_kernel,
        grid=(B, nS),
        in_specs=[row(D_MODEL), pl.BlockSpec((1, H_A // 2, tm, LANES), lambda b, s: (b, 0, s, 0))]
                 + [row(C_R)] * 7
                 + [const((1, C_R))] * 3 + [const((1, H_A * V_DIM)), const(wout.shape), const(jseg.shape)],
        out_specs=row(D_MODEL),
        out_shape=jax.ShapeDtypeStruct((B * S, D_MODEL), f32),
        compiler_params=pltpu.CompilerParams(
            dimension_semantics=("parallel", "parallel"), vmem_limit_bytes=VMEM_LIMIT),
        name="mix",
    )(x, attn, yf, yb, r, kf, kb, v, g, lng, lnb, rk, aog, wout, jseg)


def _take_max(s, ids):
    m = jnp.max(s, axis=0, keepdims=True)
    pick = jnp.min(jnp.where(s == m, ids, jnp.int32(2 ** 30)), axis=0, keepdims=True)
    return m, pick, jnp.where(ids == pick, -jnp.inf, s)


CAND_COLS = tuple(P_TOPK // (a + 1) for a in range(P_TOPK))
N_CAND = sum(CAND_COLS)
N_CAND_PAD = -(-N_CAND // 8) * 8


def _route_kernel(h_ref, g2_ref, wpq_ref, sk_ref, pos_ref, idx_ref, gate_ref, v1_ref, i1_ref, v2_ref, i2_ref, cs_ref,
                  ci_ref, bs_ref, bi_ref):
    tm = h_ref.shape[0]
    hn = _rms(h_ref[...], g2_ref[...]).astype(bf16)
    pq = _dot(hn, wpq_ref[...])
    key_ids = lax.broadcasted_iota(jnp.int32, (N_KEYS, tm), 0)
    pos_ids = pos_ref[...]
    for p in range(P_HEADS):
        for side, (vs_ref, is_ref) in enumerate(((v1_ref, i1_ref), (v2_ref, i2_ref))):
            qh = pq[:, p * D_KEY + side * HALF_KEY:p * D_KEY + (side + 1) * HALF_KEY].astype(bf16)
            s = _dot_nt(sk_ref[side], qh)
            for j in range(P_TOPK):
                m, pick, s = _take_max(s, key_ids)
                vs_ref[j:j + 1, :] = m
                is_ref[j:j + 1, :] = pick
        v2 = v2_ref[...]
        i2 = i2_ref[...]
        off = 0
        for a, nb in enumerate(CAND_COLS):
            cs_ref[off:off + nb, :] = v1_ref[a:a + 1, :] + v2[:nb, :]
            ci_ref[off:off + nb, :] = i1_ref[a:a + 1, :] * N_KEYS + i2[:nb, :]
            off += nb
        cs_ref[N_CAND:, :] = jnp.full((N_CAND_PAD - N_CAND, tm), -jnp.inf, f32)
        ci_ref[N_CAND:, :] = jnp.zeros((N_CAND_PAD - N_CAND, tm), jnp.int32)
        s = cs_ref[...]
        ci = ci_ref[...]
        for j in range(P_TOPK):
            m, pick, s = _take_max(s, pos_ids)
            bs_ref[j:j + 1, :] = m
            bi_ref[p * P_TOPK + j:p * P_TOPK + j + 1, :] = jnp.max(
                jnp.where(pos_ids == pick, ci, -1), axis=0, keepdims=True)
        bs = bs_ref[...]
        e = jnp.exp(bs - bs[0:1, :])
        gate_ref[:, p * P_TOPK:(p + 1) * P_TOPK] = (e / jnp.sum(e, axis=0, keepdims=True)).T
    idx_ref[...] = bi_ref[...].T


def _route(h, g2, wpq, sk, *, T, tm):
    const = lambda shape: pl.BlockSpec(shape, lambda i: (0,) * len(shape))
    pk = P_HEADS * P_TOPK
    flat = [a * P_TOPK + b for a, nb in enumerate(CAND_COLS) for b in range(nb)] + [2 ** 20] * (N_CAND_PAD - N_CAND)
    pos = jnp.broadcast_to(jnp.array(flat, jnp.int32)[:, None], (N_CAND_PAD, tm))
    return pl.pallas_call(
        _route_kernel,
        grid=(T // tm,),
        in_specs=[pl.BlockSpec((tm, D_MODEL), lambda i: (i, 0)), const((1, D_MODEL)), const(wpq.shape),
                  const(sk.shape), const((N_CAND_PAD, tm))],
        out_specs=[pl.BlockSpec((tm, pk), lambda i: (i, 0))] * 2,
        out_shape=[jax.ShapeDtypeStruct((T, pk), jnp.int32), jax.ShapeDtypeStruct((T, pk), f32)],
        scratch_shapes=[pltpu.VMEM((P_TOPK, tm), f32), pltpu.VMEM((P_TOPK, tm), jnp.int32),
                        pltpu.VMEM((P_TOPK, tm), f32), pltpu.VMEM((P_TOPK, tm), jnp.int32),
                        pltpu.VMEM((N_CAND_PAD, tm), f32), pltpu.VMEM((N_CAND_PAD, tm), jnp.int32),
                        pltpu.VMEM((P_TOPK, tm), f32), pltpu.VMEM((pk, tm), jnp.int32)],
        compiler_params=pltpu.CompilerParams(dimension_semantics=("parallel",), vmem_limit_bytes=VMEM_LIMIT),
        name="route",
    )(h, g2, wpq, sk, pos)


GATHER_WINDOW = 64
SC_CORES = 2
SC_SUBCORES = 16
SC_WORKERS = SC_CORES * SC_SUBCORES


def _gather_rows(table, idx):
    n = idx.shape[0]
    d = table.shape[1]
    win = GATHER_WINDOW
    n_it = n // (SC_WORKERS * win)
    assert n_it * SC_WORKERS * win == n and n_it % 2 == 0, (n, n_it)
    mesh = plsc.VectorSubcoreMesh(core_axis_name="core", subcore_axis_name="subcore")

    @functools.partial(
        pl.kernel, out_type=jax.ShapeDtypeStruct((n, d), table.dtype), mesh=mesh,
        scratch_types=[pltpu.VMEM((n_it, win), jnp.int32), pltpu.VMEM((2, win, d), table.dtype),
                       pltpu.SemaphoreType.DMA, pltpu.SemaphoreType.DMA,
                       pltpu.SemaphoreType.DMA, pltpu.SemaphoreType.DMA])
    def gather(tab_hbm, idx_hbm, out_hbm, idx_v, rows_v, gsem0, gsem1, wsem0, wsem1):
        wid = lax.axis_index("subcore") * SC_CORES + lax.axis_index("core")
        base = wid * (n_it * win)
        gsem = (gsem0, gsem1)
        wsem = (wsem0, wsem1)
        pltpu.sync_copy(idx_hbm.at[wid], idx_v)

        def fetch(j, b):
            return pltpu.make_async_copy(tab_hbm.at[idx_v.at[j]], rows_v.at[b], gsem[b])

        def flush(j, b):
            return pltpu.make_async_copy(rows_v.at[b], out_hbm.at[pl.ds(base + j * win, win)], wsem[b])

        fetch(0, 0).start()

        @pl.loop(0, n_it, step=2)
        def _(j0):
            for b in range(2):
                j = j0 + b

                @pl.when(j >= 1)
                def _():
                    flush(j - 1, 1 - b).wait()

                @pl.when(j + 1 < n_it)
                def _():
                    fetch(j + 1, 1 - b).start()

                fetch(j, b).wait()
                flush(j, b).start()

        flush(n_it - 1, 1).wait()

    return gather(table, idx.reshape(SC_WORKERS, n_it, win))


HALF_D = D_MODEL // 2


def _pack_rows(table):
    bits = lax.bitcast_convert_type(table.astype(bf16), jnp.uint16).astype(jnp.uint32)
    return bits[:, :HALF_D] | (bits[:, HALF_D:] << 16)


def _unpack_rows(words):
    lo = lax.bitcast_convert_type(words << 16, f32).astype(bf16)
    hi = lax.bitcast_convert_type(words & jnp.uint32(0xFFFF0000), f32).astype(bf16)
    return lo, hi


def _expert_kernel(h_ref, g2_ref, gate_ref, gu_ref, gv_ref, o_ref):
    h = h_ref[...]
    tb = h.shape[0]
    pk = P_HEADS * P_TOPK
    hn = _rms(h, g2_ref[...]).astype(bf16)
    row = lax.broadcasted_iota(jnp.int32, (8, pk), 0)
    for grp in range(tb // 8):
        rows = slice(grp * 8, (grp + 1) * 8)
        hn8 = hn[rows, :]
        act = jnp.zeros((8, pk), f32)
        for j in range(8):
            t = grp * 8 + j
            ulo, uhi = _unpack_rows(gu_ref[t * pk:(t + 1) * pk, :])
            act = jnp.where(row == j, _dot_nt(hn8[:, :HALF_D], ulo) + _dot_nt(hn8[:, HALF_D:], uhi), act)
        gelu = 0.5 * act * (1.0 + lax.erf(act * (2.0 ** -0.5)))
        w = (gate_ref[rows, :] * gelu).astype(bf16)
        out_lo = jnp.zeros((8, HALF_D), f32)
        out_hi = jnp.zeros((8, HALF_D), f32)
        for j in range(8):
            t = grp * 8 + j
            vlo, vhi = _unpack_rows(gv_ref[t * pk:(t + 1) * pk, :])
            wj = jnp.where(row == j, w, jnp.zeros_like(w))
            out_lo = out_lo + _dot(wj, vlo)
            out_hi = out_hi + _dot(wj, vhi)
        o_ref[rows, :HALF_D] = h[rows, :HALF_D] + out_lo
        o_ref[rows, HALF_D:] = h[rows, HALF_D:] + out_hi


def _experts(h, g2, gate, gu, gv, *, tb):
    n = h.shape[0]
    pk = P_HEADS * P_TOPK
    const = lambda shape: pl.BlockSpec(shape, lambda i: (0,) * len(shape))
    return pl.pallas_call(
        _expert_kernel,
        grid=(n // tb,),
        in_specs=[pl.BlockSpec((tb, D_MODEL), lambda i: (i, 0)), const((1, D_MODEL)),
                  pl.BlockSpec((tb, pk), lambda i: (i, 0)),
                  pl.BlockSpec((tb * pk, HALF_D), lambda i: (i, 0)),
                  pl.BlockSpec((tb * pk, HALF_D), lambda i: (i, 0))],
        out_specs=pl.BlockSpec((tb, D_MODEL), lambda i: (i, 0)),
        out_shape=jax.ShapeDtypeStruct((n, D_MODEL), f32),
        compiler_params=pltpu.CompilerParams(dimension_semantics=("parallel",), vmem_limit_bytes=VMEM_LIMIT),
        name="experts",
    )(h, g2, gate, gu, gv)


def _place(cols, width, offset):
    return jnp.pad(cols, ((0, 0), (offset, width - offset - cols.shape[1])))


def _rope_partner():
    half = ROPE_DIM // 2
    return jnp.concatenate([jnp.arange(half, ROPE_DIM), jnp.arange(0, half)])


def _prepare(norm1_g, w_in, q_lat_g, w_uq, kv_lat_g, w_ukv, q_norm_g, k_norm_g, attn_out_g, mu_prev, mu_next,
             w0, w_up, a0, a_up, g_up, k_k, k_a, r_k, ln_x_g, ln_x_b, w_out, norm2_g, w_pq, sub_keys, S):
    partner = _rope_partner()
    vperm_heads = jnp.array([0, 2, 1, 3, 4, 6, 5, 7])
    vcols = (vperm_heads[:, None] * HEAD_N + jnp.arange(HEAD_N)[None, :]).reshape(-1) + 2 * C_R

    def with_vperm(m):
        return jnp.concatenate([m, m[..., vcols]], axis=-1)

    wz = with_vperm(w_in[:, OFF_RWKV:])
    wkr = w_in[:, OFF_KR:OFF_RWKV]
    win = jnp.concatenate([
        w_in[:, OFF_Q:OFF_KV], w_in[:, OFF_KV:OFF_KR], wz,
        _place(wkr, LANES, NOPE_DIM), _place(wkr[:, partner], LANES, NOPE_DIM)], axis=1).astype(bf16)

    wq = w_uq.reshape(Q_LORA, H_A, QK_DIM)
    wqa = jnp.pad(wq, ((0, 0), (0, 0), (0, HEAD_PAD - QK_DIM))).reshape(Q_LORA, H_A * HEAD_PAD).astype(bf16)
    wqb = jnp.pad(wq[:, :, NOPE_DIM:][:, :, partner],
                  ((0, 0), (0, 0), (NOPE_DIM, HEAD_PAD - QK_DIM))).reshape(Q_LORA, H_A * HEAD_PAD).astype(bf16)
    wkv = w_ukv.reshape(KV_LORA, H_A, NOPE_DIM + V_DIM)
    wkn = jnp.pad(wkv[:, :, :NOPE_DIM], ((0, 0), (0, 0), (0, HEAD_PAD - NOPE_DIM))).reshape(
        KV_LORA, H_A * HEAD_PAD).astype(bf16)
    wv = jnp.pad(wkv[:, :, NOPE_DIM:], ((0, 0), (0, 0), (0, LANES - V_DIM))).reshape(
        KV_LORA, H_A * LANES).astype(bf16)

    def gain_rows(g):
        ga = _place(g[None, :], LANES, 0)
        gb = _place(g[None, NOPE_DIM:][:, partner], LANES, NOPE_DIM)
        return ga, gb

    gqa, gqb = gain_rows(q_norm_g)
    gq = jnp.concatenate([gqa, gqb, jnp.zeros((6, LANES), f32)], axis=0)
    gkn = _place(k_norm_g[None, :NOPE_DIM], LANES, 0)
    gka = _place(k_norm_g[None, NOPE_DIM:], LANES, NOPE_DIM)
    gkb = _place(k_norm_g[None, NOPE_DIM:][:, partner], LANES, NOPE_DIM)
    gk = jnp.concatenate([gkn, gka, gkb, jnp.zeros((5, LANES), f32)], axis=0)

    half = ROPE_DIM // 2
    inv = 1.0 / (ROPE_THETA ** (jnp.arange(half, dtype=f32) / half))
    ang = jnp.arange(S, dtype=f32)[:, None] * inv[None, :]
    c, s = jnp.cos(ang), jnp.sin(ang)
    cos = jnp.concatenate([jnp.ones((S, NOPE_DIM), f32), c, c, jnp.zeros((S, HEAD_PAD - QK_DIM), f32)], axis=1)
    sin = jnp.concatenate([jnp.zeros((S, NOPE_DIM), f32), -s, s, jnp.zeros((S, HEAD_PAD - QK_DIM), f32)], axis=1)

    zeros = jnp.zeros((W_LORA, C_R), f32)
    wl = jnp.stack([jnp.concatenate([jnp.concatenate([w_up[d], zeros], axis=1),
                                     jnp.concatenate([zeros, a_up[d]], axis=1)], axis=0) for d in range(2)]).astype(bf16)
    seg = jnp.arange(C_R) // HEAD_N
    jseg = (seg[:, None] == seg[None, :]).astype(bf16)
    lane_head = jnp.arange(LANES) // HEAD_N
    jsa = (lane_head[:, None] == lane_head[None, :]).astype(bf16)
    jy = (lane_head[:, None] == (jnp.arange(2 * LANES) // LANES)[None, :]).astype(bf16)

    return dict(
        g1=norm1_g[None, :], win=win, qlg=q_lat_g[None, :], wqa=wqa, wqb=wqb, kvlg=kv_lat_g[None, :], wkn=wkn, wv=wv,
        cos=cos, sin=sin, gq=gq, gk=gk,
        mup=with_vperm(mu_prev)[None, :], mun=with_vperm(mu_next)[None, :], wl=wl, w0=w0, a0=a0,
        gup=g_up.astype(bf16), kk=k_k[None, :], ka=k_a[None, :], jseg=jseg, jsa=jsa, jy=jy,
        lng=ln_x_g[None, :], lnb=ln_x_b[None, :], rk=r_k[None, :], aog=attn_out_g[None, :],
        wout=w_out.astype(bf16), g2=norm2_g[None, :], wpq=w_pq.astype(bf16), sk=sub_keys.astype(bf16),
    )


def _tile(n, pref):
    t = min(n, pref)
    assert n % t == 0, (n, t)
    return t


PEER_CHUNK = 4096


def _layer(x, w, expert_u, expert_v):
    B, S, _ = x.shape
    T = B * S
    xt = x.reshape(T, D_MODEL)
    tm = _tile(S, 256)
    q, k, v, z = _inproj(xt, w["g1"], w["win"], w["qlg"], w["wqa"], w["wqb"], w["kvlg"], w["wkn"], w["wv"],
                         w["cos"], w["sin"], w["gq"], w["gk"], B=B, S=S, tm=tm)
    attn = _attention(q, k, v, B=B, S=S, tq=_tile(S, 512), tk=_tile(S // 2, 512))
    r, vr, vp, nkk, g, decf, kkaf, kf, decb, kkab, kb = _rwkv_prep(
        z, w["mup"], w["mun"], w["wl"], w["w0"], w["a0"], w["gup"], w["kk"], w["ka"], w["jseg"], B=B, S=S, tm=tm)
    yf, yb = _wkv_scan(r, nkk, vp, decf, kkaf, kf, decb, kkab, kb, w["jsa"], w["jy"], B=B, S=S, tc=LANES)
    h = _mix(xt, attn, yf, yb, r, kf, kb, vr, g, w["lng"], w["lnb"], w["rk"], w["aog"], w["wout"], w["jseg"],
             B=B, S=S, tm=tm)
    idx, gate = _route(h, w["g2"], w["wpq"], w["sk"], T=T, tm=_tile(T, 256))
    pk = P_HEADS * P_TOPK
    tc = _tile(T, PEER_CHUNK)
    pu = _pack_rows(expert_u)
    pv = _pack_rows(expert_v)
    outs = []
    for c in range(T // tc):
        sl = slice(c * tc, (c + 1) * tc)
        flat = idx[sl].reshape(tc * pk)
        gu = _gather_rows(pu, flat)
        gv = _gather_rows(pv, flat)
        outs.append(_experts(h[sl], w["g2"], gate[sl], gu, gv, tb=_tile(tc, 16)))
    return jnp.concatenate(outs, axis=0).reshape(B, S, D_MODEL)


def kernel(x_prompt, x_sample, norm1_g, w_in, q_lat_g, w_uq, kv_lat_g, w_ukv, q_norm_g, k_norm_g, attn_out_g, mu_prev, mu_next, w0, w_up, a0, a_up, g_up, k_k, k_a, r_k, ln_x_g, ln_x_b, w_out, norm2_g, w_pq, sub_keys, expert_u, expert_v):
    assert x_prompt.shape[1] == x_sample.shape[1]
    S = x_prompt.shape[1]
    nb = x_prompt.shape[0]
    x = jnp.concatenate([x_prompt, x_sample], axis=0)
    for l in range(norm1_g.shape[0]):
        w = _prepare(norm1_g[l], w_in[l], q_lat_g[l], w_uq[l], kv_lat_g[l], w_ukv[l], q_norm_g[l], k_norm_g[l],
                     attn_out_g[l], mu_prev[l], mu_next[l], w0[l], w_up[l], a0[l], a_up[l], g_up[l], k_k[l], k_a[l],
                     r_k[l], ln_x_g[l], ln_x_b[l], w_out[l], norm2_g[l], w_pq[l], sub_keys[l], S)
        x = _layer(x, w, expert_u[l], expert_v[l])
    return x[:nb], x[nb:]
```

```python
import functools
import math

import jax
import jax.numpy as jnp
from jax import lax
from jax.experimental import pallas as pl
from jax.experimental.pallas import tpu as pltpu
from jax.experimental.pallas import tpu_sc as plsc

D_MODEL = 1024
H_A = 8
NOPE_DIM = 64
ROPE_DIM = 32
QK_DIM = NOPE_DIM + ROPE_DIM
V_DIM = 64
Q_LORA = 384
KV_LORA = 256
ROPE_THETA = 10000.0
H_R = 8
HEAD_N = 64
C_R = H_R * HEAD_N
W_LORA = 64
A_LORA = 64
G_LORA = 128
GN_EPS = 64e-5
RWKV_IN = 3 * C_R + W_LORA + A_LORA + G_LORA
OFF_Q = 0
OFF_KV = OFF_Q + Q_LORA
OFF_KR = OFF_KV + KV_LORA
OFF_RWKV = OFF_KR + ROPE_DIM
N_KEYS = 128
N_EXPERTS = N_KEYS * N_KEYS
P_HEADS = 8
P_TOPK = 16
D_KEY = 256
HALF_KEY = D_KEY // 2
NORM_EPS = 1e-6

LANES = 128
HEAD_PAD = LANES
Z_COLS = RWKV_IN
IN_PAD = Q_LORA + KV_LORA + Z_COLS + 2 * LANES
VMEM_LIMIT = 56 * 1024 * 1024

f32 = jnp.float32
bf16 = jnp.bfloat16


def _rms(x, g):
    return x * lax.rsqrt(jnp.mean(x * x, axis=-1, keepdims=True) + NORM_EPS) * g


def _dot(a, b):
    return jnp.dot(a, b, preferred_element_type=f32)


def _dot_nt(a, b):
    return lax.dot_general(a, b, (((1,), (1,)), ((), ())), preferred_element_type=f32)


def _split_dot(x, j):
    hi = x.astype(bf16)
    lo = (x - hi.astype(f32)).astype(bf16)
    return _dot(hi, j) + _dot(lo, j)


def _inproj_kernel(x_ref, g1_ref, win_ref, qlg_ref, wqa_ref, wqb_ref, kvlg_ref, wkn_ref, wv_ref,
                   cos_ref, sin_ref, gq_ref, gk_ref,
                   q_ref, k_ref, v_ref, z_ref):
    x = x_ref[...]
    xn = _rms(x, g1_ref[...]).astype(bf16)
    proj = _dot(xn, win_ref[...])
    o_kv, o_z, o_ra, o_rb = Q_LORA, Q_LORA + KV_LORA, Q_LORA + KV_LORA + Z_COLS, Q_LORA + KV_LORA + Z_COLS + LANES
    z_ref[...] = proj[:, o_z:o_ra]
    cos = cos_ref[...]
    sin = sin_ref[...]

    ql = _rms(proj[:, :o_kv], qlg_ref[...]).astype(bf16)
    qa = _dot(ql, wqa_ref[...])
    qb = _dot(ql, wqb_ref[...])
    ga, gb = gq_ref[0:1, :], gq_ref[1:2, :]
    scale = QK_DIM ** -0.5 * math.log2(math.e)
    for h in range(H_A):
        a = qa[:, h * LANES:(h + 1) * LANES]
        b = qb[:, h * LANES:(h + 1) * LANES]
        s = lax.rsqrt(jnp.sum(a * a, axis=-1, keepdims=True) * (1.0 / QK_DIM) + NORM_EPS) * scale
        q_ref[0, h] = (s * (a * ga * cos + b * gb * sin)).astype(bf16)

    kvl = _rms(proj[:, o_kv:o_z], kvlg_ref[...]).astype(bf16)
    kn = _dot(kvl, wkn_ref[...])
    vv = _dot(kvl, wv_ref[...])
    one = (lax.broadcasted_iota(jnp.int32, (1, LANES), 1) == V_DIM).astype(f32)
    for h in range(H_A):
        v_ref[0, h] = (vv[:, h * LANES:(h + 1) * LANES] + one).astype(bf16)
    kra = proj[:, o_ra:o_rb]
    krb = proj[:, o_rb:o_rb + LANES]
    gkn, gka, gkb = gk_ref[0:1, :], gk_ref[1:2, :], gk_ref[2:3, :]
    kr = kra * gka * cos + krb * gkb * sin
    ssr = jnp.sum(kra * kra, axis=-1, keepdims=True)
    for h in range(H_A):
        a = kn[:, h * LANES:(h + 1) * LANES]
        s = lax.rsqrt((jnp.sum(a * a, axis=-1, keepdims=True) + ssr) * (1.0 / QK_DIM) + NORM_EPS)
        k_ref[0, h] = (s * (a * gkn + kr)).astype(bf16)


def _inproj(x, g1, win, qlg, wqa, wqb, kvlg, wkn, wv, cos, sin, gq, gk, *, B, S, tm):
    nS = S // tm
    const = lambda shape: pl.BlockSpec(shape, lambda b, s: (0,) * len(shape))
    return pl.pallas_call(
        _inproj_kernel,
        grid=(B, nS),
        in_specs=[
            pl.BlockSpec((tm, D_MODEL), lambda b, s: (b * nS + s, 0)),
            const((1, D_MODEL)), const(win.shape), const((1, Q_LORA)), const(wqa.shape), const(wqb.shape),
            const((1, KV_LORA)), const(wkn.shape), const(wv.shape),
            pl.BlockSpec((tm, LANES), lambda b, s: (s, 0)),
            pl.BlockSpec((tm, LANES), lambda b, s: (s, 0)),
            const(gq.shape), const(gk.shape),
        ],
        out_specs=[
            pl.BlockSpec((1, H_A, tm, HEAD_PAD), lambda b, s: (b, 0, s, 0)),
            pl.BlockSpec((1, H_A, tm, HEAD_PAD), lambda b, s: (b, 0, s, 0)),
            pl.BlockSpec((1, H_A, tm, LANES), lambda b, s: (b, 0, s, 0)),
            pl.BlockSpec((tm, Z_COLS), lambda b, s: (b * nS + s, 0)),
        ],
        out_shape=[
            jax.ShapeDtypeStruct((B, H_A, S, HEAD_PAD), bf16),
            jax.ShapeDtypeStruct((B, H_A, S, HEAD_PAD), bf16),
            jax.ShapeDtypeStruct((B, H_A, S, LANES), bf16),
            jax.ShapeDtypeStruct((B * S, Z_COLS), f32),
        ],
        compiler_params=pltpu.CompilerParams(
            dimension_semantics=("parallel", "parallel"), vmem_limit_bytes=VMEM_LIMIT),
        name="in_proj",
    )(x, g1, win, qlg, wqa, wqb, kvlg, wkn, wv, cos, sin, gq, gk)


ATTN_ROWS = 32
ATTN_UNROLL = 16


def _attn_kernel(q_ref, k_ref, v_ref, o_ref, s_ref, p_ref, m_ref, al_ref, acc_ref, *, tk, nk):
    tq = q_ref.shape[2]
    nlt = tk // LANES

    def scores(h, q, j, buf):
        off = pl.multiple_of(jnp.minimum(j, nk - 1) * tk, tk)
        s_ref[buf] = _dot_nt(q, k_ref[0, h, pl.ds(off, tk), :])

    def softmax(buf):
        def chunk(c, carry):
            rows = pl.ds(pl.multiple_of(c * ATTN_ROWS, ATTN_ROWS), ATTN_ROWS)
            tiles = [s_ref[buf, rows, lt * LANES:(lt + 1) * LANES] for lt in range(nlt)]
            best = functools.reduce(jnp.maximum, tiles)
            m_old = m_ref[rows, :]
            m_new = jnp.maximum(m_old, jnp.broadcast_to(jnp.max(best, axis=-1, keepdims=True), m_old.shape))
            for lt in range(nlt):
                p_ref[buf, rows, lt * LANES:(lt + 1) * LANES] = jnp.exp2(tiles[lt] - m_new).astype(bf16)
            al_ref[rows, :] = jnp.exp2(m_old - m_new)
            m_ref[rows, :] = m_new
            return carry

        lax.fori_loop(0, tq // ATTN_ROWS, chunk, 0, unroll=ATTN_UNROLL)

    def accumulate(h, j, buf):
        off = pl.multiple_of(j * tk, tk)
        acc_ref[...] = al_ref[...] * acc_ref[...] + _dot(p_ref[buf], v_ref[0, h, pl.ds(off, tk), :])

    outs = []
    for h in range(2):
        q = q_ref[0, h]
        m_ref[...] = jnp.full(m_ref.shape, -jnp.inf, f32)
        acc_ref[...] = jnp.zeros(acc_ref.shape, f32)
        scores(h, q, 0, 0)

        def body(jj, carry, h=h, q=q):
            j = 2 * jj
            scores(h, q, j + 1, 1)
            softmax(0)
            accumulate(h, j, 0)
            scores(h, q, j + 2, 0)
            softmax(1)
            accumulate(h, j + 1, 1)
            return carry

        lax.fori_loop(0, nk // 2, body, 0)
        acc = acc_ref[...]
        outs.append(acc / acc[:, V_DIM:V_DIM + 1])
    lane = lax.broadcasted_iota(jnp.int32, (tq, LANES), 1)
    o_ref[0, 0] = jnp.where(lane < V_DIM, outs[0], pltpu.roll(outs[1], V_DIM, 1))


def _attention(q, k, v, *, B, S, tq, tk):
    return pl.pallas_call(
        functools.partial(_attn_kernel, tk=tk, nk=S // tk),
        grid=(B, H_A // 2, S // tq),
        in_specs=[
            pl.BlockSpec((1, 2, tq, HEAD_PAD), lambda b, p, i: (b, p, i, 0)),
            pl.BlockSpec((1, 2, S, HEAD_PAD), lambda b, p, i: (b, p, 0, 0)),
            pl.BlockSpec((1, 2, S, LANES), lambda b, p, i: (b, p, 0, 0)),
        ],
        out_specs=pl.BlockSpec((1, 1, tq, LANES), lambda b, p, i: (b, p, i, 0)),
        out_shape=jax.ShapeDtypeStruct((B, H_A // 2, S, LANES), f32),
        scratch_shapes=[pltpu.VMEM((2, tq, tk), f32), pltpu.VMEM((2, tq, tk), bf16), pltpu.VMEM((tq, LANES), f32),
                        pltpu.VMEM((tq, LANES), f32), pltpu.VMEM((tq, LANES), f32)],
        compiler_params=pltpu.CompilerParams(
            dimension_semantics=("parallel", "parallel", "parallel"), vmem_limit_bytes=VMEM_LIMIT),
        name="attention",
    )(q, k, v)


def _softplus(x):
    return jnp.maximum(x, 0.0) + jnp.log1p(jnp.exp(-jnp.abs(x)))


def _prep_kernel(z_ref, zp_ref, zn_ref, mup_ref, mun_ref, wl_ref, w0_ref, a0_ref, gup_ref, kk_ref, ka_ref, jseg_ref,
                 r_ref, v_ref, nkk_ref, g_ref,
                 decf_ref, kkaf_ref, kf_ref, decb_ref, kkab_ref, kb_ref):
    si = pl.program_id(1)
    ns = pl.num_programs(1)
    z = z_ref[...]
    tm = z.shape[0]
    row = lax.broadcasted_iota(jnp.int32, (tm, 1), 0)
    zp_row = jnp.where(si == 0, 0.0, zp_ref[7:8, :])
    zn_row = jnp.where(si == ns - 1, 0.0, zn_ref[0:1, :])
    z_prev = jnp.where(row == 0, zp_row, pltpu.roll(z, 1, 0))
    z_next = jnp.where(row == tm - 1, zn_row, pltpu.roll(z, tm - 1, 0))
    zm = z + mup_ref[...] * (z_prev - z) + mun_ref[...] * (z_next - z)

    o1, o2, o3 = C_R, 2 * C_R, 3 * C_R
    o4 = o3 + W_LORA + A_LORA
    o5 = o4 + G_LORA
    r = zm[:, :o1]
    kr = zm[:, o1:o2]
    vr = zm[:, o2:o3]
    lw = zm[:, o3:o4]
    zg = zm[:, o4:o5]
    r_ref[...] = r
    v_ref[...] = vr
    g_ref[...] = _dot(jax.nn.sigmoid(zg).astype(bf16), gup_ref[...])

    kk = kr * kk_ref[...]
    ssk = _split_dot(kk * kk, jseg_ref[...])
    kk = kk / jnp.maximum(jnp.sqrt(ssk), 1e-12)
    nkk_ref[...] = -kk

    lane = lax.broadcasted_iota(jnp.int32, lw.shape, 1)
    lin = jnp.where(lane < W_LORA, jnp.tanh(lw), lw).astype(bf16)
    ka = ka_ref[...]
    for d, (dec_ref, kka_ref, kd_ref) in enumerate(((decf_ref, kkaf_ref, kf_ref), (decb_ref, kkab_ref, kb_ref))):
        lo = _dot(lin, wl_ref[d])
        w = -_softplus(-(w0_ref[d:d + 1, :] + lo[:, :C_R])) - 0.5
        dec_ref[...] = jnp.exp(-jnp.exp(w))
        a = jax.nn.sigmoid(a0_ref[d:d + 1, :] + lo[:, C_R:])
        kd_ref[...] = kr * (1.0 + (a - 1.0) * ka)
        kka_ref[...] = kk * a


def _rwkv_prep(z, mup, mun, wl, w0, a0, gup, kk, ka, jseg, *, B, S, tm):
    nS = S // tm
    T = B * S
    nb8 = tm // 8
    const = lambda shape: pl.BlockSpec(shape, lambda b, s: (0,) * len(shape))
    row_spec = pl.BlockSpec((tm, C_R), lambda b, s: (b * nS + s, 0))
    return pl.pallas_call(
        _prep_kernel,
        grid=(B, nS),
        in_specs=[
            pl.BlockSpec((tm, Z_COLS), lambda b, s: (b * nS + s, 0)),
            pl.BlockSpec((8, Z_COLS), lambda b, s: (jnp.maximum((b * nS + s) * nb8 - 1, 0), 0)),
            pl.BlockSpec((8, Z_COLS), lambda b, s: (jnp.minimum((b * nS + s + 1) * nb8, T // 8 - 1), 0)),
            const((1, Z_COLS)), const((1, Z_COLS)), const(wl.shape), const(w0.shape), const(a0.shape),
            const(gup.shape), const((1, C_R)), const((1, C_R)), const(jseg.shape),
        ],
        out_specs=[row_spec] * 10,
        out_shape=[jax.ShapeDtypeStruct((T, C_R), f32)] * 10,
        compiler_params=pltpu.CompilerParams(
            dimension_semantics=("parallel", "parallel"), vmem_limit_bytes=VMEM_LIMIT),
        name="rwkv_prep",
    )(z, z, z, mup, mun, wl, w0, a0, gup, kk, ka, jseg)


N_PAIRS = H_R // 2
CHUNK = 64


def _halves(x, low):
    zero = jnp.zeros_like(x)
    return jnp.concatenate([jnp.where(low, x, zero), jnp.where(low, zero, x)], axis=0)


def _chunks_kernel(r_ref, nkk_ref, dec_ref, kka_ref, k_ref, tri_ref,
                   at_ref, rt_ref, aak_ref, tt_ref, arb_ref, ark_ref, bt_ref, kt_ref, wc_ref, *, reverse):
    lane = lax.broadcasted_iota(jnp.int32, (CHUNK, LANES), 1)
    low = lane < HEAD_N
    jj = lax.broadcasted_iota(jnp.int32, (CHUNK, LANES), 0)
    ii = lane & (HEAD_N - 1)
    strict = (ii > jj) if reverse else (ii < jj)
    incl = (ii >= jj) if reverse else (ii <= jj)
    rr = lax.broadcasted_iota(jnp.int32, (LANES, LANES), 0)
    cc = lax.broadcasted_iota(jnp.int32, (LANES, LANES), 1)
    eye = (rr == cc).astype(f32)
    last = 0 if reverse else CHUNK - 1
    swap = lambda x: pltpu.roll(x, HEAD_N, 1)
    zero = jnp.zeros((CHUNK, LANES), f32)
    units = [(slice(q * CHUNK, (q + 1) * CHUNK), slice(p * LANES, (p + 1) * LANES))
             for q in range(r_ref.shape[0] // CHUNK) for p in range(N_PAIRS)]
    cums = []
    for rows, cols in units:
        lw = jnp.log(dec_ref[rows, cols])
        hi = lw.astype(bf16)
        r1 = lw - hi.astype(f32)
        mid = r1.astype(bf16)
        lo = (r1 - mid.astype(f32)).astype(bf16)
        cums.append((lw, _dot(tri_ref[...], jnp.concatenate([hi, mid, lo], axis=0))))
    outs = []
    for (rows, cols), (lw, cum) in zip(units, cums):
        w_in = jnp.exp(cum)
        w_ex = jnp.exp(cum - lw)
        w_inv = jnp.exp(-cum)
        wc = w_in[last:last + 1, :]
        at = nkk_ref[rows, cols] * w_ex
        rt = r_ref[rows, cols] * w_in
        bt = kka_ref[rows, cols] * w_inv
        kt = k_ref[rows, cols] * w_inv
        at_ref[rows, cols] = at.astype(bf16)
        rt_ref[rows, cols] = rt.astype(bf16)
        bt_ref[rows, cols] = (bt * wc).astype(bf16)
        kt_ref[rows, cols] = (kt * wc).astype(bf16)
        wc_ref[rows, cols] = jnp.broadcast_to(wc, (CHUNK, LANES))
        lhs = jnp.concatenate([_halves(at, low), _halves(rt, low)], axis=0).astype(bf16)
        rhs = jnp.concatenate([bt, kt], axis=0).astype(bf16)
        outs.append(_dot_nt(lhs, rhs))
    pws, ts = [], []
    for (rows, cols), o in zip(units, outs):
        o0, o1, o2, o3 = (o[q * CHUNK:(q + 1) * CHUNK, :] for q in range(4))
        aab = jnp.where(strict, jnp.where(low, o0, swap(o1)), zero)
        aak_ref[rows, cols] = jnp.where(strict, jnp.where(low, swap(o0), o1), zero).astype(bf16)
        arb_ref[rows, cols] = jnp.where(incl, jnp.where(low, o2, swap(o3)), zero).astype(bf16)
        ark_ref[rows, cols] = jnp.where(incl, jnp.where(low, swap(o2), o3), zero).astype(bf16)
        pws.append(_halves(aab, low))
        ts.append(eye + pws[-1])
    for _ in range(int(math.log2(CHUNK)) - 1):
        pbs = [pw.astype(bf16) for pw in pws]
        pws = [_dot(pb, pb) for pb in pbs]
        ts = [t + _dot(t.astype(bf16), pw.astype(bf16)) for t, pw in zip(ts, pws)]
    for (rows, cols), t in zip(units, ts):
        tt_ref[rows, cols] = jnp.where(low, t[:CHUNK, :], t[CHUNK:, :]).astype(bf16)


CHUNKS_PER_STEP = 4


def _wkv_chunks(r, nkk, dec, kka, k, *, T, reverse):
    tri_i = jnp.arange(CHUNK)
    tri = (tri_i[None, :] >= tri_i[:, None]) if reverse else (tri_i[None, :] <= tri_i[:, None])
    tri3 = jnp.tile(tri.astype(bf16), (1, 3))
    rows = CHUNK * CHUNKS_PER_STEP
    assert T % rows == 0
    row = pl.BlockSpec((rows, C_R), lambda i: (i, 0))
    return pl.pallas_call(
        functools.partial(_chunks_kernel, reverse=reverse),
        grid=(T // rows,),
        in_specs=[row] * 5 + [pl.BlockSpec(tri3.shape, lambda i: (0, 0))],
        out_specs=[row] * 9,
        out_shape=[jax.ShapeDtypeStruct((T, C_R), bf16)] * 8 + [jax.ShapeDtypeStruct((T, C_R), f32)],
        compiler_params=pltpu.CompilerParams(dimension_semantics=("parallel",), vmem_limit_bytes=VMEM_LIMIT),
        name="wkv_chunks",
    )(r, nkk, dec, kka, k, tri3)


def _chunk_scan_kernel(vf_ref, vb_ref, *refs, nb):
    vs = (vf_ref, vb_ref)
    ins = (refs[0:9], refs[9:18])
    ys = refs[18:20]
    st_ref = refs[20]

    @pl.when(pl.program_id(1) == 0)
    def _():
        st_ref[...] = jnp.zeros_like(st_ref)

    low = lax.broadcasted_iota(jnp.int32, (CHUNK, LANES), 1) < HEAD_N
    rr = lax.broadcasted_iota(jnp.int32, (LANES, LANES), 0) < HEAD_N
    cc = lax.broadcasted_iota(jnp.int32, (LANES, LANES), 1) < HEAD_N
    same_head = rr == cc
    chains = [(bb, d, p) for bb in range(nb) for d in range(2) for p in range(N_PAIRS)]
    part = lambda bb, d, p, which: ins[d][which][bb, :, p * LANES:(p + 1) * LANES]
    s0s = [st_ref[(bb * 2 + d) * N_PAIRS + p] for bb, d, p in chains]
    vals = [vs[d][bb, :, p * LANES:(p + 1) * LANES] for bb, d, p in chains]
    vbds = [_halves(v, low).astype(bf16) for v in vals]
    gs = [_dot(jnp.concatenate([part(*ch, 0), part(*ch, 1)], axis=0), s0.astype(bf16))
          for ch, s0 in zip(chains, s0s)]
    xs = [g[:CHUNK, :] + _dot(part(*ch, 2), vbd) for ch, g, vbd in zip(chains, gs, vbds)]
    sas = [_dot(part(*ch, 3), _halves(x, low).astype(bf16)) for ch, x in zip(chains, xs)]
    for (bb, d, p), g, sa, vbd in zip(chains, gs, sas, vbds):
        ys[d][bb, :, p * LANES:(p + 1) * LANES] = (
            g[CHUNK:, :] + _dot(part(bb, d, p, 4), _halves(sa, low).astype(bf16)) + _dot(part(bb, d, p, 5), vbd))
    for (bb, d, p), s0, sa, v in zip(chains, s0s, sas, vals):
        upd = lax.dot_general(
            jnp.concatenate([part(bb, d, p, 6), part(bb, d, p, 7)], axis=0),
            jnp.concatenate([sa, v], axis=0).astype(bf16),
            (((0,), (0,)), ((), ())), preferred_element_type=f32)
        wc = part(bb, d, p, 8)
        wcol = jnp.concatenate([wc, wc], axis=0).T
        st_ref[(bb * 2 + d) * N_PAIRS + p] = jnp.where(same_head, wcol * s0 + upd, jnp.zeros_like(s0))


CHUNK_SCAN_BATCH = 4


def _chunk_scan(v, fwd_parts, bwd_parts, *, B, S):
    nC = S // CHUNK
    nb = CHUNK_SCAN_BATCH if B % CHUNK_SCAN_BATCH == 0 else 1
    as3 = lambda a: a.reshape(B, S, C_R)
    fwd = pl.BlockSpec((nb, CHUNK, C_R), lambda b, i: (b, i, 0))
    bwd = pl.BlockSpec((nb, CHUNK, C_R), lambda b, i: (b, nC - 1 - i, 0))
    yf, yb = pl.pallas_call(
        functools.partial(_chunk_scan_kernel, nb=nb),
        grid=(B // nb, nC),
        in_specs=[fwd, bwd] + [fwd] * 9 + [bwd] * 9,
        out_specs=[fwd, bwd],
        out_shape=[jax.ShapeDtypeStruct((B, S, C_R), f32)] * 2,
        scratch_shapes=[pltpu.VMEM((nb * 2 * N_PAIRS, LANES, LANES), f32)],
        compiler_params=pltpu.CompilerParams(
            dimension_semantics=("parallel", "arbitrary"), vmem_limit_bytes=VMEM_LIMIT),
        name="wkv_scan",
    )(as3(v), as3(v), *[as3(a) for a in fwd_parts], *[as3(a) for a in bwd_parts])
    return yf.reshape(B * S, C_R), yb.reshape(B * S, C_R)

def _mix_kernel(x_ref, attn_ref, yf_ref, yb_ref, r_ref, kf_ref, kb_ref, v_ref, g_ref,
                lng_ref, lnb_ref, rk_ref, aog_ref, wout_ref, jseg_ref, h_ref):
    jseg = jseg_ref[...]
    y = yf_ref[...] + yb_ref[...]
    mu = _split_dot(y, jseg) * (1.0 / HEAD_N)
    dlt = y - mu
    var = _split_dot(dlt * dlt, jseg) * (1.0 / HEAD_N)
    yn = dlt * lax.rsqrt(var + GN_EPS) * lng_ref[...] + lnb_ref[...]
    r = r_ref[...]
    k_mean = 0.5 * (kf_ref[...] + kb_ref[...])
    bonus = _split_dot(r * k_mean * rk_ref[...], jseg) * v_ref[...]
    rw = ((yn + bonus) * g_ref[...]).astype(bf16)
    attn = jnp.concatenate([attn_ref[0, p] for p in range(H_A // 2)], axis=-1)
    an = _rms(attn, aog_ref[...]).astype(bf16)
    ha = H_A * V_DIM
    h_ref[...] = x_ref[...] + _dot(an, wout_ref[0:ha, :]) + _dot(rw, wout_ref[ha:, :])


def _mix(x, attn, yf, yb, r, kf, kb, v, g, lng, lnb, rk, aog, wout, jseg, *, B, S, tm):
    nS = S // tm
    const = lambda shape: pl.BlockSpec(shape, lambda b, s: (0,) * len(shape))
    row = lambda c: pl.BlockSpec((tm, c), lambda b, s: (b * nS + s, 0))
    return pl.pallas_call(
        _mix_kernel,
        grid=(B, nS),
        in_specs=[row(D_MODEL), pl.BlockSpec((1, H_A // 2, tm, LANES), lambda b, s: (b, 0, s, 0))]
                 + [row(C_R)] * 7
                 + [const((1, C_R))] * 3 + [const((1, H_A * V_DIM)), const(wout.shape), const(jseg.shape)],
        out_specs=row(D_MODEL),
        out_shape=jax.ShapeDtypeStruct((B * S, D_MODEL), f32),
        compiler_params=pltpu.CompilerParams(
            dimension_semantics=("parallel", "parallel"), vmem_limit_bytes=VMEM_LIMIT),
        name="mix",
    )(x, attn, yf, yb, r, kf, kb, v, g, lng, lnb, rk, aog, wout, jseg)


def _take_max(s, ids):
    m = jnp.max(s, axis=0, keepdims=True)
    pick = jnp.min(jnp.where(s == m, ids, jnp.int32(2 ** 30)), axis=0, keepdims=True)
    return m, pick, jnp.where(ids == pick, -jnp.inf, s)


CAND_COLS = tuple(P_TOPK // (a + 1) for a in range(P_TOPK))
N_CAND = sum(CAND_COLS)
N_CAND_PAD = -(-N_CAND // 8) * 8


def _route_kernel(h_ref, g2_ref, wpq_ref, sk_ref, pos_ref, idx_ref, gate_ref, v1_ref, i1_ref, v2_ref, i2_ref, cs_ref,
                  ci_ref, bs_ref, bi_ref):
    tm = h_ref.shape[0]
    hn = _rms(h_ref[...], g2_ref[...]).astype(bf16)
    pq = _dot(hn, wpq_ref[...])
    key_ids = lax.broadcasted_iota(jnp.int32, (N_KEYS, tm), 0)
    pos_ids = pos_ref[...]
    for p in range(P_HEADS):
        for side, (vs_ref, is_ref) in enumerate(((v1_ref, i1_ref), (v2_ref, i2_ref))):
            qh = pq[:, p * D_KEY + side * HALF_KEY:p * D_KEY + (side + 1) * HALF_KEY].astype(bf16)
            s = _dot_nt(sk_ref[side], qh)
            for j in range(P_TOPK):
                m, pick, s = _take_max(s, key_ids)
                vs_ref[j:j + 1, :] = m
                is_ref[j:j + 1, :] = pick
        v2 = v2_ref[...]
        i2 = i2_ref[...]
        off = 0
        for a, nb in enumerate(CAND_COLS):
            cs_ref[off:off + nb, :] = v1_ref[a:a + 1, :] + v2[:nb, :]
            ci_ref[off:off + nb, :] = i1_ref[a:a + 1, :] * N_KEYS + i2[:nb, :]
            off += nb
        cs_ref[N_CAND:, :] = jnp.full((N_CAND_PAD - N_CAND, tm), -jnp.inf, f32)
        ci_ref[N_CAND:, :] = jnp.zeros((N_CAND_PAD - N_CAND, tm), jnp.int32)
        s = cs_ref[...]
        ci = ci_ref[...]
        for j in range(P_TOPK):
            m, pick, s = _take_max(s, pos_ids)
            bs_ref[j:j + 1, :] = m
            bi_ref[p * P_TOPK + j:p * P_TOPK + j + 1, :] = jnp.max(
                jnp.where(pos_ids == pick, ci, -1), axis=0, keepdims=True)
        bs = bs_ref[...]
        e = jnp.exp(bs - bs[0:1, :])
        gate_ref[:, p * P_TOPK:(p + 1) * P_TOPK] = (e / jnp.sum(e, axis=0, keepdims=True)).T
    idx_ref[...] = bi_ref[...].T


def _route(h, g2, wpq, sk, *, T, tm):
    const = lambda shape: pl.BlockSpec(shape, lambda i: (0,) * len(shape))
    pk = P_HEADS * P_TOPK
    flat = [a * P_TOPK + b for a, nb in enumerate(CAND_COLS) for b in range(nb)] + [2 ** 20] * (N_CAND_PAD - N_CAND)
    pos = jnp.broadcast_to(jnp.array(flat, jnp.int32)[:, None], (N_CAND_PAD, tm))
    return pl.pallas_call(
        _route_kernel,
        grid=(T // tm,),
        in_specs=[pl.BlockSpec((tm, D_MODEL), lambda i: (i, 0)), const((1, D_MODEL)), const(wpq.shape),
                  const(sk.shape), const((N_CAND_PAD, tm))],
        out_specs=[pl.BlockSpec((tm, pk), lambda i: (i, 0))] * 2,
        out_shape=[jax.ShapeDtypeStruct((T, pk), jnp.int32), jax.ShapeDtypeStruct((T, pk), f32)],
        scratch_shapes=[pltpu.VMEM((P_TOPK, tm), f32), pltpu.VMEM((P_TOPK, tm), jnp.int32),
                        pltpu.VMEM((P_TOPK, tm), f32), pltpu.VMEM((P_TOPK, tm), jnp.int32),
                        pltpu.VMEM((N_CAND_PAD, tm), f32), pltpu.VMEM((N_CAND_PAD, tm), jnp.int32),
                        pltpu.VMEM((P_TOPK, tm), f32), pltpu.VMEM((pk, tm), jnp.int32)],
        compiler_params=pltpu.CompilerParams(dimension_semantics=("parallel",), vmem_limit_bytes=VMEM_LIMIT),
        name="route",
    )(h, g2, wpq, sk, pos)


GATHER_WINDOW = 64
SC_CORES = 2
SC_SUBCORES = 16
SC_WORKERS = SC_CORES * SC_SUBCORES


def _gather_rows(table, idx):
    n = idx.shape[0]
    d = table.shape[1]
    win = GATHER_WINDOW
    n_it = n // (SC_WORKERS * win)
    assert n_it * SC_WORKERS * win == n and n_it % 2 == 0, (n, n_it)
    mesh = plsc.VectorSubcoreMesh(core_axis_name="core", subcore_axis_name="subcore")

    @functools.partial(
        pl.kernel, out_type=jax.ShapeDtypeStruct((n, d), table.dtype), mesh=mesh,
        scratch_types=[pltpu.VMEM((n_it, win), jnp.int32), pltpu.VMEM((2, win, d), table.dtype),
                       pltpu.SemaphoreType.DMA, pltpu.SemaphoreType.DMA,
                       pltpu.SemaphoreType.DMA, pltpu.SemaphoreType.DMA])
    def gather(tab_hbm, idx_hbm, out_hbm, idx_v, rows_v, gsem0, gsem1, wsem0, wsem1):
        wid = lax.axis_index("subcore") * SC_CORES + lax.axis_index("core")
        base = wid * (n_it * win)
        gsem = (gsem0, gsem1)
        wsem = (wsem0, wsem1)
        pltpu.sync_copy(idx_hbm.at[wid], idx_v)

        def fetch(j, b):
            return pltpu.make_async_copy(tab_hbm.at[idx_v.at[j]], rows_v.at[b], gsem[b])

        def flush(j, b):
            return pltpu.make_async_copy(rows_v.at[b], out_hbm.at[pl.ds(base + j * win, win)], wsem[b])

        fetch(0, 0).start()

        @pl.loop(0, n_it, step=2)
        def _(j0):
            for b in range(2):
                j = j0 + b

                @pl.when(j >= 1)
                def _():
                    flush(j - 1, 1 - b).wait()

                @pl.when(j + 1 < n_it)
                def _():
                    fetch(j + 1, 1 - b).start()

                fetch(j, b).wait()
                flush(j, b).start()

        flush(n_it - 1, 1).wait()

    return gather(table, idx.reshape(SC_WORKERS, n_it, win))


HALF_D = D_MODEL // 2


def _pack_rows(table):
    bits = lax.bitcast_convert_type(table.astype(bf16), jnp.uint16).astype(jnp.uint32)
    return bits[:, :HALF_D] | (bits[:, HALF_D:] << 16)


def _unpack_rows(words):
    lo = lax.bitcast_convert_type(words << 16, f32).astype(bf16)
    hi = lax.bitcast_convert_type(words & jnp.uint32(0xFFFF0000), f32).astype(bf16)
    return lo, hi


def _expert_kernel(h_ref, g2_ref, gate_ref, gu_ref, gv_ref, o_ref):
    h = h_ref[...]
    tb = h.shape[0]
    pk = P_HEADS * P_TOPK
    hn = _rms(h, g2_ref[...]).astype(bf16)
    row = lax.broadcasted_iota(jnp.int32, (8, pk), 0)
    for grp in range(tb // 8):
        rows = slice(grp * 8, (grp + 1) * 8)
        hn8 = hn[rows, :]
        act = jnp.zeros((8, pk), f32)
        for j in range(8):
            t = grp * 8 + j
            ulo, uhi = _unpack_rows(gu_ref[t * pk:(t + 1) * pk, :])
            act = jnp.where(row == j, _dot_nt(hn8[:, :HALF_D], ulo) + _dot_nt(hn8[:, HALF_D:], uhi), act)
        gelu = 0.5 * act * (1.0 + lax.erf(act * (2.0 ** -0.5)))
        w = (gate_ref[rows, :] * gelu).astype(bf16)
        out_lo = jnp.zeros((8, HALF_D), f32)
        out_hi = jnp.zeros((8, HALF_D), f32)
        for j in range(8):
            t = grp * 8 + j
            vlo, vhi = _unpack_rows(gv_ref[t * pk:(t + 1) * pk, :])
            wj = jnp.where(row == j, w, jnp.zeros_like(w))
            out_lo = out_lo + _dot(wj, vlo)
            out_hi = out_hi + _dot(wj, vhi)
        o_ref[rows, :HALF_D] = h[rows, :HALF_D] + out_lo
        o_ref[rows, HALF_D:] = h[rows, HALF_D:] + out_hi


def _experts(h, g2, gate, gu, gv, *, tb):
    n = h.shape[0]
    pk = P_HEADS * P_TOPK
    const = lambda shape: pl.BlockSpec(shape, lambda i: (0,) * len(shape))
    return pl.pallas_call(
        _expert_kernel,
        grid=(n // tb,),
        in_specs=[pl.BlockSpec((tb, D_MODEL), lambda i: (i, 0)), const((1, D_MODEL)),
                  pl.BlockSpec((tb, pk), lambda i: (i, 0)),
                  pl.BlockSpec((tb * pk, HALF_D), lambda i: (i, 0)),
                  pl.BlockSpec((tb * pk, HALF_D), lambda i: (i, 0))],
        out_specs=pl.BlockSpec((tb, D_MODEL), lambda i: (i, 0)),
        out_shape=jax.ShapeDtypeStruct((n, D_MODEL), f32),
        compiler_params=pltpu.CompilerParams(dimension_semantics=("parallel",), vmem_limit_bytes=VMEM_LIMIT),
        name="experts",
    )(h, g2, gate, gu, gv)


def _place(cols, width, offset):
    return jnp.pad(cols, ((0, 0), (offset, width - offset - cols.shape[1])))


def _rope_partner():
    half = ROPE_DIM // 2
    return jnp.concatenate([jnp.arange(half, ROPE_DIM), jnp.arange(0, half)])


def _prepare(norm1_g, w_in, q_lat_g, w_uq, kv_lat_g, w_ukv, q_norm_g, k_norm_g, attn_out_g, mu_prev, mu_next,
             w0, w_up, a0, a_up, g_up, k_k, k_a, r_k, ln_x_g, ln_x_b, w_out, norm2_g, w_pq, sub_keys, S):
    partner = _rope_partner()
    wz = w_in[:, OFF_RWKV:]
    wkr = w_in[:, OFF_KR:OFF_RWKV]
    win = jnp.concatenate([
        w_in[:, OFF_Q:OFF_KV], w_in[:, OFF_KV:OFF_KR], wz,
        _place(wkr, LANES, NOPE_DIM), _place(wkr[:, partner], LANES, NOPE_DIM)], axis=1).astype(bf16)

    wq = w_uq.reshape(Q_LORA, H_A, QK_DIM)
    wqa = jnp.pad(wq, ((0, 0), (0, 0), (0, HEAD_PAD - QK_DIM))).reshape(Q_LORA, H_A * HEAD_PAD).astype(bf16)
    wqb = jnp.pad(wq[:, :, NOPE_DIM:][:, :, partner],
                  ((0, 0), (0, 0), (NOPE_DIM, HEAD_PAD - QK_DIM))).reshape(Q_LORA, H_A * HEAD_PAD).astype(bf16)
    wkv = w_ukv.reshape(KV_LORA, H_A, NOPE_DIM + V_DIM)
    wkn = jnp.pad(wkv[:, :, :NOPE_DIM], ((0, 0), (0, 0), (0, HEAD_PAD - NOPE_DIM))).reshape(
        KV_LORA, H_A * HEAD_PAD).astype(bf16)
    wv = jnp.pad(wkv[:, :, NOPE_DIM:], ((0, 0), (0, 0), (0, LANES - V_DIM))).reshape(
        KV_LORA, H_A * LANES).astype(bf16)

    def gain_rows(g):
        ga = _place(g[None, :], LANES, 0)
        gb = _place(g[None, NOPE_DIM:][:, partner], LANES, NOPE_DIM)
        return ga, gb

    gqa, gqb = gain_rows(q_norm_g)
    gq = jnp.concatenate([gqa, gqb, jnp.zeros((6, LANES), f32)], axis=0)
    gkn = _place(k_norm_g[None, :NOPE_DIM], LANES, 0)
    gka = _place(k_norm_g[None, NOPE_DIM:], LANES, NOPE_DIM)
    gkb = _place(k_norm_g[None, NOPE_DIM:][:, partner], LANES, NOPE_DIM)
    gk = jnp.concatenate([gkn, gka, gkb, jnp.zeros((5, LANES), f32)], axis=0)

    half = ROPE_DIM // 2
    inv = 1.0 / (ROPE_THETA ** (jnp.arange(half, dtype=f32) / half))
    ang = jnp.arange(S, dtype=f32)[:, None] * inv[None, :]
    c, s = jnp.cos(ang), jnp.sin(ang)
    cos = jnp.concatenate([jnp.ones((S, NOPE_DIM), f32), c, c, jnp.zeros((S, HEAD_PAD - QK_DIM), f32)], axis=1)
    sin = jnp.concatenate([jnp.zeros((S, NOPE_DIM), f32), -s, s, jnp.zeros((S, HEAD_PAD - QK_DIM), f32)], axis=1)

    zeros = jnp.zeros((W_LORA, C_R), f32)
    wl = jnp.stack([jnp.concatenate([jnp.concatenate([w_up[d], zeros], axis=1),
                                     jnp.concatenate([zeros, a_up[d]], axis=1)], axis=0) for d in range(2)]).astype(bf16)
    seg = jnp.arange(C_R) // HEAD_N
    jseg = (seg[:, None] == seg[None, :]).astype(bf16)

    return dict(
        g1=norm1_g[None, :], win=win, qlg=q_lat_g[None, :], wqa=wqa, wqb=wqb, kvlg=kv_lat_g[None, :], wkn=wkn, wv=wv,
        cos=cos, sin=sin, gq=gq, gk=gk,
        mup=mu_prev[None, :], mun=mu_next[None, :], wl=wl, w0=w0, a0=a0,
        gup=g_up.astype(bf16), kk=k_k[None, :], ka=k_a[None, :], jseg=jseg,
        lng=ln_x_g[None, :], lnb=ln_x_b[None, :], rk=r_k[None, :], aog=attn_out_g[None, :],
        wout=w_out.astype(bf16), g2=norm2_g[None, :], wpq=w_pq.astype(bf16), sk=sub_keys.astype(bf16),
    )


def _tile(n, pref):
    t = min(n, pref)
    assert n % t == 0, (n, t)
    return t


PEER_CHUNK = 4096


def _layer(x, w, expert_u, expert_v):
    B, S, _ = x.shape
    T = B * S
    xt = x.reshape(T, D_MODEL)
    tm = _tile(S, 256)
    q, k, v, z = _inproj(xt, w["g1"], w["win"], w["qlg"], w["wqa"], w["wqb"], w["kvlg"], w["wkn"], w["wv"],
                         w["cos"], w["sin"], w["gq"], w["gk"], B=B, S=S, tm=tm)
    attn = _attention(q, k, v, B=B, S=S, tq=_tile(S, 512), tk=_tile(S // 2, 512))
    r, vr, nkk, g, decf, kkaf, kf, decb, kkab, kb = _rwkv_prep(
        z, w["mup"], w["mun"], w["wl"], w["w0"], w["a0"], w["gup"], w["kk"], w["ka"], w["jseg"], B=B, S=S, tm=tm)
    fw = _wkv_chunks(r, nkk, decf, kkaf, kf, T=T, reverse=False)
    bw = _wkv_chunks(r, nkk, decb, kkab, kb, T=T, reverse=True)
    yf, yb = _chunk_scan(vr, fw, bw, B=B, S=S)
    h = _mix(xt, attn, yf, yb, r, kf, kb, vr, g, w["lng"], w["lnb"], w["rk"], w["aog"], w["wout"], w["jseg"],
             B=B, S=S, tm=tm)
    idx, gate = _route(h, w["g2"], w["wpq"], w["sk"], T=T, tm=_tile(T, 256))
    pk = P_HEADS * P_TOPK
    tc = _tile(T, PEER_CHUNK)
    pu = _pack_rows(expert_u)
    pv = _pack_rows(expert_v)
    outs = []
    for c in range(T // tc):
        sl = slice(c * tc, (c + 1) * tc)
        flat = idx[sl].reshape(tc * pk)
        gu = _gather_rows(pu, flat)
        gv = _gather_rows(pv, flat)
        outs.append(_experts(h[sl], w["g2"], gate[sl], gu, gv, tb=_tile(tc, 16)))
    return jnp.concatenate(outs, axis=0).reshape(B, S, D_MODEL)


def kernel(x_prompt, x_sample, norm1_g, w_in, q_lat_g, w_uq, kv_lat_g, w_ukv, q_norm_g, k_norm_g, attn_out_g, mu_prev, mu_next, w0, w_up, a0, a_up, g_up, k_k, k_a, r_k, ln_x_g, ln_x_b, w_out, norm2_g, w_pq, sub_keys, expert_u, expert_v):
    assert x_prompt.shape[1] == x_sample.shape[1]
    S = x_prompt.shape[1]
    nb = x_prompt.shape[0]
    x = jnp.concatenate([x_prompt, x_sample], axis=0)
    for l in range(norm1_g.shape[0]):
        w = _prepare(norm1_g[l], w_in[l], q_lat_g[l], w_uq[l], kv_lat_g[l], w_ukv[l], q_norm_g[l], k_norm_g[l],
                     attn_out_g[l], mu_prev[l], mu_next[l], w0[l], w_up[l], a0[l], a_up[l], g_up[l], k_k[l], k_a[l],
                     r_k[l], ln_x_g[l], ln_x_b[l], w_out[l], norm2_g[l], w_pq[l], sub_keys[l], S)
        x = _layer(x, w, expert_u[l], expert_v[l])
    return x[:nb], x[nb:]
```

```python
import functools
import math

import jax
import jax.numpy as jnp
from jax import lax
from jax.experimental import pallas as pl
from jax.experimental.pallas import tpu as pltpu
from jax.experimental.pallas import tpu_sc as plsc

D_MODEL = 1024
H_A = 8
NOPE_DIM = 64
ROPE_DIM = 32
QK_DIM = NOPE_DIM + ROPE_DIM
V_DIM = 64
Q_LORA = 384
KV_LORA = 256
ROPE_THETA = 10000.0
H_R = 8
HEAD_N = 64
C_R = H_R * HEAD_N
W_LORA = 64
A_LORA = 64
G_LORA = 128
GN_EPS = 64e-5
RWKV_IN = 3 * C_R + W_LORA + A_LORA + G_LORA
OFF_Q = 0
OFF_KV = OFF_Q + Q_LORA
OFF_KR = OFF_KV + KV_LORA
OFF_RWKV = OFF_KR + ROPE_DIM
N_KEYS = 128
N_EXPERTS = N_KEYS * N_KEYS
P_HEADS = 8
P_TOPK = 16
D_KEY = 256
HALF_KEY = D_KEY // 2
NORM_EPS = 1e-6

LANES = 128
HEAD_PAD = LANES
Z_COLS = RWKV_IN
IN_PAD = Q_LORA + KV_LORA + Z_COLS + 2 * LANES
VMEM_LIMIT = 56 * 1024 * 1024

f32 = jnp.float32
bf16 = jnp.bfloat16


def _rms(x, g):
    return x * lax.rsqrt(jnp.mean(x * x, axis=-1, keepdims=True) + NORM_EPS) * g


def _dot(a, b):
    return jnp.dot(a, b, preferred_element_type=f32)


def _dot_nt(a, b):
    return lax.dot_general(a, b, (((1,), (1,)), ((), ())), preferred_element_type=f32)


def _split_dot(x, j):
    hi = x.astype(bf16)
    lo = (x - hi.astype(f32)).astype(bf16)
    return _dot(hi, j) + _dot(lo, j)


def _inproj_kernel(x_ref, g1_ref, win_ref, qlg_ref, wqa_ref, wqb_ref, kvlg_ref, wkn_ref, wv_ref,
                   cos_ref, sin_ref, gq_ref, gk_ref,
                   q_ref, k_ref, v_ref, z_ref):
    x = x_ref[...]
    xn = _rms(x, g1_ref[...]).astype(bf16)
    proj = _dot(xn, win_ref[...])
    o_kv, o_z, o_ra, o_rb = Q_LORA, Q_LORA + KV_LORA, Q_LORA + KV_LORA + Z_COLS, Q_LORA + KV_LORA + Z_COLS + LANES
    z_ref[...] = proj[:, o_z:o_ra]
    cos = cos_ref[...]
    sin = sin_ref[...]

    ql = _rms(proj[:, :o_kv], qlg_ref[...]).astype(bf16)
    qa = _dot(ql, wqa_ref[...])
    qb = _dot(ql, wqb_ref[...])
    ga, gb = gq_ref[0:1, :], gq_ref[1:2, :]
    scale = QK_DIM ** -0.5 * math.log2(math.e)
    for h in range(H_A):
        a = qa[:, h * LANES:(h + 1) * LANES]
        b = qb[:, h * LANES:(h + 1) * LANES]
        s = lax.rsqrt(jnp.sum(a * a, axis=-1, keepdims=True) * (1.0 / QK_DIM) + NORM_EPS) * scale
        q_ref[0, h] = (s * (a * ga * cos + b * gb * sin)).astype(bf16)

    kvl = _rms(proj[:, o_kv:o_z], kvlg_ref[...]).astype(bf16)
    kn = _dot(kvl, wkn_ref[...])
    vv = _dot(kvl, wv_ref[...])
    one = (lax.broadcasted_iota(jnp.int32, (1, LANES), 1) == V_DIM).astype(f32)
    for h in range(H_A):
        v_ref[0, h] = (vv[:, h * LANES:(h + 1) * LANES] + one).astype(bf16)
    kra = proj[:, o_ra:o_rb]
    krb = proj[:, o_rb:o_rb + LANES]
    gkn, gka, gkb = gk_ref[0:1, :], gk_ref[1:2, :], gk_ref[2:3, :]
    kr = kra * gka * cos + krb * gkb * sin
    ssr = jnp.sum(kra * kra, axis=-1, keepdims=True)
    for h in range(H_A):
        a = kn[:, h * LANES:(h + 1) * LANES]
        s = lax.rsqrt((jnp.sum(a * a, axis=-1, keepdims=True) + ssr) * (1.0 / QK_DIM) + NORM_EPS)
        k_ref[0, h] = (s * (a * gkn + kr)).astype(bf16)


def _inproj(x, g1, win, qlg, wqa, wqb, kvlg, wkn, wv, cos, sin, gq, gk, *, B, S, tm):
    nS = S // tm
    const = lambda shape: pl.BlockSpec(shape, lambda b, s: (0,) * len(shape))
    return pl.pallas_call(
        _inproj_kernel,
        grid=(B, nS),
        in_specs=[
            pl.BlockSpec((tm, D_MODEL), lambda b, s: (b * nS + s, 0)),
            const((1, D_MODEL)), const(win.shape), const((1, Q_LORA)), const(wqa.shape), const(wqb.shape),
            const((1, KV_LORA)), const(wkn.shape), const(wv.shape),
            pl.BlockSpec((tm, LANES), lambda b, s: (s, 0)),
            pl.BlockSpec((tm, LANES), lambda b, s: (s, 0)),
            const(gq.shape), const(gk.shape),
        ],
        out_specs=[
            pl.BlockSpec((1, H_A, tm, HEAD_PAD), lambda b, s: (b, 0, s, 0)),
            pl.BlockSpec((1, H_A, tm, HEAD_PAD), lambda b, s: (b, 0, s, 0)),
            pl.BlockSpec((1, H_A, tm, LANES), lambda b, s: (b, 0, s, 0)),
            pl.BlockSpec((tm, Z_COLS), lambda b, s: (b * nS + s, 0)),
        ],
        out_shape=[
            jax.ShapeDtypeStruct((B, H_A, S, HEAD_PAD), bf16),
            jax.ShapeDtypeStruct((B, H_A, S, HEAD_PAD), bf16),
            jax.ShapeDtypeStruct((B, H_A, S, LANES), bf16),
            jax.ShapeDtypeStruct((B * S, Z_COLS), f32),
        ],
        compiler_params=pltpu.CompilerParams(
            dimension_semantics=("parallel", "parallel"), vmem_limit_bytes=VMEM_LIMIT),
        name="in_proj",
    )(x, g1, win, qlg, wqa, wqb, kvlg, wkn, wv, cos, sin, gq, gk)


ATTN_ROWS = 32
ATTN_UNROLL = 16


def _attn_kernel(q_ref, k_ref, v_ref, o_ref, s_ref, p_ref, m_ref, al_ref, acc_ref, *, tk, nk):
    tq = q_ref.shape[2]
    nlt = tk // LANES

    def scores(h, q, j, buf):
        off = pl.multiple_of(jnp.minimum(j, nk - 1) * tk, tk)
        s_ref[buf] = _dot_nt(q, k_ref[0, h, pl.ds(off, tk), :])

    def softmax(buf):
        def chunk(c, carry):
            rows = pl.ds(pl.multiple_of(c * ATTN_ROWS, ATTN_ROWS), ATTN_ROWS)
            tiles = [s_ref[buf, rows, lt * LANES:(lt + 1) * LANES] for lt in range(nlt)]
            best = functools.reduce(jnp.maximum, tiles)
            m_old = m_ref[rows, :]
            m_new = jnp.maximum(m_old, jnp.broadcast_to(jnp.max(best, axis=-1, keepdims=True), m_old.shape))
            for lt in range(nlt):
                p_ref[buf, rows, lt * LANES:(lt + 1) * LANES] = jnp.exp2(tiles[lt] - m_new).astype(bf16)
            al_ref[rows, :] = jnp.exp2(m_old - m_new)
            m_ref[rows, :] = m_new
            return carry

        lax.fori_loop(0, tq // ATTN_ROWS, chunk, 0, unroll=ATTN_UNROLL)

    def accumulate(h, j, buf):
        off = pl.multiple_of(j * tk, tk)
        acc_ref[...] = al_ref[...] * acc_ref[...] + _dot(p_ref[buf], v_ref[0, h, pl.ds(off, tk), :])

    outs = []
    for h in range(2):
        q = q_ref[0, h]
        m_ref[...] = jnp.full(m_ref.shape, -jnp.inf, f32)
        acc_ref[...] = jnp.zeros(acc_ref.shape, f32)
        scores(h, q, 0, 0)

        def body(jj, carry, h=h, q=q):
            j = 2 * jj
            scores(h, q, j + 1, 1)
            softmax(0)
            accumulate(h, j, 0)
            scores(h, q, j + 2, 0)
            softmax(1)
            accumulate(h, j + 1, 1)
            return carry

        lax.fori_loop(0, nk // 2, body, 0)
        acc = acc_ref[...]
        outs.append(acc / acc[:, V_DIM:V_DIM + 1])
    lane = lax.broadcasted_iota(jnp.int32, (tq, LANES), 1)
    o_ref[0, 0] = jnp.where(lane < V_DIM, outs[0], pltpu.roll(outs[1], V_DIM, 1))


def _attention(q, k, v, *, B, S, tq, tk):
    return pl.pallas_call(
        functools.partial(_attn_kernel, tk=tk, nk=S // tk),
        grid=(B, H_A // 2, S // tq),
        in_specs=[
            pl.BlockSpec((1, 2, tq, HEAD_PAD), lambda b, p, i: (b, p, i, 0)),
            pl.BlockSpec((1, 2, S, HEAD_PAD), lambda b, p, i: (b, p, 0, 0)),
            pl.BlockSpec((1, 2, S, LANES), lambda b, p, i: (b, p, 0, 0)),
        ],
        out_specs=pl.BlockSpec((1, 1, tq, LANES), lambda b, p, i: (b, p, i, 0)),
        out_shape=jax.ShapeDtypeStruct((B, H_A // 2, S, LANES), f32),
        scratch_shapes=[pltpu.VMEM((2, tq, tk), f32), pltpu.VMEM((2, tq, tk), bf16), pltpu.VMEM((tq, LANES), f32),
                        pltpu.VMEM((tq, LANES), f32), pltpu.VMEM((tq, LANES), f32)],
        compiler_params=pltpu.CompilerParams(
            dimension_semantics=("parallel", "parallel", "parallel"), vmem_limit_bytes=VMEM_LIMIT),
        name="attention",
    )(q, k, v)


def _softplus(x):
    return jnp.maximum(x, 0.0) + jnp.log1p(jnp.exp(-jnp.abs(x)))


def _prep_kernel(z_ref, zp_ref, zn_ref, mup_ref, mun_ref, wl_ref, w0_ref, a0_ref, gup_ref, kk_ref, ka_ref, jseg_ref,
                 r_ref, v_ref, nkk_ref, g_ref,
                 decf_ref, kkaf_ref, kf_ref, decb_ref, kkab_ref, kb_ref):
    si = pl.program_id(1)
    ns = pl.num_programs(1)
    z = z_ref[...]
    tm = z.shape[0]
    row = lax.broadcasted_iota(jnp.int32, (tm, 1), 0)
    zp_row = jnp.where(si == 0, 0.0, zp_ref[7:8, :])
    zn_row = jnp.where(si == ns - 1, 0.0, zn_ref[0:1, :])
    z_prev = jnp.where(row == 0, zp_row, pltpu.roll(z, 1, 0))
    z_next = jnp.where(row == tm - 1, zn_row, pltpu.roll(z, tm - 1, 0))
    zm = z + mup_ref[...] * (z_prev - z) + mun_ref[...] * (z_next - z)

    o1, o2, o3 = C_R, 2 * C_R, 3 * C_R
    o4 = o3 + W_LORA + A_LORA
    o5 = o4 + G_LORA
    r = zm[:, :o1]
    kr = zm[:, o1:o2]
    vr = zm[:, o2:o3]
    lw = zm[:, o3:o4]
    zg = zm[:, o4:o5]
    r_ref[...] = r
    v_ref[...] = vr
    g_ref[...] = _dot(jax.nn.sigmoid(zg).astype(bf16), gup_ref[...])

    kk = kr * kk_ref[...]
    ssk = _split_dot(kk * kk, jseg_ref[...])
    kk = kk / jnp.maximum(jnp.sqrt(ssk), 1e-12)
    nkk_ref[...] = -kk

    lane = lax.broadcasted_iota(jnp.int32, lw.shape, 1)
    lin = jnp.where(lane < W_LORA, jnp.tanh(lw), lw).astype(bf16)
    ka = ka_ref[...]
    for d, (dec_ref, kka_ref, kd_ref) in enumerate(((decf_ref, kkaf_ref, kf_ref), (decb_ref, kkab_ref, kb_ref))):
        lo = _dot(lin, wl_ref[d])
        w = -_softplus(-(w0_ref[d:d + 1, :] + lo[:, :C_R])) - 0.5
        dec_ref[...] = jnp.exp(-jnp.exp(w))
        a = jax.nn.sigmoid(a0_ref[d:d + 1, :] + lo[:, C_R:])
        kd_ref[...] = kr * (1.0 + (a - 1.0) * ka)
        kka_ref[...] = kk * a


def _rwkv_prep(z, mup, mun, wl, w0, a0, gup, kk, ka, jseg, *, B, S, tm):
    nS = S // tm
    T = B * S
    nb8 = tm // 8
    const = lambda shape: pl.BlockSpec(shape, lambda b, s: (0,) * len(shape))
    row_spec = pl.BlockSpec((tm, C_R), lambda b, s: (b * nS + s, 0))
    return pl.pallas_call(
        _prep_kernel,
        grid=(B, nS),
        in_specs=[
            pl.BlockSpec((tm, Z_COLS), lambda b, s: (b * nS + s, 0)),
            pl.BlockSpec((8, Z_COLS), lambda b, s: (jnp.maximum((b * nS + s) * nb8 - 1, 0), 0)),
            pl.BlockSpec((8, Z_COLS), lambda b, s: (jnp.minimum((b * nS + s + 1) * nb8, T // 8 - 1), 0)),
            const((1, Z_COLS)), const((1, Z_COLS)), const(wl.shape), const(w0.shape), const(a0.shape),
            const(gup.shape), const((1, C_R)), const((1, C_R)), const(jseg.shape),
        ],
        out_specs=[row_spec] * 10,
        out_shape=[jax.ShapeDtypeStruct((T, C_R), f32)] * 10,
        compiler_params=pltpu.CompilerParams(
            dimension_semantics=("parallel", "parallel"), vmem_limit_bytes=VMEM_LIMIT),
        name="rwkv_prep",
    )(z, z, z, mup, mun, wl, w0, a0, gup, kk, ka, jseg)


N_PAIRS = H_R // 2
CHUNK = 64


def _halves(x, low):
    zero = jnp.zeros_like(x)
    return jnp.concatenate([jnp.where(low, x, zero), jnp.where(low, zero, x)], axis=0)


def _chunks_kernel(r_ref, nkk_ref, dec_ref, kka_ref, k_ref, tri_ref,
                   at_ref, rt_ref, aak_ref, tt_ref, arb_ref, ark_ref, bt_ref, kt_ref, wc_ref, *, reverse):
    lane = lax.broadcasted_iota(jnp.int32, (CHUNK, LANES), 1)
    low = lane < HEAD_N
    jj = lax.broadcasted_iota(jnp.int32, (CHUNK, LANES), 0)
    ii = lane & (HEAD_N - 1)
    strict = (ii > jj) if reverse else (ii < jj)
    incl = (ii >= jj) if reverse else (ii <= jj)
    rr = lax.broadcasted_iota(jnp.int32, (LANES, LANES), 0)
    cc = lax.broadcasted_iota(jnp.int32, (LANES, LANES), 1)
    eye = (rr == cc).astype(f32)
    last = 0 if reverse else CHUNK - 1
    swap = lambda x: pltpu.roll(x, HEAD_N, 1)
    zero = jnp.zeros((CHUNK, LANES), f32)
    units = [(slice(q * CHUNK, (q + 1) * CHUNK), slice(p * LANES, (p + 1) * LANES))
             for q in range(r_ref.shape[0] // CHUNK) for p in range(N_PAIRS)]
    cums = []
    for rows, cols in units:
        lw = jnp.log(dec_ref[rows, cols])
        hi = lw.astype(bf16)
        r1 = lw - hi.astype(f32)
        mid = r1.astype(bf16)
        lo = (r1 - mid.astype(f32)).astype(bf16)
        cums.append((lw, _dot(tri_ref[...], jnp.concatenate([hi, mid, lo], axis=0))))
    outs = []
    for (rows, cols), (lw, cum) in zip(units, cums):
        w_in = jnp.exp(cum)
        w_ex = jnp.exp(cum - lw)
        w_inv = jnp.exp(-cum)
        wc = w_in[last:last + 1, :]
        at = nkk_ref[rows, cols] * w_ex
        rt = r_ref[rows, cols] * w_in
        bt = kka_ref[rows, cols] * w_inv
        kt = k_ref[rows, cols] * w_inv
        at_ref[rows, cols] = at.astype(bf16)
        rt_ref[rows, cols] = rt.astype(bf16)
        bt_ref[rows, cols] = (bt * wc).astype(bf16)
        kt_ref[rows, cols] = (kt * wc).astype(bf16)
        wc_ref[rows, cols] = jnp.broadcast_to(wc, (CHUNK, LANES))
        lhs = jnp.concatenate([_halves(at, low), _halves(rt, low)], axis=0).astype(bf16)
        rhs = jnp.concatenate([bt, kt], axis=0).astype(bf16)
        outs.append(_dot_nt(lhs, rhs))
    pws, ts = [], []
    for (rows, cols), o in zip(units, outs):
        o0, o1, o2, o3 = (o[q * CHUNK:(q + 1) * CHUNK, :] for q in range(4))
        aab = jnp.where(strict, jnp.where(low, o0, swap(o1)), zero)
        aak_ref[rows, cols] = jnp.where(strict, jnp.where(low, swap(o0), o1), zero).astype(bf16)
        arb_ref[rows, cols] = jnp.where(incl, jnp.where(low, o2, swap(o3)), zero).astype(bf16)
        ark_ref[rows, cols] = jnp.where(incl, jnp.where(low, swap(o2), o3), zero).astype(bf16)
        pws.append(_halves(aab, low))
        ts.append(eye + pws[-1])
    for _ in range(int(math.log2(CHUNK)) - 1):
        pbs = [pw.astype(bf16) for pw in pws]
        pws = [_dot(pb, pb) for pb in pbs]
        ts = [t + _dot(t.astype(bf16), pw.astype(bf16)) for t, pw in zip(ts, pws)]
    for (rows, cols), t in zip(units, ts):
        tt_ref[rows, cols] = jnp.where(low, t[:CHUNK, :], t[CHUNK:, :]).astype(bf16)


CHUNKS_PER_STEP = 4


def _wkv_chunks(r, nkk, dec, kka, k, *, T, reverse):
    tri_i = jnp.arange(CHUNK)
    tri = (tri_i[None, :] >= tri_i[:, None]) if reverse else (tri_i[None, :] <= tri_i[:, None])
    tri3 = jnp.tile(tri.astype(bf16), (1, 3))
    rows = CHUNK * CHUNKS_PER_STEP
    assert T % rows == 0
    row = pl.BlockSpec((rows, C_R), lambda i: (i, 0))
    return pl.pallas_call(
        functools.partial(_chunks_kernel, reverse=reverse),
        grid=(T // rows,),
        in_specs=[row] * 5 + [pl.BlockSpec(tri3.shape, lambda i: (0, 0))],
        out_specs=[row] * 9,
        out_shape=[jax.ShapeDtypeStruct((T, C_R), bf16)] * 8 + [jax.ShapeDtypeStruct((T, C_R), f32)],
        compiler_params=pltpu.CompilerParams(dimension_semantics=("parallel",), vmem_limit_bytes=VMEM_LIMIT),
        name="wkv_chunks",
    )(r, nkk, dec, kka, k, tri3)


def _chunk_scan_kernel(vf_ref, vb_ref, *refs, nb):
    vs = (vf_ref, vb_ref)
    ins = (refs[0:9], refs[9:18])
    ys = refs[18:20]
    st_ref = refs[20]

    @pl.when(pl.program_id(1) == 0)
    def _():
        st_ref[...] = jnp.zeros_like(st_ref)

    low = lax.broadcasted_iota(jnp.int32, (CHUNK, LANES), 1) < HEAD_N
    rr = lax.broadcasted_iota(jnp.int32, (LANES, LANES), 0) < HEAD_N
    cc = lax.broadcasted_iota(jnp.int32, (LANES, LANES), 1) < HEAD_N
    same_head = rr == cc
    chains = [(bb, d, p) for bb in range(nb) for d in range(2) for p in range(N_PAIRS)]
    part = lambda bb, d, p, which: ins[d][which][bb, :, p * LANES:(p + 1) * LANES]
    s0s = [st_ref[(bb * 2 + d) * N_PAIRS + p] for bb, d, p in chains]
    vals = [vs[d][bb, :, p * LANES:(p + 1) * LANES] for bb, d, p in chains]
    vbds = [_halves(v, low).astype(bf16) for v in vals]
    gs = [_dot(jnp.concatenate([part(*ch, 0), part(*ch, 1)], axis=0), s0.astype(bf16))
          for ch, s0 in zip(chains, s0s)]
    xs = [g[:CHUNK, :] + _dot(part(*ch, 2), vbd) for ch, g, vbd in zip(chains, gs, vbds)]
    sas = [_dot(part(*ch, 3), _halves(x, low).astype(bf16)) for ch, x in zip(chains, xs)]
    for (bb, d, p), g, sa, vbd in zip(chains, gs, sas, vbds):
        ys[d][bb, :, p * LANES:(p + 1) * LANES] = (
            g[CHUNK:, :] + _dot(part(bb, d, p, 4), _halves(sa, low).astype(bf16)) + _dot(part(bb, d, p, 5), vbd))
    for (bb, d, p), s0, sa, v in zip(chains, s0s, sas, vals):
        upd = lax.dot_general(
            jnp.concatenate([part(bb, d, p, 6), part(bb, d, p, 7)], axis=0),
            jnp.concatenate([sa, v], axis=0).astype(bf16),
            (((0,), (0,)), ((), ())), preferred_element_type=f32)
        wc = part(bb, d, p, 8)
        wcol = jnp.concatenate([wc, wc], axis=0).T
        st_ref[(bb * 2 + d) * N_PAIRS + p] = jnp.where(same_head, wcol * s0 + upd, jnp.zeros_like(s0))


CHUNK_SCAN_BATCH = 4


def _chunk_scan(v, fwd_parts, bwd_parts, *, B, S):
    nC = S // CHUNK
    nb = CHUNK_SCAN_BATCH if B % CHUNK_SCAN_BATCH == 0 else 1
    as3 = lambda a: a.reshape(B, S, C_R)
    fwd = pl.BlockSpec((nb, CHUNK, C_R), lambda b, i: (b, i, 0))
    bwd = pl.BlockSpec((nb, CHUNK, C_R), lambda b, i: (b, nC - 1 - i, 0))
    yf, yb = pl.pallas_call(
        functools.partial(_chunk_scan_kernel, nb=nb),
        grid=(B // nb, nC),
        in_specs=[fwd, bwd] + [fwd] * 9 + [bwd] * 9,
        out_specs=[fwd, bwd],
        out_shape=[jax.ShapeDtypeStruct((B, S, C_R), f32)] * 2,
        scratch_shapes=[pltpu.VMEM((nb * 2 * N_PAIRS, LANES, LANES), f32)],
        compiler_params=pltpu.CompilerParams(
            dimension_semantics=("parallel", "arbitrary"), vmem_limit_bytes=VMEM_LIMIT),
        name="wkv_scan",
    )(as3(v), as3(v), *[as3(a) for a in fwd_parts], *[as3(a) for a in bwd_parts])
    return yf.reshape(B * S, C_R), yb.reshape(B * S, C_R)

def _mix_kernel(x_ref, attn_ref, yf_ref, yb_ref, r_ref, kf_ref, kb_ref, v_ref, g_ref,
                lng_ref, lnb_ref, rk_ref, aog_ref, wout_ref, jseg_ref, h_ref):
    jseg = jseg_ref[...]
    y = yf_ref[...] + yb_ref[...]
    mu = _split_dot(y, jseg) * (1.0 / HEAD_N)
    dlt = y - mu
    var = _split_dot(dlt * dlt, jseg) * (1.0 / HEAD_N)
    yn = dlt * lax.rsqrt(var + GN_EPS) * lng_ref[...] + lnb_ref[...]
    r = r_ref[...]
    k_mean = 0.5 * (kf_ref[...] + kb_ref[...])
    bonus = _split_dot(r * k_mean * rk_ref[...], jseg) * v_ref[...]
    rw = ((yn + bonus) * g_ref[...]).astype(bf16)
    attn = jnp.concatenate([attn_ref[0, p] for p in range(H_A // 2)], axis=-1)
    an = _rms(attn, aog_ref[...]).astype(bf16)
    ha = H_A * V_DIM
    h_ref[...] = x_ref[...] + _dot(an, wout_ref[0:ha, :]) + _dot(rw, wout_ref[ha:, :])


def _mix(x, attn, yf, yb, r, kf, kb, v, g, lng, lnb, rk, aog, wout, jseg, *, B, S, tm):
    nS = S // tm
    const = lambda shape: pl.BlockSpec(shape, lambda b, s: (0,) * len(shape))
    row = lambda c: pl.BlockSpec((tm, c), lambda b, s: (b * nS + s, 0))
    return pl.pallas_call(
        _mix_kernel,
        grid=(B, nS),
        in_specs=[row(D_MODEL), pl.BlockSpec((1, H_A // 2, tm, LANES), lambda b, s: (b, 0, s, 0))]
                 + [row(C_R)] * 7
                 + [const((1, C_R))] * 3 + [const((1, H_A * V_DIM)), const(wout.shape), const(jseg.shape)],
        out_specs=row(D_MODEL),
        out_shape=jax.ShapeDtypeStruct((B * S, D_MODEL), f32),
        compiler_params=pltpu.CompilerParams(
            dimension_semantics=("parallel", "parallel"), vmem_limit_bytes=VMEM_LIMIT),
        name="mix",
    )(x, attn, yf, yb, r, kf, kb, v, g, lng, lnb, rk, aog, wout, jseg)


def _take_max(s, ids):
    m = jnp.max(s, axis=0, keepdims=True)
    pick = jnp.min(jnp.where(s == m, ids, jnp.int32(2 ** 30)), axis=0, keepdims=True)
    return m, pick, jnp.where(ids == pick, -jnp.inf, s)


CAND_COLS = tuple(P_TOPK // (a + 1) for a in range(P_TOPK))
N_CAND = sum(CAND_COLS)
N_CAND_PAD = -(-N_CAND // 8) * 8


def _route_kernel(h_ref, g2_ref, wpq_ref, sk_ref, pos_ref, idx_ref, gate_ref, v1_ref, i1_ref, v2_ref, i2_ref, cs_ref,
                  ci_ref, bs_ref, bi_ref):
    tm = h_ref.shape[0]
    hn = _rms(h_ref[...], g2_ref[...]).astype(bf16)
    pq = _dot(hn, wpq_ref[...])
    key_ids = lax.broadcasted_iota(jnp.int32, (N_KEYS, tm), 0)
    pos_ids = pos_ref[...]
    for p in range(P_HEADS):
        for side, (vs_ref, is_ref) in enumerate(((v1_ref, i1_ref), (v2_ref, i2_ref))):
            qh = pq[:, p * D_KEY + side * HALF_KEY:p * D_KEY + (side + 1) * HALF_KEY].astype(bf16)
            s = _dot_nt(sk_ref[side], qh)
            for j in range(P_TOPK):
                m, pick, s = _take_max(s, key_ids)
                vs_ref[j:j + 1, :] = m
                is_ref[j:j + 1, :] = pick
        v2 = v2_ref[...]
        i2 = i2_ref[...]
        off = 0
        for a, nb in enumerate(CAND_COLS):
            cs_ref[off:off + nb, :] = v1_ref[a:a + 1, :] + v2[:nb, :]
            ci_ref[off:off + nb, :] = i1_ref[a:a + 1, :] * N_KEYS + i2[:nb, :]
            off += nb
        cs_ref[N_CAND:, :] = jnp.full((N_CAND_PAD - N_CAND, tm), -jnp.inf, f32)
        ci_ref[N_CAND:, :] = jnp.zeros((N_CAND_PAD - N_CAND, tm), jnp.int32)
        s = cs_ref[...]
        ci = ci_ref[...]
        for j in range(P_TOPK):
            m, pick, s = _take_max(s, pos_ids)
            bs_ref[j:j + 1, :] = m
            bi_ref[p * P_TOPK + j:p * P_TOPK + j + 1, :] = jnp.max(
                jnp.where(pos_ids == pick, ci, -1), axis=0, keepdims=True)
        bs = bs_ref[...]
        e = jnp.exp(bs - bs[0:1, :])
        gate_ref[:, p * P_TOPK:(p + 1) * P_TOPK] = (e / jnp.sum(e, axis=0, keepdims=True)).T
    idx_ref[...] = bi_ref[...].T


def _route(h, g2, wpq, sk, *, T, tm):
    const = lambda shape: pl.BlockSpec(shape, lambda i: (0,) * len(shape))
    pk = P_HEADS * P_TOPK
    flat = [a * P_TOPK + b for a, nb in enumerate(CAND_COLS) for b in range(nb)] + [2 ** 20] * (N_CAND_PAD - N_CAND)
    pos = jnp.broadcast_to(jnp.array(flat, jnp.int32)[:, None], (N_CAND_PAD, tm))
    return pl.pallas_call(
        _route_kernel,
        grid=(T // tm,),
        in_specs=[pl.BlockSpec((tm, D_MODEL), lambda i: (i, 0)), const((1, D_MODEL)), const(wpq.shape),
                  const(sk.shape), const((N_CAND_PAD, tm))],
        out_specs=[pl.BlockSpec((tm, pk), lambda i: (i, 0))] * 2,
        out_shape=[jax.ShapeDtypeStruct((T, pk), jnp.int32), jax.ShapeDtypeStruct((T, pk), f32)],
        scratch_shapes=[pltpu.VMEM((P_TOPK, tm), f32), pltpu.VMEM((P_TOPK, tm), jnp.int32),
                        pltpu.VMEM((P_TOPK, tm), f32), pltpu.VMEM((P_TOPK, tm), jnp.int32),
                        pltpu.VMEM((N_CAND_PAD, tm), f32), pltpu.VMEM((N_CAND_PAD, tm), jnp.int32),
                        pltpu.VMEM((P_TOPK, tm), f32), pltpu.VMEM((pk, tm), jnp.int32)],
        compiler_params=pltpu.CompilerParams(dimension_semantics=("parallel",), vmem_limit_bytes=VMEM_LIMIT),
        name="route",
    )(h, g2, wpq, sk, pos)


GATHER_WINDOW = 64
SC_CORES = 2
SC_SUBCORES = 16
SC_WORKERS = SC_CORES * SC_SUBCORES


def _gather_rows(table, idx):
    n = idx.shape[0]
    d = table.shape[1]
    win = GATHER_WINDOW
    n_it = n // (SC_WORKERS * win)
    assert n_it * SC_WORKERS * win == n and n_it % 2 == 0, (n, n_it)
    mesh = plsc.VectorSubcoreMesh(core_axis_name="core", subcore_axis_name="subcore")

    @functools.partial(
        pl.kernel, out_type=jax.ShapeDtypeStruct((n, d), table.dtype), mesh=mesh,
        scratch_types=[pltpu.VMEM((n_it, win), jnp.int32), pltpu.VMEM((2, win, d), table.dtype),
                       pltpu.SemaphoreType.DMA, pltpu.SemaphoreType.DMA,
                       pltpu.SemaphoreType.DMA, pltpu.SemaphoreType.DMA])
    def gather(tab_hbm, idx_hbm, out_hbm, idx_v, rows_v, gsem0, gsem1, wsem0, wsem1):
        wid = lax.axis_index("subcore") * SC_CORES + lax.axis_index("core")
        base = wid * (n_it * win)
        gsem = (gsem0, gsem1)
        wsem = (wsem0, wsem1)
        pltpu.sync_copy(idx_hbm.at[wid], idx_v)

        def fetch(j, b):
            return pltpu.make_async_copy(tab_hbm.at[idx_v.at[j]], rows_v.at[b], gsem[b])

        def flush(j, b):
            return pltpu.make_async_copy(rows_v.at[b], out_hbm.at[pl.ds(base + j * win, win)], wsem[b])

        fetch(0, 0).start()

        @pl.loop(0, n_it, step=2)
        def _(j0):
            for b in range(2):
                j = j0 + b

                @pl.when(j >= 1)
                def _():
                    flush(j - 1, 1 - b).wait()

                @pl.when(j + 1 < n_it)
                def _():
                    fetch(j + 1, 1 - b).start()

                fetch(j, b).wait()
                flush(j, b).start()

        flush(n_it - 1, 1).wait()

    return gather(table, idx.reshape(SC_WORKERS, n_it, win))


HALF_D = D_MODEL // 2


def _pack_rows(table):
    bits = lax.bitcast_convert_type(table.astype(bf16), jnp.uint16).astype(jnp.uint32)
    return bits[:, :HALF_D] | (bits[:, HALF_D:] << 16)


def _unpack_rows(words):
    lo = lax.bitcast_convert_type(words << 16, f32).astype(bf16)
    hi = lax.bitcast_convert_type(words & jnp.uint32(0xFFFF0000), f32).astype(bf16)
    return lo, hi


def _expert_kernel(h_ref, g2_ref, gate_ref, gu_ref, gv_ref, o_ref):
    h = h_ref[...]
    tb = h.shape[0]
    pk = P_HEADS * P_TOPK
    hn = _rms(h, g2_ref[...]).astype(bf16)
    row = lax.broadcasted_iota(jnp.int32, (8, pk), 0)
    for grp in range(tb // 8):
        rows = slice(grp * 8, (grp + 1) * 8)
        hn8 = hn[rows, :]
        act = jnp.zeros((8, pk), f32)
        for j in range(8):
            t = grp * 8 + j
            ulo, uhi = _unpack_rows(gu_ref[t * pk:(t + 1) * pk, :])
            act = jnp.where(row == j, _dot_nt(hn8[:, :HALF_D], ulo) + _dot_nt(hn8[:, HALF_D:], uhi), act)
        gelu = 0.5 * act * (1.0 + lax.erf(act * (2.0 ** -0.5)))
        w = (gate_ref[rows, :] * gelu).astype(bf16)
        out_lo = jnp.zeros((8, HALF_D), f32)
        out_hi = jnp.zeros((8, HALF_D), f32)
        for j in range(8):
            t = grp * 8 + j
            vlo, vhi = _unpack_rows(gv_ref[t * pk:(t + 1) * pk, :])
            wj = jnp.where(row == j, w, jnp.zeros_like(w))
            out_lo = out_lo + _dot(wj, vlo)
            out_hi = out_hi + _dot(wj, vhi)
        o_ref[rows, :HALF_D] = h[rows, :HALF_D] + out_lo
        o_ref[rows, HALF_D:] = h[rows, HALF_D:] + out_hi


def _experts(h, g2, gate, gu, gv, *, tb):
    n = h.shape[0]
    pk = P_HEADS * P_TOPK
    const = lambda shape: pl.BlockSpec(shape, lambda i: (0,) * len(shape))
    return pl.pallas_call(
        _expert_kernel,
        grid=(n // tb,),
        in_specs=[pl.BlockSpec((tb, D_MODEL), lambda i: (i, 0)), const((1, D_MODEL)),
                  pl.BlockSpec((tb, pk), lambda i: (i, 0)),
                  pl.BlockSpec((tb * pk, HALF_D), lambda i: (i, 0)),
                  pl.BlockSpec((tb * pk, HALF_D), lambda i: (i, 0))],
        out_specs=pl.BlockSpec((tb, D_MODEL), lambda i: (i, 0)),
        out_shape=jax.ShapeDtypeStruct((n, D_MODEL), f32),
        compiler_params=pltpu.CompilerParams(dimension_semantics=("parallel",), vmem_limit_bytes=VMEM_LIMIT),
        name="experts",
    )(h, g2, gate, gu, gv)


def _place(cols, width, offset):
    return jnp.pad(cols, ((0, 0), (offset, width - offset - cols.shape[1])))


def _rope_partner():
    half = ROPE_DIM // 2
    return jnp.concatenate([jnp.arange(half, ROPE_DIM), jnp.arange(0, half)])


def _prepare(norm1_g, w_in, q_lat_g, w_uq, kv_lat_g, w_ukv, q_norm_g, k_norm_g, attn_out_g, mu_prev, mu_next,
             w0, w_up, a0, a_up, g_up, k_k, k_a, r_k, ln_x_g, ln_x_b, w_out, norm2_g, w_pq, sub_keys, S):
    partner = _rope_partner()
    wz = w_in[:, OFF_RWKV:]
    wkr = w_in[:, OFF_KR:OFF_RWKV]
    win = jnp.concatenate([
        w_in[:, OFF_Q:OFF_KV], w_in[:, OFF_KV:OFF_KR], wz,
        _place(wkr, LANES, NOPE_DIM), _place(wkr[:, partner], LANES, NOPE_DIM)], axis=1).astype(bf16)

    wq = w_uq.reshape(Q_LORA, H_A, QK_DIM)
    wqa = jnp.pad(wq, ((0, 0), (0, 0), (0, HEAD_PAD - QK_DIM))).reshape(Q_LORA, H_A * HEAD_PAD).astype(bf16)
    wqb = jnp.pad(wq[:, :, NOPE_DIM:][:, :, partner],
                  ((0, 0), (0, 0), (NOPE_DIM, HEAD_PAD - QK_DIM))).reshape(Q_LORA, H_A * HEAD_PAD).astype(bf16)
    wkv = w_ukv.reshape(KV_LORA, H_A, NOPE_DIM + V_DIM)
    wkn = jnp.pad(wkv[:, :, :NOPE_DIM], ((0, 0), (0, 0), (0, HEAD_PAD - NOPE_DIM))).reshape(
        KV_LORA, H_A * HEAD_PAD).astype(bf16)
    wv = jnp.pad(wkv[:, :, NOPE_DIM:], ((0, 0), (0, 0), (0, LANES - V_DIM))).reshape(
        KV_LORA, H_A * LANES).astype(bf16)

    def gain_rows(g):
        ga = _place(g[None, :], LANES, 0)
        gb = _place(g[None, NOPE_DIM:][:, partner], LANES, NOPE_DIM)
        return ga, gb

    gqa, gqb = gain_rows(q_norm_g)
    gq = jnp.concatenate([gqa, gqb, jnp.zeros((6, LANES), f32)], axis=0)
    gkn = _place(k_norm_g[None, :NOPE_DIM], LANES, 0)
    gka = _place(k_norm_g[None, NOPE_DIM:], LANES, NOPE_DIM)
    gkb = _place(k_norm_g[None, NOPE_DIM:][:, partner], LANES, NOPE_DIM)
    gk = jnp.concatenate([gkn, gka, gkb, jnp.zeros((5, LANES), f32)], axis=0)

    half = ROPE_DIM // 2
    inv = 1.0 / (ROPE_THETA ** (jnp.arange(half, dtype=f32) / half))
    ang = jnp.arange(S, dtype=f32)[:, None] * inv[None, :]
    c, s = jnp.cos(ang), jnp.sin(ang)
    cos = jnp.concatenate([jnp.ones((S, NOPE_DIM), f32), c, c, jnp.zeros((S, HEAD_PAD - QK_DIM), f32)], axis=1)
    sin = jnp.concatenate([jnp.zeros((S, NOPE_DIM), f32), -s, s, jnp.zeros((S, HEAD_PAD - QK_DIM), f32)], axis=1)

    zeros = jnp.zeros((W_LORA, C_R), f32)
    wl = jnp.stack([jnp.concatenate([jnp.concatenate([w_up[d], zeros], axis=1),
                                     jnp.concatenate([zeros, a_up[d]], axis=1)], axis=0) for d in range(2)]).astype(bf16)
    seg = jnp.arange(C_R) // HEAD_N
    jseg = (seg[:, None] == seg[None, :]).astype(bf16)

    return dict(
        g1=norm1_g[None, :], win=win, qlg=q_lat_g[None, :], wqa=wqa, wqb=wqb, kvlg=kv_lat_g[None, :], wkn=wkn, wv=wv,
        cos=cos, sin=sin, gq=gq, gk=gk,
        mup=mu_prev[None, :], mun=mu_next[None, :], wl=wl, w0=w0, a0=a0,
        gup=g_up.astype(bf16), kk=k_k[None, :], ka=k_a[None, :], jseg=jseg,
        lng=ln_x_g[None, :], lnb=ln_x_b[None, :], rk=r_k[None, :], aog=attn_out_g[None, :],
        wout=w_out.astype(bf16), g2=norm2_g[None, :], wpq=w_pq.astype(bf16), sk=sub_keys.astype(bf16),
    )


def _tile(n, pref):
    t = min(n, pref)
    assert n % t == 0, (n, t)
    return t


PEER_CHUNK = 4096


def _dense(x, w):
    B, S, _ = x.shape
    T = B * S
    xt = x.reshape(T, D_MODEL)
    tm = _tile(S, 256)
    q, k, v, z = _inproj(xt, w["g1"], w["win"], w["qlg"], w["wqa"], w["wqb"], w["kvlg"], w["wkn"], w["wv"],
                         w["cos"], w["sin"], w["gq"], w["gk"], B=B, S=S, tm=tm)
    attn = _attention(q, k, v, B=B, S=S, tq=_tile(S, 512), tk=_tile(S // 2, 512))
    r, vr, nkk, g, decf, kkaf, kf, decb, kkab, kb = _rwkv_prep(
        z, w["mup"], w["mun"], w["wl"], w["w0"], w["a0"], w["gup"], w["kk"], w["ka"], w["jseg"], B=B, S=S, tm=tm)
    fw = _wkv_chunks(r, nkk, decf, kkaf, kf, T=T, reverse=False)
    bw = _wkv_chunks(r, nkk, decb, kkab, kb, T=T, reverse=True)
    yf, yb = _chunk_scan(vr, fw, bw, B=B, S=S)
    h = _mix(xt, attn, yf, yb, r, kf, kb, vr, g, w["lng"], w["lnb"], w["rk"], w["aog"], w["wout"], w["jseg"],
             B=B, S=S, tm=tm)
    idx, gate = _route(h, w["g2"], w["wpq"], w["sk"], T=T, tm=_tile(T, 256))
    return h, idx, gate


def _peer(h, idx, gate, w, pu, pv, shape):
    T = h.shape[0]
    pk = P_HEADS * P_TOPK
    tc = _tile(T, PEER_CHUNK)
    outs = []
    for c in range(T // tc):
        sl = slice(c * tc, (c + 1) * tc)
        flat = idx[sl].reshape(tc * pk)
        gu = _gather_rows(pu, flat)
        gv = _gather_rows(pv, flat)
        outs.append(_experts(h[sl], w["g2"], gate[sl], gu, gv, tb=_tile(tc, 16)))
    return jnp.concatenate(outs, axis=0).reshape(shape)


def _layer(xs, w, expert_u, expert_v):
    pu = _pack_rows(expert_u)
    pv = _pack_rows(expert_v)
    outs = []
    pending = None
    for x in xs:
        cur = _dense(x, w) + (x.shape,)
        if pending is not None:
            outs.append(_peer(*pending[:3], w, pu, pv, pending[3]))
        pending = cur
    outs.append(_peer(*pending[:3], w, pu, pv, pending[3]))
    return outs


GROUP_BATCH = 4


def kernel(x_prompt, x_sample, norm1_g, w_in, q_lat_g, w_uq, kv_lat_g, w_ukv, q_norm_g, k_norm_g, attn_out_g, mu_prev, mu_next, w0, w_up, a0, a_up, g_up, k_k, k_a, r_k, ln_x_g, ln_x_b, w_out, norm2_g, w_pq, sub_keys, expert_u, expert_v):
    assert x_prompt.shape[1] == x_sample.shape[1]
    S = x_prompt.shape[1]
    groups = []
    for x in (x_prompt, x_sample):
        gb = GROUP_BATCH if x.shape[0] % GROUP_BATCH == 0 else x.shape[0]
        groups.append([x[i:i + gb] for i in range(0, x.shape[0], gb)])
    xs = groups[0] + groups[1]
    for l in range(norm1_g.shape[0]):
        w = _prepare(norm1_g[l], w_in[l], q_lat_g[l], w_uq[l], kv_lat_g[l], w_ukv[l], q_norm_g[l], k_norm_g[l],
                     attn_out_g[l], mu_prev[l], mu_next[l], w0[l], w_up[l], a0[l], a_up[l], g_up[l], k_k[l], k_a[l],
                     r_k[l], ln_x_g[l], ln_x_b[l], w_out[l], norm2_g[l], w_pq[l], sub_keys[l], S)
        xs = _layer(xs, w, expert_u[l], expert_v[l])
    n0 = len(groups[0])
    join = lambda parts: parts[0] if len(parts) == 1 else jnp.concatenate(parts, axis=0)
    return join(xs[:n0]), join(xs[n0:])
```

```python
import functools
import math

import jax
import jax.numpy as jnp
from jax import lax
from jax.experimental import pallas as pl
from jax.experimental.pallas import tpu as pltpu
from jax.experimental.pallas import tpu_sc as plsc

D_MODEL = 1024
H_A = 8
NOPE_DIM = 64
ROPE_DIM = 32
QK_DIM = NOPE_DIM + ROPE_DIM
V_DIM = 64
Q_LORA = 384
KV_LORA = 256
ROPE_THETA = 10000.0
H_R = 8
HEAD_N = 64
C_R = H_R * HEAD_N
W_LORA = 64
A_LORA = 64
G_LORA = 128
GN_EPS = 64e-5
RWKV_IN = 3 * C_R + W_LORA + A_LORA + G_LORA
OFF_Q = 0
OFF_KV = OFF_Q + Q_LORA
OFF_KR = OFF_KV + KV_LORA
OFF_RWKV = OFF_KR + ROPE_DIM
N_KEYS = 128
N_EXPERTS = N_KEYS * N_KEYS
P_HEADS = 8
P_TOPK = 16
D_KEY = 256
HALF_KEY = D_KEY // 2
NORM_EPS = 1e-6

LANES = 128
HEAD_PAD = LANES
Z_COLS = RWKV_IN
IN_PAD = Q_LORA + KV_LORA + Z_COLS + 2 * LANES
VMEM_LIMIT = 56 * 1024 * 1024

f32 = jnp.float32
bf16 = jnp.bfloat16


def _rms(x, g):
    return x * lax.rsqrt(jnp.mean(x * x, axis=-1, keepdims=True) + NORM_EPS) * g


def _dot(a, b):
    return jnp.dot(a, b, preferred_element_type=f32)


def _dot_nt(a, b):
    return lax.dot_general(a, b, (((1,), (1,)), ((), ())), preferred_element_type=f32)


def _split_dot(x, j):
    hi = x.astype(bf16)
    lo = (x - hi.astype(f32)).astype(bf16)
    return _dot(hi, j) + _dot(lo, j)


def _inproj_kernel(x_ref, g1_ref, win_ref, qlg_ref, wqa_ref, wqb_ref, kvlg_ref, wkn_ref, wv_ref,
                   cos_ref, sin_ref, gq_ref, gk_ref,
                   q_ref, k_ref, v_ref, z_ref):
    x = x_ref[...]
    xn = _rms(x, g1_ref[...]).astype(bf16)
    proj = _dot(xn, win_ref[...])
    o_kv, o_z, o_ra, o_rb = Q_LORA, Q_LORA + KV_LORA, Q_LORA + KV_LORA + Z_COLS, Q_LORA + KV_LORA + Z_COLS + LANES
    z_ref[...] = proj[:, o_z:o_ra]
    cos = cos_ref[...]
    sin = sin_ref[...]

    ql = _rms(proj[:, :o_kv], qlg_ref[...]).astype(bf16)
    qa = _dot(ql, wqa_ref[...])
    qb = _dot(ql, wqb_ref[...])
    ga, gb = gq_ref[0:1, :], gq_ref[1:2, :]
    scale = QK_DIM ** -0.5 * math.log2(math.e)
    for h in range(H_A):
        a = qa[:, h * LANES:(h + 1) * LANES]
        b = qb[:, h * LANES:(h + 1) * LANES]
        s = lax.rsqrt(jnp.sum(a * a, axis=-1, keepdims=True) * (1.0 / QK_DIM) + NORM_EPS) * scale
        q_ref[0, h] = (s * (a * ga * cos + b * gb * sin)).astype(bf16)

    kvl = _rms(proj[:, o_kv:o_z], kvlg_ref[...]).astype(bf16)
    kn = _dot(kvl, wkn_ref[...])
    vv = _dot(kvl, wv_ref[...])
    one = (lax.broadcasted_iota(jnp.int32, (1, LANES), 1) == V_DIM).astype(f32)
    for h in range(H_A):
        v_ref[0, h] = (vv[:, h * LANES:(h + 1) * LANES] + one).astype(bf16)
    kra = proj[:, o_ra:o_rb]
    krb = proj[:, o_rb:o_rb + LANES]
    gkn, gka, gkb = gk_ref[0:1, :], gk_ref[1:2, :], gk_ref[2:3, :]
    kr = kra * gka * cos + krb * gkb * sin
    ssr = jnp.sum(kra * kra, axis=-1, keepdims=True)
    for h in range(H_A):
        a = kn[:, h * LANES:(h + 1) * LANES]
        s = lax.rsqrt((jnp.sum(a * a, axis=-1, keepdims=True) + ssr) * (1.0 / QK_DIM) + NORM_EPS)
        k_ref[0, h] = (s * (a * gkn + kr)).astype(bf16)


def _inproj(x, g1, win, qlg, wqa, wqb, kvlg, wkn, wv, cos, sin, gq, gk, *, B, S, tm):
    nS = S // tm
    const = lambda shape: pl.BlockSpec(shape, lambda b, s: (0,) * len(shape))
    return pl.pallas_call(
        _inproj_kernel,
        grid=(B, nS),
        in_specs=[
            pl.BlockSpec((tm, D_MODEL), lambda b, s: (b * nS + s, 0)),
            const((1, D_MODEL)), const(win.shape), const((1, Q_LORA)), const(wqa.shape), const(wqb.shape),
            const((1, KV_LORA)), const(wkn.shape), const(wv.shape),
            pl.BlockSpec((tm, LANES), lambda b, s: (s, 0)),
            pl.BlockSpec((tm, LANES), lambda b, s: (s, 0)),
            const(gq.shape), const(gk.shape),
        ],
        out_specs=[
            pl.BlockSpec((1, H_A, tm, HEAD_PAD), lambda b, s: (b, 0, s, 0)),
            pl.BlockSpec((1, H_A, tm, HEAD_PAD), lambda b, s: (b, 0, s, 0)),
            pl.BlockSpec((1, H_A, tm, LANES), lambda b, s: (b, 0, s, 0)),
            pl.BlockSpec((tm, Z_COLS), lambda b, s: (b * nS + s, 0)),
        ],
        out_shape=[
            jax.ShapeDtypeStruct((B, H_A, S, HEAD_PAD), bf16),
            jax.ShapeDtypeStruct((B, H_A, S, HEAD_PAD), bf16),
            jax.ShapeDtypeStruct((B, H_A, S, LANES), bf16),
            jax.ShapeDtypeStruct((B * S, Z_COLS), f32),
        ],
        compiler_params=pltpu.CompilerParams(
            dimension_semantics=("parallel", "parallel"), vmem_limit_bytes=VMEM_LIMIT),
        name="in_proj",
    )(x, g1, win, qlg, wqa, wqb, kvlg, wkn, wv, cos, sin, gq, gk)


ATTN_ROWS = 32
ATTN_UNROLL = 16


def _attn_kernel(q_ref, k_ref, v_ref, o_ref, s_ref, p_ref, m_ref, al_ref, acc_ref, *, tk, nk):
    tq = q_ref.shape[2]
    nlt = tk // LANES

    def scores(h, q, j, buf):
        off = pl.multiple_of(jnp.minimum(j, nk - 1) * tk, tk)
        s_ref[buf] = _dot_nt(q, k_ref[0, h, pl.ds(off, tk), :])

    def softmax(buf):
        def chunk(c, carry):
            rows = pl.ds(pl.multiple_of(c * ATTN_ROWS, ATTN_ROWS), ATTN_ROWS)
            tiles = [s_ref[buf, rows, lt * LANES:(lt + 1) * LANES] for lt in range(nlt)]
            best = functools.reduce(jnp.maximum, tiles)
            m_old = m_ref[rows, :]
            m_new = jnp.maximum(m_old, jnp.broadcast_to(jnp.max(best, axis=-1, keepdims=True), m_old.shape))
            for lt in range(nlt):
                p_ref[buf, rows, lt * LANES:(lt + 1) * LANES] = jnp.exp2(tiles[lt] - m_new).astype(bf16)
            al_ref[rows, :] = jnp.exp2(m_old - m_new)
            m_ref[rows, :] = m_new
            return carry

        lax.fori_loop(0, tq // ATTN_ROWS, chunk, 0, unroll=ATTN_UNROLL)

    def accumulate(h, j, buf):
        off = pl.multiple_of(j * tk, tk)
        acc_ref[...] = al_ref[...] * acc_ref[...] + _dot(p_ref[buf], v_ref[0, h, pl.ds(off, tk), :])

    outs = []
    for h in range(2):
        q = q_ref[0, h]
        m_ref[...] = jnp.full(m_ref.shape, -jnp.inf, f32)
        acc_ref[...] = jnp.zeros(acc_ref.shape, f32)
        scores(h, q, 0, 0)

        def body(jj, carry, h=h, q=q):
            j = 2 * jj
            scores(h, q, j + 1, 1)
            softmax(0)
            accumulate(h, j, 0)
            scores(h, q, j + 2, 0)
            softmax(1)
            accumulate(h, j + 1, 1)
            return carry

        lax.fori_loop(0, nk // 2, body, 0)
        acc = acc_ref[...]
        outs.append(acc / acc[:, V_DIM:V_DIM + 1])
    lane = lax.broadcasted_iota(jnp.int32, (tq, LANES), 1)
    o_ref[0, 0] = jnp.where(lane < V_DIM, outs[0], pltpu.roll(outs[1], V_DIM, 1))


def _attention(q, k, v, *, B, S, tq, tk):
    return pl.pallas_call(
        functools.partial(_attn_kernel, tk=tk, nk=S // tk),
        grid=(B, H_A // 2, S // tq),
        in_specs=[
            pl.BlockSpec((1, 2, tq, HEAD_PAD), lambda b, p, i: (b, p, i, 0)),
            pl.BlockSpec((1, 2, S, HEAD_PAD), lambda b, p, i: (b, p, 0, 0)),
            pl.BlockSpec((1, 2, S, LANES), lambda b, p, i: (b, p, 0, 0)),
        ],
        out_specs=pl.BlockSpec((1, 1, tq, LANES), lambda b, p, i: (b, p, i, 0)),
        out_shape=jax.ShapeDtypeStruct((B, H_A // 2, S, LANES), f32),
        scratch_shapes=[pltpu.VMEM((2, tq, tk), f32), pltpu.VMEM((2, tq, tk), bf16), pltpu.VMEM((tq, LANES), f32),
                        pltpu.VMEM((tq, LANES), f32), pltpu.VMEM((tq, LANES), f32)],
        compiler_params=pltpu.CompilerParams(
            dimension_semantics=("parallel", "parallel", "parallel"), vmem_limit_bytes=VMEM_LIMIT),
        name="attention",
    )(q, k, v)


def _softplus(x):
    return jnp.maximum(x, 0.0) + jnp.log1p(jnp.exp(-jnp.abs(x)))


def _prep_kernel(z_ref, zp_ref, zn_ref, mup_ref, mun_ref, wl_ref, w0_ref, a0_ref, gup_ref, kk_ref, ka_ref, jseg_ref,
                 r_ref, v_ref, nkk_ref, g_ref,
                 decf_ref, kkaf_ref, kf_ref, decb_ref, kkab_ref, kb_ref):
    si = pl.program_id(1)
    ns = pl.num_programs(1)
    z = z_ref[...]
    tm = z.shape[0]
    row = lax.broadcasted_iota(jnp.int32, (tm, 1), 0)
    zp_row = jnp.where(si == 0, 0.0, zp_ref[7:8, :])
    zn_row = jnp.where(si == ns - 1, 0.0, zn_ref[0:1, :])
    z_prev = jnp.where(row == 0, zp_row, pltpu.roll(z, 1, 0))
    z_next = jnp.where(row == tm - 1, zn_row, pltpu.roll(z, tm - 1, 0))
    zm = z + mup_ref[...] * (z_prev - z) + mun_ref[...] * (z_next - z)

    o1, o2, o3 = C_R, 2 * C_R, 3 * C_R
    o4 = o3 + W_LORA + A_LORA
    o5 = o4 + G_LORA
    r = zm[:, :o1]
    kr = zm[:, o1:o2]
    vr = zm[:, o2:o3]
    lw = zm[:, o3:o4]
    zg = zm[:, o4:o5]
    r_ref[...] = r
    v_ref[...] = vr
    g_ref[...] = _dot(jax.nn.sigmoid(zg).astype(bf16), gup_ref[...])

    kk = kr * kk_ref[...]
    ssk = _split_dot(kk * kk, jseg_ref[...])
    kk = kk / jnp.maximum(jnp.sqrt(ssk), 1e-12)
    nkk_ref[...] = -kk

    lane = lax.broadcasted_iota(jnp.int32, lw.shape, 1)
    lin = jnp.where(lane < W_LORA, jnp.tanh(lw), lw).astype(bf16)
    ka = ka_ref[...]
    for d, (dec_ref, kka_ref, kd_ref) in enumerate(((decf_ref, kkaf_ref, kf_ref), (decb_ref, kkab_ref, kb_ref))):
        lo = _dot(lin, wl_ref[d])
        w = -_softplus(-(w0_ref[d:d + 1, :] + lo[:, :C_R])) - 0.5
        dec_ref[...] = jnp.exp(-jnp.exp(w))
        a = jax.nn.sigmoid(a0_ref[d:d + 1, :] + lo[:, C_R:])
        kd_ref[...] = kr * (1.0 + (a - 1.0) * ka)
        kka_ref[...] = kk * a


def _rwkv_prep(z, mup, mun, wl, w0, a0, gup, kk, ka, jseg, *, B, S, tm):
    nS = S // tm
    T = B * S
    nb8 = tm // 8
    const = lambda shape: pl.BlockSpec(shape, lambda b, s: (0,) * len(shape))
    row_spec = pl.BlockSpec((tm, C_R), lambda b, s: (b * nS + s, 0))
    return pl.pallas_call(
        _prep_kernel,
        grid=(B, nS),
        in_specs=[
            pl.BlockSpec((tm, Z_COLS), lambda b, s: (b * nS + s, 0)),
            pl.BlockSpec((8, Z_COLS), lambda b, s: (jnp.maximum((b * nS + s) * nb8 - 1, 0), 0)),
            pl.BlockSpec((8, Z_COLS), lambda b, s: (jnp.minimum((b * nS + s + 1) * nb8, T // 8 - 1), 0)),
            const((1, Z_COLS)), const((1, Z_COLS)), const(wl.shape), const(w0.shape), const(a0.shape),
            const(gup.shape), const((1, C_R)), const((1, C_R)), const(jseg.shape),
        ],
        out_specs=[row_spec] * 10,
        out_shape=[jax.ShapeDtypeStruct((T, C_R), f32)] * 10,
        compiler_params=pltpu.CompilerParams(
            dimension_semantics=("parallel", "parallel"), vmem_limit_bytes=VMEM_LIMIT),
        name="rwkv_prep",
    )(z, z, z, mup, mun, wl, w0, a0, gup, kk, ka, jseg)


N_PAIRS = H_R // 2
CHUNK = 64


def _halves(x, low):
    zero = jnp.zeros_like(x)
    return jnp.concatenate([jnp.where(low, x, zero), jnp.where(low, zero, x)], axis=0)


def _chunks_kernel(r_ref, nkk_ref, dec_ref, kka_ref, k_ref, tri_ref,
                   at_ref, rt_ref, aak_ref, tt_ref, arb_ref, ark_ref, bt_ref, kt_ref, wc_ref, *, reverse):
    lane = lax.broadcasted_iota(jnp.int32, (CHUNK, LANES), 1)
    low = lane < HEAD_N
    jj = lax.broadcasted_iota(jnp.int32, (CHUNK, LANES), 0)
    ii = lane & (HEAD_N - 1)
    strict = (ii > jj) if reverse else (ii < jj)
    incl = (ii >= jj) if reverse else (ii <= jj)
    rr = lax.broadcasted_iota(jnp.int32, (LANES, LANES), 0)
    cc = lax.broadcasted_iota(jnp.int32, (LANES, LANES), 1)
    eye = (rr == cc).astype(f32)
    last = 0 if reverse else CHUNK - 1
    swap = lambda x: pltpu.roll(x, HEAD_N, 1)
    zero = jnp.zeros((CHUNK, LANES), f32)
    units = [(slice(q * CHUNK, (q + 1) * CHUNK), slice(p * LANES, (p + 1) * LANES))
             for q in range(r_ref.shape[0] // CHUNK) for p in range(N_PAIRS)]
    cums = []
    for rows, cols in units:
        lw = jnp.log(dec_ref[rows, cols])
        hi = lw.astype(bf16)
        r1 = lw - hi.astype(f32)
        mid = r1.astype(bf16)
        lo = (r1 - mid.astype(f32)).astype(bf16)
        cums.append((lw, _dot(tri_ref[...], jnp.concatenate([hi, mid, lo], axis=0))))
    outs = []
    for (rows, cols), (lw, cum) in zip(units, cums):
        w_in = jnp.exp(cum)
        w_ex = jnp.exp(cum - lw)
        w_inv = jnp.exp(-cum)
        wc = w_in[last:last + 1, :]
        at = nkk_ref[rows, cols] * w_ex
        rt = r_ref[rows, cols] * w_in
        bt = kka_ref[rows, cols] * w_inv
        kt = k_ref[rows, cols] * w_inv
        at_ref[rows, cols] = at.astype(bf16)
        rt_ref[rows, cols] = rt.astype(bf16)
        bt_ref[rows, cols] = (bt * wc).astype(bf16)
        kt_ref[rows, cols] = (kt * wc).astype(bf16)
        wc_ref[rows, cols] = jnp.broadcast_to(wc, (CHUNK, LANES))
        lhs = jnp.concatenate([_halves(at, low), _halves(rt, low)], axis=0).astype(bf16)
        rhs = jnp.concatenate([bt, kt], axis=0).astype(bf16)
        outs.append(_dot_nt(lhs, rhs))
    pws, ts = [], []
    for (rows, cols), o in zip(units, outs):
        o0, o1, o2, o3 = (o[q * CHUNK:(q + 1) * CHUNK, :] for q in range(4))
        aab = jnp.where(strict, jnp.where(low, o0, swap(o1)), zero)
        aak_ref[rows, cols] = jnp.where(strict, jnp.where(low, swap(o0), o1), zero).astype(bf16)
        arb_ref[rows, cols] = jnp.where(incl, jnp.where(low, o2, swap(o3)), zero).astype(bf16)
        ark_ref[rows, cols] = jnp.where(incl, jnp.where(low, swap(o2), o3), zero).astype(bf16)
        pws.append(_halves(aab, low))
        ts.append(eye + pws[-1])
    for _ in range(int(math.log2(CHUNK)) - 1):
        pbs = [pw.astype(bf16) for pw in pws]
        pws = [_dot(pb, pb) for pb in pbs]
        ts = [t + _dot(t.astype(bf16), pw.astype(bf16)) for t, pw in zip(ts, pws)]
    for (rows, cols), t in zip(units, ts):
        tt_ref[rows, cols] = jnp.where(low, t[:CHUNK, :], t[CHUNK:, :]).astype(bf16)


CHUNKS_PER_STEP = 4


def _wkv_chunks(r, nkk, dec, kka, k, *, T, reverse):
    tri_i = jnp.arange(CHUNK)
    tri = (tri_i[None, :] >= tri_i[:, None]) if reverse else (tri_i[None, :] <= tri_i[:, None])
    tri3 = jnp.tile(tri.astype(bf16), (1, 3))
    rows = CHUNK * CHUNKS_PER_STEP
    assert T % rows == 0
    row = pl.BlockSpec((rows, C_R), lambda i: (i, 0))
    return pl.pallas_call(
        functools.partial(_chunks_kernel, reverse=reverse),
        grid=(T // rows,),
        in_specs=[row] * 5 + [pl.BlockSpec(tri3.shape, lambda i: (0, 0))],
        out_specs=[row] * 9,
        out_shape=[jax.ShapeDtypeStruct((T, C_R), bf16)] * 8 + [jax.ShapeDtypeStruct((T, C_R), f32)],
        compiler_params=pltpu.CompilerParams(dimension_semantics=("parallel",), vmem_limit_bytes=VMEM_LIMIT),
        name="wkv_chunks",
    )(r, nkk, dec, kka, k, tri3)


def _chunk_scan_kernel(vf_ref, vb_ref, *refs, nb):
    vs = (vf_ref, vb_ref)
    ins = (refs[0:9], refs[9:18])
    ys = refs[18:20]
    st_ref = refs[20]

    @pl.when(pl.program_id(1) == 0)
    def _():
        st_ref[...] = jnp.zeros_like(st_ref)

    low = lax.broadcasted_iota(jnp.int32, (CHUNK, LANES), 1) < HEAD_N
    rr = lax.broadcasted_iota(jnp.int32, (LANES, LANES), 0) < HEAD_N
    cc = lax.broadcasted_iota(jnp.int32, (LANES, LANES), 1) < HEAD_N
    same_head = rr == cc
    chains = [(bb, d, p) for bb in range(nb) for d in range(2) for p in range(N_PAIRS)]
    part = lambda bb, d, p, which: ins[d][which][bb, :, p * LANES:(p + 1) * LANES]
    s0s = [st_ref[(bb * 2 + d) * N_PAIRS + p] for bb, d, p in chains]
    vals = [vs[d][bb, :, p * LANES:(p + 1) * LANES] for bb, d, p in chains]
    vbds = [_halves(v, low).astype(bf16) for v in vals]
    gs = [_dot(jnp.concatenate([part(*ch, 0), part(*ch, 1)], axis=0), s0.astype(bf16))
          for ch, s0 in zip(chains, s0s)]
    xs = [g[:CHUNK, :] + _dot(part(*ch, 2), vbd) for ch, g, vbd in zip(chains, gs, vbds)]
    sas = [_dot(part(*ch, 3), _halves(x, low).astype(bf16)) for ch, x in zip(chains, xs)]
    for (bb, d, p), g, sa, vbd in zip(chains, gs, sas, vbds):
        ys[d][bb, :, p * LANES:(p + 1) * LANES] = (
            g[CHUNK:, :] + _dot(part(bb, d, p, 4), _halves(sa, low).astype(bf16)) + _dot(part(bb, d, p, 5), vbd))
    for (bb, d, p), s0, sa, v in zip(chains, s0s, sas, vals):
        upd = lax.dot_general(
            jnp.concatenate([part(bb, d, p, 6), part(bb, d, p, 7)], axis=0),
            jnp.concatenate([sa, v], axis=0).astype(bf16),
            (((0,), (0,)), ((), ())), preferred_element_type=f32)
        wc = part(bb, d, p, 8)
        wcol = jnp.concatenate([wc, wc], axis=0).T
        st_ref[(bb * 2 + d) * N_PAIRS + p] = jnp.where(same_head, wcol * s0 + upd, jnp.zeros_like(s0))


CHUNK_SCAN_BATCH = 4


def _chunk_scan(v, fwd_parts, bwd_parts, *, B, S):
    nC = S // CHUNK
    nb = CHUNK_SCAN_BATCH if B % CHUNK_SCAN_BATCH == 0 else 1
    as3 = lambda a: a.reshape(B, S, C_R)
    fwd = pl.BlockSpec((nb, CHUNK, C_R), lambda b, i: (b, i, 0))
    bwd = pl.BlockSpec((nb, CHUNK, C_R), lambda b, i: (b, nC - 1 - i, 0))
    yf, yb = pl.pallas_call(
        functools.partial(_chunk_scan_kernel, nb=nb),
        grid=(B // nb, nC),
        in_specs=[fwd, bwd] + [fwd] * 9 + [bwd] * 9,
        out_specs=[fwd, bwd],
        out_shape=[jax.ShapeDtypeStruct((B, S, C_R), f32)] * 2,
        scratch_shapes=[pltpu.VMEM((nb * 2 * N_PAIRS, LANES, LANES), f32)],
        compiler_params=pltpu.CompilerParams(
            dimension_semantics=("parallel", "arbitrary"), vmem_limit_bytes=VMEM_LIMIT),
        name="wkv_scan",
    )(as3(v), as3(v), *[as3(a) for a in fwd_parts], *[as3(a) for a in bwd_parts])
    return yf.reshape(B * S, C_R), yb.reshape(B * S, C_R)

def _mix_kernel(x_ref, attn_ref, yf_ref, yb_ref, r_ref, kf_ref, kb_ref, v_ref, g_ref,
                lng_ref, lnb_ref, rk_ref, aog_ref, wout_ref, jseg_ref, h_ref):
    jseg = jseg_ref[...]
    y = yf_ref[...] + yb_ref[...]
    mu = _split_dot(y, jseg) * (1.0 / HEAD_N)
    dlt = y - mu
    var = _split_dot(dlt * dlt, jseg) * (1.0 / HEAD_N)
    yn = dlt * lax.rsqrt(var + GN_EPS) * lng_ref[...] + lnb_ref[...]
    r = r_ref[...]
    k_mean = 0.5 * (kf_ref[...] + kb_ref[...])
    bonus = _split_dot(r * k_mean * rk_ref[...], jseg) * v_ref[...]
    rw = ((yn + bonus) * g_ref[...]).astype(bf16)
    attn = jnp.concatenate([attn_ref[0, p] for p in range(H_A // 2)], axis=-1)
    an = _rms(attn, aog_ref[...]).astype(bf16)
    ha = H_A * V_DIM
    h_ref[...] = x_ref[...] + _dot(an, wout_ref[0:ha, :]) + _dot(rw, wout_ref[ha:, :])


def _mix(x, attn, yf, yb, r, kf, kb, v, g, lng, lnb, rk, aog, wout, jseg, *, B, S, tm):
    nS = S // tm
    const = lambda shape: pl.BlockSpec(shape, lambda b, s: (0,) * len(shape))
    row = lambda c: pl.BlockSpec((tm, c), lambda b, s: (b * nS + s, 0))
    return pl.pallas_call(
        _mix_kernel,
        grid=(B, nS),
        in_specs=[row(D_MODEL), pl.BlockSpec((1, H_A // 2, tm, LANES), lambda b, s: (b, 0, s, 0))]
                 + [row(C_R)] * 7
                 + [const((1, C_R))] * 3 + [const((1, H_A * V_DIM)), const(wout.shape), const(jseg.shape)],
        out_specs=row(D_MODEL),
        out_shape=jax.ShapeDtypeStruct((B * S, D_MODEL), f32),
        compiler_params=pltpu.CompilerParams(
            dimension_semantics=("parallel", "parallel"), vmem_limit_bytes=VMEM_LIMIT),
        name="mix",
    )(x, attn, yf, yb, r, kf, kb, v, g, lng, lnb, rk, aog, wout, jseg)


def _take_max(s, ids):
    m = jnp.max(s, axis=0, keepdims=True)
    pick = jnp.min(jnp.where(s == m, ids, jnp.int32(2 ** 30)), axis=0, keepdims=True)
    return m, pick, jnp.where(ids == pick, -jnp.inf, s)


CAND_COLS = tuple(P_TOPK // (a + 1) for a in range(P_TOPK))
N_CAND = sum(CAND_COLS)
N_CAND_PAD = -(-N_CAND // 8) * 8


def _route_kernel(h_ref, g2_ref, wpq_ref, sk_ref, pos_ref, idx_ref, gate_ref, v1_ref, i1_ref, v2_ref, i2_ref, cs_ref,
                  ci_ref, bs_ref, bi_ref):
    tm = h_ref.shape[0]
    hn = _rms(h_ref[...], g2_ref[...]).astype(bf16)
    pq = _dot(hn, wpq_ref[...])
    key_ids = lax.broadcasted_iota(jnp.int32, (N_KEYS, tm), 0)
    pos_ids = pos_ref[...]
    for p in range(P_HEADS):
        for side, (vs_ref, is_ref) in enumerate(((v1_ref, i1_ref), (v2_ref, i2_ref))):
            qh = pq[:, p * D_KEY + side * HALF_KEY:p * D_KEY + (side + 1) * HALF_KEY].astype(bf16)
            s = _dot_nt(sk_ref[side], qh)
            for j in range(P_TOPK):
                m, pick, s = _take_max(s, key_ids)
                vs_ref[j:j + 1, :] = m
                is_ref[j:j + 1, :] = pick
        v2 = v2_ref[...]
        i2 = i2_ref[...]
        off = 0
        for a, nb in enumerate(CAND_COLS):
            cs_ref[off:off + nb, :] = v1_ref[a:a + 1, :] + v2[:nb, :]
            ci_ref[off:off + nb, :] = i1_ref[a:a + 1, :] * N_KEYS + i2[:nb, :]
            off += nb
        cs_ref[N_CAND:, :] = jnp.full((N_CAND_PAD - N_CAND, tm), -jnp.inf, f32)
        ci_ref[N_CAND:, :] = jnp.zeros((N_CAND_PAD - N_CAND, tm), jnp.int32)
        s = cs_ref[...]
        ci = ci_ref[...]
        for j in range(P_TOPK):
            m, pick, s = _take_max(s, pos_ids)
            bs_ref[j:j + 1, :] = m
            bi_ref[p * P_TOPK + j:p * P_TOPK + j + 1, :] = jnp.max(
                jnp.where(pos_ids == pick, ci, -1), axis=0, keepdims=True)
        bs = bs_ref[...]
        e = jnp.exp(bs - bs[0:1, :])
        gate_ref[:, p * P_TOPK:(p + 1) * P_TOPK] = (e / jnp.sum(e, axis=0, keepdims=True)).T
    idx_ref[...] = bi_ref[...].T


def _route(h, g2, wpq, sk, *, T, tm):
    const = lambda shape: pl.BlockSpec(shape, lambda i: (0,) * len(shape))
    pk = P_HEADS * P_TOPK
    flat = [a * P_TOPK + b for a, nb in enumerate(CAND_COLS) for b in range(nb)] + [2 ** 20] * (N_CAND_PAD - N_CAND)
    pos = jnp.broadcast_to(jnp.array(flat, jnp.int32)[:, None], (N_CAND_PAD, tm))
    return pl.pallas_call(
        _route_kernel,
        grid=(T // tm,),
        in_specs=[pl.BlockSpec((tm, D_MODEL), lambda i: (i, 0)), const((1, D_MODEL)), const(wpq.shape),
                  const(sk.shape), const((N_CAND_PAD, tm))],
        out_specs=[pl.BlockSpec((tm, pk), lambda i: (i, 0))] * 2,
        out_shape=[jax.ShapeDtypeStruct((T, pk), jnp.int32), jax.ShapeDtypeStruct((T, pk), f32)],
        scratch_shapes=[pltpu.VMEM((P_TOPK, tm), f32), pltpu.VMEM((P_TOPK, tm), jnp.int32),
                        pltpu.VMEM((P_TOPK, tm), f32), pltpu.VMEM((P_TOPK, tm), jnp.int32),
                        pltpu.VMEM((N_CAND_PAD, tm), f32), pltpu.VMEM((N_CAND_PAD, tm), jnp.int32),
                        pltpu.VMEM((P_TOPK, tm), f32), pltpu.VMEM((pk, tm), jnp.int32)],
        compiler_params=pltpu.CompilerParams(dimension_semantics=("parallel",), vmem_limit_bytes=VMEM_LIMIT),
        name="route",
    )(h, g2, wpq, sk, pos)


GATHER_WINDOW = 64
SC_CORES = 2
SC_SUBCORES = 16
SC_WORKERS = SC_CORES * SC_SUBCORES


def _gather_rows(table, idx):
    n = idx.shape[0]
    d = table.shape[1]
    win = GATHER_WINDOW
    n_it = n // (SC_WORKERS * win)
    assert n_it * SC_WORKERS * win == n and n_it % 2 == 0, (n, n_it)
    mesh = plsc.VectorSubcoreMesh(core_axis_name="core", subcore_axis_name="subcore")

    @functools.partial(
        pl.kernel, out_type=jax.ShapeDtypeStruct((n, d), table.dtype), mesh=mesh,
        scratch_types=[pltpu.VMEM((n_it, win), jnp.int32), pltpu.VMEM((2, win, d), table.dtype),
                       pltpu.SemaphoreType.DMA, pltpu.SemaphoreType.DMA,
                       pltpu.SemaphoreType.DMA, pltpu.SemaphoreType.DMA])
    def gather(tab_hbm, idx_hbm, out_hbm, idx_v, rows_v, gsem0, gsem1, wsem0, wsem1):
        wid = lax.axis_index("subcore") * SC_CORES + lax.axis_index("core")
        base = wid * (n_it * win)
        gsem = (gsem0, gsem1)
        wsem = (wsem0, wsem1)
        pltpu.sync_copy(idx_hbm.at[wid], idx_v)

        def fetch(j, b):
            return pltpu.make_async_copy(tab_hbm.at[idx_v.at[j]], rows_v.at[b], gsem[b])

        def flush(j, b):
            return pltpu.make_async_copy(rows_v.at[b], out_hbm.at[pl.ds(base + j * win, win)], wsem[b])

        fetch(0, 0).start()

        @pl.loop(0, n_it, step=2)
        def _(j0):
            for b in range(2):
                j = j0 + b

                @pl.when(j >= 1)
                def _():
                    flush(j - 1, 1 - b).wait()

                @pl.when(j + 1 < n_it)
                def _():
                    fetch(j + 1, 1 - b).start()

                fetch(j, b).wait()
                flush(j, b).start()

        flush(n_it - 1, 1).wait()

    return gather(table, idx.reshape(SC_WORKERS, n_it, win))


HALF_D = D_MODEL // 2


def _pack_rows(table):
    bits = lax.bitcast_convert_type(table.astype(bf16), jnp.uint16).astype(jnp.uint32)
    return bits[:, :HALF_D] | (bits[:, HALF_D:] << 16)


def _unpack_rows(words):
    lo = lax.bitcast_convert_type(words << 16, f32).astype(bf16)
    hi = lax.bitcast_convert_type(words & jnp.uint32(0xFFFF0000), f32).astype(bf16)
    return lo, hi


def _expert_kernel(h_ref, g2_ref, gate_ref, gu_ref, gv_ref, o_ref):
    h = h_ref[...]
    tb = h.shape[0]
    pk = P_HEADS * P_TOPK
    hn = _rms(h, g2_ref[...]).astype(bf16)
    row = lax.broadcasted_iota(jnp.int32, (8, pk), 0)
    for grp in range(tb // 8):
        rows = slice(grp * 8, (grp + 1) * 8)
        hn8 = hn[rows, :]
        act = jnp.zeros((8, pk), f32)
        for j in range(8):
            t = grp * 8 + j
            ulo, uhi = _unpack_rows(gu_ref[t * pk:(t + 1) * pk, :])
            act = jnp.where(row == j, _dot_nt(hn8[:, :HALF_D], ulo) + _dot_nt(hn8[:, HALF_D:], uhi), act)
        gelu = 0.5 * act * (1.0 + lax.erf(act * (2.0 ** -0.5)))
        w = (gate_ref[rows, :] * gelu).astype(bf16)
        out_lo = jnp.zeros((8, HALF_D), f32)
        out_hi = jnp.zeros((8, HALF_D), f32)
        for j in range(8):
            t = grp * 8 + j
            vlo, vhi = _unpack_rows(gv_ref[t * pk:(t + 1) * pk, :])
            wj = jnp.where(row == j, w, jnp.zeros_like(w))
            out_lo = out_lo + _dot(wj, vlo)
            out_hi = out_hi + _dot(wj, vhi)
        o_ref[rows, :HALF_D] = h[rows, :HALF_D] + out_lo
        o_ref[rows, HALF_D:] = h[rows, HALF_D:] + out_hi


def _experts(h, g2, gate, gu, gv, *, tb):
    n = h.shape[0]
    pk = P_HEADS * P_TOPK
    const = lambda shape: pl.BlockSpec(shape, lambda i: (0,) * len(shape))
    return pl.pallas_call(
        _expert_kernel,
        grid=(n // tb,),
        in_specs=[pl.BlockSpec((tb, D_MODEL), lambda i: (i, 0)), const((1, D_MODEL)),
                  pl.BlockSpec((tb, pk), lambda i: (i, 0)),
                  pl.BlockSpec((tb * pk, HALF_D), lambda i: (i, 0)),
                  pl.BlockSpec((tb * pk, HALF_D), lambda i: (i, 0))],
        out_specs=pl.BlockSpec((tb, D_MODEL), lambda i: (i, 0)),
        out_shape=jax.ShapeDtypeStruct((n, D_MODEL), f32),
        compiler_params=pltpu.CompilerParams(dimension_semantics=("parallel",), vmem_limit_bytes=VMEM_LIMIT),
        name="experts",
    )(h, g2, gate, gu, gv)


def _place(cols, width, offset):
    return jnp.pad(cols, ((0, 0), (offset, width - offset - cols.shape[1])))


def _rope_partner():
    half = ROPE_DIM // 2
    return jnp.concatenate([jnp.arange(half, ROPE_DIM), jnp.arange(0, half)])


def _prepare(norm1_g, w_in, q_lat_g, w_uq, kv_lat_g, w_ukv, q_norm_g, k_norm_g, attn_out_g, mu_prev, mu_next,
             w0, w_up, a0, a_up, g_up, k_k, k_a, r_k, ln_x_g, ln_x_b, w_out, norm2_g, w_pq, sub_keys, S):
    partner = _rope_partner()
    wz = w_in[:, OFF_RWKV:]
    wkr = w_in[:, OFF_KR:OFF_RWKV]
    win = jnp.concatenate([
        w_in[:, OFF_Q:OFF_KV], w_in[:, OFF_KV:OFF_KR], wz,
        _place(wkr, LANES, NOPE_DIM), _place(wkr[:, partner], LANES, NOPE_DIM)], axis=1).astype(bf16)

    wq = w_uq.reshape(Q_LORA, H_A, QK_DIM)
    wqa = jnp.pad(wq, ((0, 0), (0, 0), (0, HEAD_PAD - QK_DIM))).reshape(Q_LORA, H_A * HEAD_PAD).astype(bf16)
    wqb = jnp.pad(wq[:, :, NOPE_DIM:][:, :, partner],
                  ((0, 0), (0, 0), (NOPE_DIM, HEAD_PAD - QK_DIM))).reshape(Q_LORA, H_A * HEAD_PAD).astype(bf16)
    wkv = w_ukv.reshape(KV_LORA, H_A, NOPE_DIM + V_DIM)
    wkn = jnp.pad(wkv[:, :, :NOPE_DIM], ((0, 0), (0, 0), (0, HEAD_PAD - NOPE_DIM))).reshape(
        KV_LORA, H_A * HEAD_PAD).astype(bf16)
    wv = jnp.pad(wkv[:, :, NOPE_DIM:], ((0, 0), (0, 0), (0, LANES - V_DIM))).reshape(
        KV_LORA, H_A * LANES).astype(bf16)

    def gain_rows(g):
        ga = _place(g[None, :], LANES, 0)
        gb = _place(g[None, NOPE_DIM:][:, partner], LANES, NOPE_DIM)
        return ga, gb

    gqa, gqb = gain_rows(q_norm_g)
    gq = jnp.concatenate([gqa, gqb, jnp.zeros((6, LANES), f32)], axis=0)
    gkn = _place(k_norm_g[None, :NOPE_DIM], LANES, 0)
    gka = _place(k_norm_g[None, NOPE_DIM:], LANES, NOPE_DIM)
    gkb = _place(k_norm_g[None, NOPE_DIM:][:, partner], LANES, NOPE_DIM)
    gk = jnp.concatenate([gkn, gka, gkb, jnp.zeros((5, LANES), f32)], axis=0)

    half = ROPE_DIM // 2
    inv = 1.0 / (ROPE_THETA ** (jnp.arange(half, dtype=f32) / half))
    ang = jnp.arange(S, dtype=f32)[:, None] * inv[None, :]
    c, s = jnp.cos(ang), jnp.sin(ang)
    cos = jnp.concatenate([jnp.ones((S, NOPE_DIM), f32), c, c, jnp.zeros((S, HEAD_PAD - QK_DIM), f32)], axis=1)
    sin = jnp.concatenate([jnp.zeros((S, NOPE_DIM), f32), -s, s, jnp.zeros((S, HEAD_PAD - QK_DIM), f32)], axis=1)

    zeros = jnp.zeros((W_LORA, C_R), f32)
    wl = jnp.stack([jnp.concatenate([jnp.concatenate([w_up[d], zeros], axis=1),
                                     jnp.concatenate([zeros, a_up[d]], axis=1)], axis=0) for d in range(2)]).astype(bf16)
    seg = jnp.arange(C_R) // HEAD_N
    jseg = (seg[:, None] == seg[None, :]).astype(bf16)

    return dict(
        g1=norm1_g[None, :], win=win, qlg=q_lat_g[None, :], wqa=wqa, wqb=wqb, kvlg=kv_lat_g[None, :], wkn=wkn, wv=wv,
        cos=cos, sin=sin, gq=gq, gk=gk,
        mup=mu_prev[None, :], mun=mu_next[None, :], wl=wl, w0=w0, a0=a0,
        gup=g_up.astype(bf16), kk=k_k[None, :], ka=k_a[None, :], jseg=jseg,
        lng=ln_x_g[None, :], lnb=ln_x_b[None, :], rk=r_k[None, :], aog=attn_out_g[None, :],
        wout=w_out.astype(bf16), g2=norm2_g[None, :], wpq=w_pq.astype(bf16), sk=sub_keys.astype(bf16),
    )


def _tile(n, pref):
    t = min(n, pref)
    assert n % t == 0, (n, t)
    return t


PEER_CHUNK = 4096


def _dense(x, w):
    B, S, _ = x.shape
    T = B * S
    xt = x.reshape(T, D_MODEL)
    tm = _tile(S, 256)
    q, k, v, z = _inproj(xt, w["g1"], w["win"], w["qlg"], w["wqa"], w["wqb"], w["kvlg"], w["wkn"], w["wv"],
                         w["cos"], w["sin"], w["gq"], w["gk"], B=B, S=S, tm=tm)
    attn = _attention(q, k, v, B=B, S=S, tq=_tile(S, 512), tk=_tile(S // 2, 512))
    r, vr, nkk, g, decf, kkaf, kf, decb, kkab, kb = _rwkv_prep(
        z, w["mup"], w["mun"], w["wl"], w["w0"], w["a0"], w["gup"], w["kk"], w["ka"], w["jseg"], B=B, S=S, tm=tm)
    fw = _wkv_chunks(r, nkk, decf, kkaf, kf, T=T, reverse=False)
    bw = _wkv_chunks(r, nkk, decb, kkab, kb, T=T, reverse=True)
    yf, yb = _chunk_scan(vr, fw, bw, B=B, S=S)
    h = _mix(xt, attn, yf, yb, r, kf, kb, vr, g, w["lng"], w["lnb"], w["rk"], w["aog"], w["wout"], w["jseg"],
             B=B, S=S, tm=tm)
    idx, gate = _route(h, w["g2"], w["wpq"], w["sk"], T=T, tm=_tile(T, 256))
    return h, idx, gate


def _peer(h, idx, gate, w, pu, pv, shape):
    T = h.shape[0]
    pk = P_HEADS * P_TOPK
    tc = _tile(T, PEER_CHUNK)
    outs = []
    for c in range(T // tc):
        sl = slice(c * tc, (c + 1) * tc)
        flat = idx[sl].reshape(tc * pk)
        gu = _gather_rows(pu, flat)
        gv = _gather_rows(pv, flat)
        outs.append(_experts(h[sl], w["g2"], gate[sl], gu, gv, tb=_tile(tc, 16)))
    return jnp.concatenate(outs, axis=0).reshape(shape)


def _layer(xs, w, expert_u, expert_v):
    pu = _pack_rows(expert_u)
    pv = _pack_rows(expert_v)
    outs = []
    pending = None
    for x in xs:
        cur = _dense(x, w) + (x.shape,)
        if pending is not None:
            outs.append(_peer(*pending[:3], w, pu, pv, pending[3]))
        pending = cur
    outs.append(_peer(*pending[:3], w, pu, pv, pending[3]))
    return outs


GROUP_BATCH = 1


def kernel(x_prompt, x_sample, norm1_g, w_in, q_lat_g, w_uq, kv_lat_g, w_ukv, q_norm_g, k_norm_g, attn_out_g, mu_prev, mu_next, w0, w_up, a0, a_up, g_up, k_k, k_a, r_k, ln_x_g, ln_x_b, w_out, norm2_g, w_pq, sub_keys, expert_u, expert_v):
    assert x_prompt.shape[1] == x_sample.shape[1]
    S = x_prompt.shape[1]
    groups = []
    for x in (x_prompt, x_sample):
        gb = GROUP_BATCH if x.shape[0] % GROUP_BATCH == 0 else x.shape[0]
        groups.append([x[i:i + gb] for i in range(0, x.shape[0], gb)])
    xs = groups[0] + groups[1]
    for l in range(norm1_g.shape[0]):
        w = _prepare(norm1_g[l], w_in[l], q_lat_g[l], w_uq[l], kv_lat_g[l], w_ukv[l], q_norm_g[l], k_norm_g[l],
                     attn_out_g[l], mu_prev[l], mu_next[l], w0[l], w_up[l], a0[l], a_up[l], g_up[l], k_k[l], k_a[l],
                     r_k[l], ln_x_g[l], ln_x_b[l], w_out[l], norm2_g[l], w_pq[l], sub_keys[l], S)
        xs = _layer(xs, w, expert_u[l], expert_v[l])
    n0 = len(groups[0])
    join = lambda parts: parts[0] if len(parts) == 1 else jnp.concatenate(parts, axis=0)
    return join(xs[:n0]), join(xs[n0:])
```

```python
import functools
import math

import jax
import jax.numpy as jnp
from jax import lax
from jax.experimental import pallas as pl
from jax.experimental.pallas import tpu as pltpu
from jax.experimental.pallas import tpu_sc as plsc

D_MODEL = 1024
H_A = 8
NOPE_DIM = 64
ROPE_DIM = 32
QK_DIM = NOPE_DIM + ROPE_DIM
V_DIM = 64
Q_LORA = 384
KV_LORA = 256
ROPE_THETA = 10000.0
H_R = 8
HEAD_N = 64
C_R = H_R * HEAD_N
W_LORA = 64
A_LORA = 64
G_LORA = 128
GN_EPS = 64e-5
RWKV_IN = 3 * C_R + W_LORA + A_LORA + G_LORA
OFF_Q = 0
OFF_KV = OFF_Q + Q_LORA
OFF_KR = OFF_KV + KV_LORA
OFF_RWKV = OFF_KR + ROPE_DIM
N_KEYS = 128
N_EXPERTS = N_KEYS * N_KEYS
P_HEADS = 8
P_TOPK = 16
D_KEY = 256
HALF_KEY = D_KEY // 2
NORM_EPS = 1e-6

LANES = 128
HEAD_PAD = LANES
Z_COLS = RWKV_IN
IN_PAD = Q_LORA + KV_LORA + Z_COLS + 2 * LANES
VMEM_LIMIT = 56 * 1024 * 1024

f32 = jnp.float32
bf16 = jnp.bfloat16


def _rms(x, g):
    return x * lax.rsqrt(jnp.mean(x * x, axis=-1, keepdims=True) + NORM_EPS) * g


def _dot(a, b):
    return jnp.dot(a, b, preferred_element_type=f32)


def _dot_nt(a, b):
    return lax.dot_general(a, b, (((1,), (1,)), ((), ())), preferred_element_type=f32)


def _split_dot(x, j):
    hi = x.astype(bf16)
    lo = (x - hi.astype(f32)).astype(bf16)
    return _dot(hi, j) + _dot(lo, j)


def _inproj_kernel(x_ref, g1_ref, win_ref, qlg_ref, wqa_ref, wqb_ref, kvlg_ref, wkn_ref, wv_ref,
                   cos_ref, sin_ref, gq_ref, gk_ref,
                   q_ref, k_ref, v_ref, z_ref):
    x = x_ref[...]
    xn = _rms(x, g1_ref[...]).astype(bf16)
    proj = _dot(xn, win_ref[...])
    o_kv, o_z, o_ra, o_rb = Q_LORA, Q_LORA + KV_LORA, Q_LORA + KV_LORA + Z_COLS, Q_LORA + KV_LORA + Z_COLS + LANES
    z_ref[...] = proj[:, o_z:o_ra]
    cos = cos_ref[...]
    sin = sin_ref[...]

    ql = _rms(proj[:, :o_kv], qlg_ref[...]).astype(bf16)
    qa = _dot(ql, wqa_ref[...])
    qb = _dot(ql, wqb_ref[...])
    ga, gb = gq_ref[0:1, :], gq_ref[1:2, :]
    scale = QK_DIM ** -0.5 * math.log2(math.e)
    for h in range(H_A):
        a = qa[:, h * LANES:(h + 1) * LANES]
        b = qb[:, h * LANES:(h + 1) * LANES]
        s = lax.rsqrt(jnp.sum(a * a, axis=-1, keepdims=True) * (1.0 / QK_DIM) + NORM_EPS) * scale
        q_ref[0, h] = (s * (a * ga * cos + b * gb * sin)).astype(bf16)

    kvl = _rms(proj[:, o_kv:o_z], kvlg_ref[...]).astype(bf16)
    kn = _dot(kvl, wkn_ref[...])
    vv = _dot(kvl, wv_ref[...])
    one = (lax.broadcasted_iota(jnp.int32, (1, LANES), 1) == V_DIM).astype(f32)
    for h in range(H_A):
        v_ref[0, h] = (vv[:, h * LANES:(h + 1) * LANES] + one).astype(bf16)
    kra = proj[:, o_ra:o_rb]
    krb = proj[:, o_rb:o_rb + LANES]
    gkn, gka, gkb = gk_ref[0:1, :], gk_ref[1:2, :], gk_ref[2:3, :]
    kr = kra * gka * cos + krb * gkb * sin
    ssr = jnp.sum(kra * kra, axis=-1, keepdims=True)
    for h in range(H_A):
        a = kn[:, h * LANES:(h + 1) * LANES]
        s = lax.rsqrt((jnp.sum(a * a, axis=-1, keepdims=True) + ssr) * (1.0 / QK_DIM) + NORM_EPS)
        k_ref[0, h] = (s * (a * gkn + kr)).astype(bf16)


def _inproj(x, g1, win, qlg, wqa, wqb, kvlg, wkn, wv, cos, sin, gq, gk, *, B, S, tm):
    nS = S // tm
    const = lambda shape: pl.BlockSpec(shape, lambda b, s: (0,) * len(shape))
    return pl.pallas_call(
        _inproj_kernel,
        grid=(B, nS),
        in_specs=[
            pl.BlockSpec((tm, D_MODEL), lambda b, s: (b * nS + s, 0)),
            const((1, D_MODEL)), const(win.shape), const((1, Q_LORA)), const(wqa.shape), const(wqb.shape),
            const((1, KV_LORA)), const(wkn.shape), const(wv.shape),
            pl.BlockSpec((tm, LANES), lambda b, s: (s, 0)),
            pl.BlockSpec((tm, LANES), lambda b, s: (s, 0)),
            const(gq.shape), const(gk.shape),
        ],
        out_specs=[
            pl.BlockSpec((1, H_A, tm, HEAD_PAD), lambda b, s: (b, 0, s, 0)),
            pl.BlockSpec((1, H_A, tm, HEAD_PAD), lambda b, s: (b, 0, s, 0)),
            pl.BlockSpec((1, H_A, tm, LANES), lambda b, s: (b, 0, s, 0)),
            pl.BlockSpec((tm, Z_COLS), lambda b, s: (b * nS + s, 0)),
        ],
        out_shape=[
            jax.ShapeDtypeStruct((B, H_A, S, HEAD_PAD), bf16),
            jax.ShapeDtypeStruct((B, H_A, S, HEAD_PAD), bf16),
            jax.ShapeDtypeStruct((B, H_A, S, LANES), bf16),
            jax.ShapeDtypeStruct((B * S, Z_COLS), f32),
        ],
        compiler_params=pltpu.CompilerParams(
            dimension_semantics=("parallel", "parallel"), vmem_limit_bytes=VMEM_LIMIT),
        name="in_proj",
    )(x, g1, win, qlg, wqa, wqb, kvlg, wkn, wv, cos, sin, gq, gk)


ATTN_ROWS = 32
ATTN_UNROLL = 16


def _attn_kernel(q_ref, k_ref, v_ref, o_ref, s_ref, p_ref, m_ref, al_ref, acc_ref, *, tk, nk):
    tq = q_ref.shape[2]
    nlt = tk // LANES

    def scores(h, q, j, buf):
        off = pl.multiple_of(jnp.minimum(j, nk - 1) * tk, tk)
        s_ref[buf] = _dot_nt(q, k_ref[0, h, pl.ds(off, tk), :])

    def softmax(buf):
        def chunk(c, carry):
            rows = pl.ds(pl.multiple_of(c * ATTN_ROWS, ATTN_ROWS), ATTN_ROWS)
            tiles = [s_ref[buf, rows, lt * LANES:(lt + 1) * LANES] for lt in range(nlt)]
            best = functools.reduce(jnp.maximum, tiles)
            m_old = m_ref[rows, :]
            m_new = jnp.maximum(m_old, jnp.broadcast_to(jnp.max(best, axis=-1, keepdims=True), m_old.shape))
            for lt in range(nlt):
                p_ref[buf, rows, lt * LANES:(lt + 1) * LANES] = jnp.exp2(tiles[lt] - m_new).astype(bf16)
            al_ref[rows, :] = jnp.exp2(m_old - m_new)
            m_ref[rows, :] = m_new
            return carry

        lax.fori_loop(0, tq // ATTN_ROWS, chunk, 0, unroll=ATTN_UNROLL)

    def accumulate(h, j, buf):
        off = pl.multiple_of(j * tk, tk)
        acc_ref[...] = al_ref[...] * acc_ref[...] + _dot(p_ref[buf], v_ref[0, h, pl.ds(off, tk), :])

    outs = []
    for h in range(2):
        q = q_ref[0, h]
        m_ref[...] = jnp.full(m_ref.shape, -jnp.inf, f32)
        acc_ref[...] = jnp.zeros(acc_ref.shape, f32)
        scores(h, q, 0, 0)

        def body(jj, carry, h=h, q=q):
            j = 2 * jj
            scores(h, q, j + 1, 1)
            softmax(0)
            accumulate(h, j, 0)
            scores(h, q, j + 2, 0)
            softmax(1)
            accumulate(h, j + 1, 1)
            return carry

        lax.fori_loop(0, nk // 2, body, 0)
        acc = acc_ref[...]
        outs.append(acc / acc[:, V_DIM:V_DIM + 1])
    lane = lax.broadcasted_iota(jnp.int32, (tq, LANES), 1)
    o_ref[0, 0] = jnp.where(lane < V_DIM, outs[0], pltpu.roll(outs[1], V_DIM, 1))


def _attention(q, k, v, *, B, S, tq, tk):
    return pl.pallas_call(
        functools.partial(_attn_kernel, tk=tk, nk=S // tk),
        grid=(B, H_A // 2, S // tq),
        in_specs=[
            pl.BlockSpec((1, 2, tq, HEAD_PAD), lambda b, p, i: (b, p, i, 0)),
            pl.BlockSpec((1, 2, S, HEAD_PAD), lambda b, p, i: (b, p, 0, 0)),
            pl.BlockSpec((1, 2, S, LANES), lambda b, p, i: (b, p, 0, 0)),
        ],
        out_specs=pl.BlockSpec((1, 1, tq, LANES), lambda b, p, i: (b, p, i, 0)),
        out_shape=jax.ShapeDtypeStruct((B, H_A // 2, S, LANES), f32),
        scratch_shapes=[pltpu.VMEM((2, tq, tk), f32), pltpu.VMEM((2, tq, tk), bf16), pltpu.VMEM((tq, LANES), f32),
                        pltpu.VMEM((tq, LANES), f32), pltpu.VMEM((tq, LANES), f32)],
        compiler_params=pltpu.CompilerParams(
            dimension_semantics=("parallel", "parallel", "parallel"), vmem_limit_bytes=VMEM_LIMIT),
        name="attention",
    )(q, k, v)


def _softplus(x):
    return jnp.maximum(x, 0.0) + jnp.log1p(jnp.exp(-jnp.abs(x)))


def _prep_kernel(z_ref, zp_ref, zn_ref, mup_ref, mun_ref, wl_ref, w0_ref, a0_ref, gup_ref, kk_ref, ka_ref, jseg_ref,
                 r_ref, v_ref, nkk_ref, g_ref,
                 decf_ref, kkaf_ref, kf_ref, decb_ref, kkab_ref, kb_ref):
    si = pl.program_id(1)
    ns = pl.num_programs(1)
    z = z_ref[...]
    tm = z.shape[0]
    row = lax.broadcasted_iota(jnp.int32, (tm, 1), 0)
    zp_row = jnp.where(si == 0, 0.0, zp_ref[7:8, :])
    zn_row = jnp.where(si == ns - 1, 0.0, zn_ref[0:1, :])
    z_prev = jnp.where(row == 0, zp_row, pltpu.roll(z, 1, 0))
    z_next = jnp.where(row == tm - 1, zn_row, pltpu.roll(z, tm - 1, 0))
    zm = z + mup_ref[...] * (z_prev - z) + mun_ref[...] * (z_next - z)

    o1, o2, o3 = C_R, 2 * C_R, 3 * C_R
    o4 = o3 + W_LORA + A_LORA
    o5 = o4 + G_LORA
    r = zm[:, :o1]
    kr = zm[:, o1:o2]
    vr = zm[:, o2:o3]
    lw = zm[:, o3:o4]
    zg = zm[:, o4:o5]
    r_ref[...] = r
    v_ref[...] = vr
    g_ref[...] = _dot(jax.nn.sigmoid(zg).astype(bf16), gup_ref[...])

    kk = kr * kk_ref[...]
    ssk = _split_dot(kk * kk, jseg_ref[...])
    kk = kk / jnp.maximum(jnp.sqrt(ssk), 1e-12)
    nkk_ref[...] = -kk

    lane = lax.broadcasted_iota(jnp.int32, lw.shape, 1)
    lin = jnp.where(lane < W_LORA, jnp.tanh(lw), lw).astype(bf16)
    ka = ka_ref[...]
    for d, (dec_ref, kka_ref, kd_ref) in enumerate(((decf_ref, kkaf_ref, kf_ref), (decb_ref, kkab_ref, kb_ref))):
        lo = _dot(lin, wl_ref[d])
        w = -_softplus(-(w0_ref[d:d + 1, :] + lo[:, :C_R])) - 0.5
        dec_ref[...] = jnp.exp(-jnp.exp(w))
        a = jax.nn.sigmoid(a0_ref[d:d + 1, :] + lo[:, C_R:])
        kd_ref[...] = kr * (1.0 + (a - 1.0) * ka)
        kka_ref[...] = kk * a


def _rwkv_prep(z, mup, mun, wl, w0, a0, gup, kk, ka, jseg, *, B, S, tm):
    nS = S // tm
    T = B * S
    nb8 = tm // 8
    const = lambda shape: pl.BlockSpec(shape, lambda b, s: (0,) * len(shape))
    row_spec = pl.BlockSpec((tm, C_R), lambda b, s: (b * nS + s, 0))
    return pl.pallas_call(
        _prep_kernel,
        grid=(B, nS),
        in_specs=[
            pl.BlockSpec((tm, Z_COLS), lambda b, s: (b * nS + s, 0)),
            pl.BlockSpec((8, Z_COLS), lambda b, s: (jnp.maximum((b * nS + s) * nb8 - 1, 0), 0)),
            pl.BlockSpec((8, Z_COLS), lambda b, s: (jnp.minimum((b * nS + s + 1) * nb8, T // 8 - 1), 0)),
            const((1, Z_COLS)), const((1, Z_COLS)), const(wl.shape), const(w0.shape), const(a0.shape),
            const(gup.shape), const((1, C_R)), const((1, C_R)), const(jseg.shape),
        ],
        out_specs=[row_spec] * 10,
        out_shape=[jax.ShapeDtypeStruct((T, C_R), f32)] * 10,
        compiler_params=pltpu.CompilerParams(
            dimension_semantics=("parallel", "parallel"), vmem_limit_bytes=VMEM_LIMIT),
        name="rwkv_prep",
    )(z, z, z, mup, mun, wl, w0, a0, gup, kk, ka, jseg)


N_PAIRS = H_R // 2
CHUNK = 64


def _halves(x, low):
    zero = jnp.zeros_like(x)
    return jnp.concatenate([jnp.where(low, x, zero), jnp.where(low, zero, x)], axis=0)


def _chunks_kernel(r_ref, nkk_ref, dec_ref, kka_ref, k_ref, tri_ref,
                   at_ref, rt_ref, aak_ref, tt_ref, arb_ref, ark_ref, bt_ref, kt_ref, wc_ref, *, reverse):
    lane = lax.broadcasted_iota(jnp.int32, (CHUNK, LANES), 1)
    low = lane < HEAD_N
    jj = lax.broadcasted_iota(jnp.int32, (CHUNK, LANES), 0)
    ii = lane & (HEAD_N - 1)
    strict = (ii > jj) if reverse else (ii < jj)
    incl = (ii >= jj) if reverse else (ii <= jj)
    rr = lax.broadcasted_iota(jnp.int32, (LANES, LANES), 0)
    cc = lax.broadcasted_iota(jnp.int32, (LANES, LANES), 1)
    eye = (rr == cc).astype(f32)
    last = 0 if reverse else CHUNK - 1
    swap = lambda x: pltpu.roll(x, HEAD_N, 1)
    zero = jnp.zeros((CHUNK, LANES), f32)
    units = [(slice(q * CHUNK, (q + 1) * CHUNK), slice(p * LANES, (p + 1) * LANES))
             for q in range(r_ref.shape[0] // CHUNK) for p in range(N_PAIRS)]
    cums = []
    for rows, cols in units:
        lw = jnp.log(dec_ref[rows, cols])
        hi = lw.astype(bf16)
        r1 = lw - hi.astype(f32)
        mid = r1.astype(bf16)
        lo = (r1 - mid.astype(f32)).astype(bf16)
        cums.append((lw, _dot(tri_ref[...], jnp.concatenate([hi, mid, lo], axis=0))))
    outs = []
    for (rows, cols), (lw, cum) in zip(units, cums):
        w_in = jnp.exp(cum)
        w_ex = jnp.exp(cum - lw)
        w_inv = jnp.exp(-cum)
        wc = w_in[last:last + 1, :]
        at = nkk_ref[rows, cols] * w_ex
        rt = r_ref[rows, cols] * w_in
        bt = kka_ref[rows, cols] * w_inv
        kt = k_ref[rows, cols] * w_inv
        at_ref[rows, cols] = at.astype(bf16)
        rt_ref[rows, cols] = rt.astype(bf16)
        bt_ref[rows, cols] = (bt * wc).astype(bf16)
        kt_ref[rows, cols] = (kt * wc).astype(bf16)
        wc_ref[rows, cols] = jnp.broadcast_to(wc, (CHUNK, LANES))
        lhs = jnp.concatenate([_halves(at, low), _halves(rt, low)], axis=0).astype(bf16)
        rhs = jnp.concatenate([bt, kt], axis=0).astype(bf16)
        outs.append(_dot_nt(lhs, rhs))
    pws, ts = [], []
    for (rows, cols), o in zip(units, outs):
        o0, o1, o2, o3 = (o[q * CHUNK:(q + 1) * CHUNK, :] for q in range(4))
        aab = jnp.where(strict, jnp.where(low, o0, swap(o1)), zero)
        aak_ref[rows, cols] = jnp.where(strict, jnp.where(low, swap(o0), o1), zero).astype(bf16)
        arb_ref[rows, cols] = jnp.where(incl, jnp.where(low, o2, swap(o3)), zero).astype(bf16)
        ark_ref[rows, cols] = jnp.where(incl, jnp.where(low, swap(o2), o3), zero).astype(bf16)
        pws.append(_halves(aab, low))
        ts.append(eye + pws[-1])
    for _ in range(int(math.log2(CHUNK)) - 1):
        pbs = [pw.astype(bf16) for pw in pws]
        pws = [_dot(pb, pb) for pb in pbs]
        ts = [t + _dot(t.astype(bf16), pw.astype(bf16)) for t, pw in zip(ts, pws)]
    for (rows, cols), t in zip(units, ts):
        tt_ref[rows, cols] = jnp.where(low, t[:CHUNK, :], t[CHUNK:, :]).astype(bf16)


CHUNKS_PER_STEP = 4


def _wkv_chunks(r, nkk, dec, kka, k, *, T, reverse):
    tri_i = jnp.arange(CHUNK)
    tri = (tri_i[None, :] >= tri_i[:, None]) if reverse else (tri_i[None, :] <= tri_i[:, None])
    tri3 = jnp.tile(tri.astype(bf16), (1, 3))
    rows = CHUNK * CHUNKS_PER_STEP
    assert T % rows == 0
    row = pl.BlockSpec((rows, C_R), lambda i: (i, 0))
    return pl.pallas_call(
        functools.partial(_chunks_kernel, reverse=reverse),
        grid=(T // rows,),
        in_specs=[row] * 5 + [pl.BlockSpec(tri3.shape, lambda i: (0, 0))],
        out_specs=[row] * 9,
        out_shape=[jax.ShapeDtypeStruct((T, C_R), bf16)] * 8 + [jax.ShapeDtypeStruct((T, C_R), f32)],
        compiler_params=pltpu.CompilerParams(dimension_semantics=("parallel",), vmem_limit_bytes=VMEM_LIMIT),
        name="wkv_chunks",
    )(r, nkk, dec, kka, k, tri3)


def _chunk_scan_kernel(vf_ref, vb_ref, *refs, nb):
    vs = (vf_ref, vb_ref)
    ins = (refs[0:9], refs[9:18])
    ys = refs[18:20]
    st_ref = refs[20]

    @pl.when(pl.program_id(1) == 0)
    def _():
        st_ref[...] = jnp.zeros_like(st_ref)

    low = lax.broadcasted_iota(jnp.int32, (CHUNK, LANES), 1) < HEAD_N
    rr = lax.broadcasted_iota(jnp.int32, (LANES, LANES), 0) < HEAD_N
    cc = lax.broadcasted_iota(jnp.int32, (LANES, LANES), 1) < HEAD_N
    same_head = rr == cc
    chains = [(bb, d, p) for bb in range(nb) for d in range(2) for p in range(N_PAIRS)]
    part = lambda bb, d, p, which: ins[d][which][bb, :, p * LANES:(p + 1) * LANES]
    s0s = [st_ref[(bb * 2 + d) * N_PAIRS + p] for bb, d, p in chains]
    vals = [vs[d][bb, :, p * LANES:(p + 1) * LANES] for bb, d, p in chains]
    vbds = [_halves(v, low).astype(bf16) for v in vals]
    gs = [_dot(jnp.concatenate([part(*ch, 0), part(*ch, 1)], axis=0), s0.astype(bf16))
          for ch, s0 in zip(chains, s0s)]
    xs = [g[:CHUNK, :] + _dot(part(*ch, 2), vbd) for ch, g, vbd in zip(chains, gs, vbds)]
    sas = [_dot(part(*ch, 3), _halves(x, low).astype(bf16)) for ch, x in zip(chains, xs)]
    for (bb, d, p), g, sa, vbd in zip(chains, gs, sas, vbds):
        ys[d][bb, :, p * LANES:(p + 1) * LANES] = (
            g[CHUNK:, :] + _dot(part(bb, d, p, 4), _halves(sa, low).astype(bf16)) + _dot(part(bb, d, p, 5), vbd))
    for (bb, d, p), s0, sa, v in zip(chains, s0s, sas, vals):
        upd = lax.dot_general(
            jnp.concatenate([part(bb, d, p, 6), part(bb, d, p, 7)], axis=0),
            jnp.concatenate([sa, v], axis=0).astype(bf16),
            (((0,), (0,)), ((), ())), preferred_element_type=f32)
        wc = part(bb, d, p, 8)
        wcol = jnp.concatenate([wc, wc], axis=0).T
        st_ref[(bb * 2 + d) * N_PAIRS + p] = jnp.where(same_head, wcol * s0 + upd, jnp.zeros_like(s0))


CHUNK_SCAN_BATCH = 4


def _chunk_scan(v, fwd_parts, bwd_parts, *, B, S):
    nC = S // CHUNK
    nb = CHUNK_SCAN_BATCH if B % CHUNK_SCAN_BATCH == 0 else 1
    as3 = lambda a: a.reshape(B, S, C_R)
    fwd = pl.BlockSpec((nb, CHUNK, C_R), lambda b, i: (b, i, 0))
    bwd = pl.BlockSpec((nb, CHUNK, C_R), lambda b, i: (b, nC - 1 - i, 0))
    yf, yb = pl.pallas_call(
        functools.partial(_chunk_scan_kernel, nb=nb),
        grid=(B // nb, nC),
        in_specs=[fwd, bwd] + [fwd] * 9 + [bwd] * 9,
        out_specs=[fwd, bwd],
        out_shape=[jax.ShapeDtypeStruct((B, S, C_R), f32)] * 2,
        scratch_shapes=[pltpu.VMEM((nb * 2 * N_PAIRS, LANES, LANES), f32)],
        compiler_params=pltpu.CompilerParams(
            dimension_semantics=("parallel", "arbitrary"), vmem_limit_bytes=VMEM_LIMIT),
        name="wkv_scan",
    )(as3(v), as3(v), *[as3(a) for a in fwd_parts], *[as3(a) for a in bwd_parts])
    return yf.reshape(B * S, C_R), yb.reshape(B * S, C_R)

def _mix_kernel(x_ref, attn_ref, yf_ref, yb_ref, r_ref, kf_ref, kb_ref, v_ref, g_ref,
                lng_ref, lnb_ref, rk_ref, aog_ref, wout_ref, jseg_ref, h_ref):
    jseg = jseg_ref[...]
    y = yf_ref[...] + yb_ref[...]
    mu = _split_dot(y, jseg) * (1.0 / HEAD_N)
    dlt = y - mu
    var = _split_dot(dlt * dlt, jseg) * (1.0 / HEAD_N)
    yn = dlt * lax.rsqrt(var + GN_EPS) * lng_ref[...] + lnb_ref[...]
    r = r_ref[...]
    k_mean = 0.5 * (kf_ref[...] + kb_ref[...])
    bonus = _split_dot(r * k_mean * rk_ref[...], jseg) * v_ref[...]
    rw = ((yn + bonus) * g_ref[...]).astype(bf16)
    attn = jnp.concatenate([attn_ref[0, p] for p in range(H_A // 2)], axis=-1)
    an = _rms(attn, aog_ref[...]).astype(bf16)
    ha = H_A * V_DIM
    h_ref[...] = x_ref[...] + _dot(an, wout_ref[0:ha, :]) + _dot(rw, wout_ref[ha:, :])


def _mix(x, attn, yf, yb, r, kf, kb, v, g, lng, lnb, rk, aog, wout, jseg, *, B, S, tm):
    nS = S // tm
    const = lambda shape: pl.BlockSpec(shape, lambda b, s: (0,) * len(shape))
    row = lambda c: pl.BlockSpec((tm, c), lambda b, s: (b * nS + s, 0))
    return pl.pallas_call(
        _mix_kernel,
        grid=(B, nS),
        in_specs=[row(D_MODEL), pl.BlockSpec((1, H_A // 2, tm, LANES), lambda b, s: (b, 0, s, 0))]
                 + [row(C_R)] * 7
                 + [const((1, C_R))] * 3 + [const((1, H_A * V_DIM)), const(wout.shape), const(jseg.shape)],
        out_specs=row(D_MODEL),
        out_shape=jax.ShapeDtypeStruct((B * S, D_MODEL), f32),
        compiler_params=pltpu.CompilerParams(
            dimension_semantics=("parallel", "parallel"), vmem_limit_bytes=VMEM_LIMIT),
        name="mix",
    )(x, attn, yf, yb, r, kf, kb, v, g, lng, lnb, rk, aog, wout, jseg)


def _take_max(s, ids):
    m = jnp.max(s, axis=0, keepdims=True)
    pick = jnp.min(jnp.where(s == m, ids, jnp.int32(2 ** 30)), axis=0, keepdims=True)
    return m, pick, jnp.where(ids == pick, -jnp.inf, s)


CAND_COLS = tuple(P_TOPK // (a + 1) for a in range(P_TOPK))
N_CAND = sum(CAND_COLS)
N_CAND_PAD = -(-N_CAND // 8) * 8


def _route_kernel(h_ref, g2_ref, wpq_ref, sk_ref, pos_ref, idx_ref, gate_ref, v1_ref, i1_ref, v2_ref, i2_ref, cs_ref,
                  ci_ref, bs_ref, bi_ref):
    tm = h_ref.shape[0]
    hn = _rms(h_ref[...], g2_ref[...]).astype(bf16)
    pq = _dot(hn, wpq_ref[...])
    key_ids = lax.broadcasted_iota(jnp.int32, (N_KEYS, tm), 0)
    pos_ids = pos_ref[...]
    for p in range(P_HEADS):
        for side, (vs_ref, is_ref) in enumerate(((v1_ref, i1_ref), (v2_ref, i2_ref))):
            qh = pq[:, p * D_KEY + side * HALF_KEY:p * D_KEY + (side + 1) * HALF_KEY].astype(bf16)
            s = _dot_nt(sk_ref[side], qh)
            for j in range(P_TOPK):
                m, pick, s = _take_max(s, key_ids)
                vs_ref[j:j + 1, :] = m
                is_ref[j:j + 1, :] = pick
        v2 = v2_ref[...]
        i2 = i2_ref[...]
        off = 0
        for a, nb in enumerate(CAND_COLS):
            cs_ref[off:off + nb, :] = v1_ref[a:a + 1, :] + v2[:nb, :]
            ci_ref[off:off + nb, :] = i1_ref[a:a + 1, :] * N_KEYS + i2[:nb, :]
            off += nb
        cs_ref[N_CAND:, :] = jnp.full((N_CAND_PAD - N_CAND, tm), -jnp.inf, f32)
        ci_ref[N_CAND:, :] = jnp.zeros((N_CAND_PAD - N_CAND, tm), jnp.int32)
        s = cs_ref[...]
        ci = ci_ref[...]
        for j in range(P_TOPK):
            m, pick, s = _take_max(s, pos_ids)
            bs_ref[j:j + 1, :] = m
            bi_ref[p * P_TOPK + j:p * P_TOPK + j + 1, :] = jnp.max(
                jnp.where(pos_ids == pick, ci, -1), axis=0, keepdims=True)
        bs = bs_ref[...]
        e = jnp.exp(bs - bs[0:1, :])
        gate_ref[:, p * P_TOPK:(p + 1) * P_TOPK] = (e / jnp.sum(e, axis=0, keepdims=True)).T
    idx_ref[...] = bi_ref[...].T


def _route(h, g2, wpq, sk, *, T, tm):
    const = lambda shape: pl.BlockSpec(shape, lambda i: (0,) * len(shape))
    pk = P_HEADS * P_TOPK
    flat = [a * P_TOPK + b for a, nb in enumerate(CAND_COLS) for b in range(nb)] + [2 ** 20] * (N_CAND_PAD - N_CAND)
    pos = jnp.broadcast_to(jnp.array(flat, jnp.int32)[:, None], (N_CAND_PAD, tm))
    return pl.pallas_call(
        _route_kernel,
        grid=(T // tm,),
        in_specs=[pl.BlockSpec((tm, D_MODEL), lambda i: (i, 0)), const((1, D_MODEL)), const(wpq.shape),
                  const(sk.shape), const((N_CAND_PAD, tm))],
        out_specs=[pl.BlockSpec((tm, pk), lambda i: (i, 0))] * 2,
        out_shape=[jax.ShapeDtypeStruct((T, pk), jnp.int32), jax.ShapeDtypeStruct((T, pk), f32)],
        scratch_shapes=[pltpu.VMEM((P_TOPK, tm), f32), pltpu.VMEM((P_TOPK, tm), jnp.int32),
                        pltpu.VMEM((P_TOPK, tm), f32), pltpu.VMEM((P_TOPK, tm), jnp.int32),
                        pltpu.VMEM((N_CAND_PAD, tm), f32), pltpu.VMEM((N_CAND_PAD, tm), jnp.int32),
                        pltpu.VMEM((P_TOPK, tm), f32), pltpu.VMEM((pk, tm), jnp.int32)],
        compiler_params=pltpu.CompilerParams(dimension_semantics=("parallel",), vmem_limit_bytes=VMEM_LIMIT),
        name="route",
    )(h, g2, wpq, sk, pos)


GATHER_WINDOW = 64
SC_CORES = 2
SC_SUBCORES = 16
SC_WORKERS = SC_CORES * SC_SUBCORES


def _gather_rows(table, idx):
    n = idx.shape[0]
    d = table.shape[1]
    win = GATHER_WINDOW
    n_it = n // (SC_WORKERS * win)
    assert n_it * SC_WORKERS * win == n and n_it % 2 == 0, (n, n_it)
    mesh = plsc.VectorSubcoreMesh(core_axis_name="core", subcore_axis_name="subcore")

    @functools.partial(
        pl.kernel, out_type=jax.ShapeDtypeStruct((n, d), table.dtype), mesh=mesh,
        scratch_types=[pltpu.VMEM((n_it, win), jnp.int32), pltpu.VMEM((2, win, d), table.dtype),
                       pltpu.SemaphoreType.DMA, pltpu.SemaphoreType.DMA,
                       pltpu.SemaphoreType.DMA, pltpu.SemaphoreType.DMA])
    def gather(tab_hbm, idx_hbm, out_hbm, idx_v, rows_v, gsem0, gsem1, wsem0, wsem1):
        wid = lax.axis_index("subcore") * SC_CORES + lax.axis_index("core")
        base = wid * (n_it * win)
        gsem = (gsem0, gsem1)
        wsem = (wsem0, wsem1)
        pltpu.sync_copy(idx_hbm.at[wid], idx_v)

        def fetch(j, b):
            return pltpu.make_async_copy(tab_hbm.at[idx_v.at[j]], rows_v.at[b], gsem[b])

        def flush(j, b):
            return pltpu.make_async_copy(rows_v.at[b], out_hbm.at[pl.ds(base + j * win, win)], wsem[b])

        fetch(0, 0).start()

        @pl.loop(0, n_it, step=2)
        def _(j0):
            for b in range(2):
                j = j0 + b

                @pl.when(j >= 1)
                def _():
                    flush(j - 1, 1 - b).wait()

                @pl.when(j + 1 < n_it)
                def _():
                    fetch(j + 1, 1 - b).start()

                fetch(j, b).wait()
                flush(j, b).start()

        flush(n_it - 1, 1).wait()

    return gather(table, idx.reshape(SC_WORKERS, n_it, win))


HALF_D = D_MODEL // 2


def _pack_rows(table):
    bits = lax.bitcast_convert_type(table.astype(bf16), jnp.uint16).astype(jnp.uint32)
    return bits[:, :HALF_D] | (bits[:, HALF_D:] << 16)


def _unpack_rows(words):
    lo = lax.bitcast_convert_type(words << 16, f32).astype(bf16)
    hi = lax.bitcast_convert_type(words & jnp.uint32(0xFFFF0000), f32).astype(bf16)
    return lo, hi


def _expert_kernel(h_ref, g2_ref, gate_ref, gu_ref, gv_ref, o_ref):
    h = h_ref[...]
    tb = h.shape[0]
    pk = P_HEADS * P_TOPK
    hn = _rms(h, g2_ref[...]).astype(bf16)
    row = lax.broadcasted_iota(jnp.int32, (8, pk), 0)
    for grp in range(tb // 8):
        rows = slice(grp * 8, (grp + 1) * 8)
        hn8 = hn[rows, :]
        act = jnp.zeros((8, pk), f32)
        for j in range(8):
            t = grp * 8 + j
            ulo, uhi = _unpack_rows(gu_ref[t * pk:(t + 1) * pk, :])
            act = jnp.where(row == j, _dot_nt(hn8[:, :HALF_D], ulo) + _dot_nt(hn8[:, HALF_D:], uhi), act)
        gelu = 0.5 * act * (1.0 + lax.erf(act * (2.0 ** -0.5)))
        w = (gate_ref[rows, :] * gelu).astype(bf16)
        out_lo = jnp.zeros((8, HALF_D), f32)
        out_hi = jnp.zeros((8, HALF_D), f32)
        for j in range(8):
            t = grp * 8 + j
            vlo, vhi = _unpack_rows(gv_ref[t * pk:(t + 1) * pk, :])
            wj = jnp.where(row == j, w, jnp.zeros_like(w))
            out_lo = out_lo + _dot(wj, vlo)
            out_hi = out_hi + _dot(wj, vhi)
        o_ref[rows, :HALF_D] = h[rows, :HALF_D] + out_lo
        o_ref[rows, HALF_D:] = h[rows, HALF_D:] + out_hi


def _experts(h, g2, gate, gu, gv, *, tb):
    n = h.shape[0]
    pk = P_HEADS * P_TOPK
    const = lambda shape: pl.BlockSpec(shape, lambda i: (0,) * len(shape))
    return pl.pallas_call(
        _expert_kernel,
        grid=(n // tb,),
        in_specs=[pl.BlockSpec((tb, D_MODEL), lambda i: (i, 0)), const((1, D_MODEL)),
                  pl.BlockSpec((tb, pk), lambda i: (i, 0)),
                  pl.BlockSpec((tb * pk, HALF_D), lambda i: (i, 0)),
                  pl.BlockSpec((tb * pk, HALF_D), lambda i: (i, 0))],
        out_specs=pl.BlockSpec((tb, D_MODEL), lambda i: (i, 0)),
        out_shape=jax.ShapeDtypeStruct((n, D_MODEL), f32),
        compiler_params=pltpu.CompilerParams(dimension_semantics=("parallel",), vmem_limit_bytes=VMEM_LIMIT),
        name="experts",
    )(h, g2, gate, gu, gv)


def _place(cols, width, offset):
    return jnp.pad(cols, ((0, 0), (offset, width - offset - cols.shape[1])))


def _rope_partner():
    half = ROPE_DIM // 2
    return jnp.concatenate([jnp.arange(half, ROPE_DIM), jnp.arange(0, half)])


def _prepare(norm1_g, w_in, q_lat_g, w_uq, kv_lat_g, w_ukv, q_norm_g, k_norm_g, attn_out_g, mu_prev, mu_next,
             w0, w_up, a0, a_up, g_up, k_k, k_a, r_k, ln_x_g, ln_x_b, w_out, norm2_g, w_pq, sub_keys, S):
    partner = _rope_partner()
    wz = w_in[:, OFF_RWKV:]
    wkr = w_in[:, OFF_KR:OFF_RWKV]
    win = jnp.concatenate([
        w_in[:, OFF_Q:OFF_KV], w_in[:, OFF_KV:OFF_KR], wz,
        _place(wkr, LANES, NOPE_DIM), _place(wkr[:, partner], LANES, NOPE_DIM)], axis=1).astype(bf16)

    wq = w_uq.reshape(Q_LORA, H_A, QK_DIM)
    wqa = jnp.pad(wq, ((0, 0), (0, 0), (0, HEAD_PAD - QK_DIM))).reshape(Q_LORA, H_A * HEAD_PAD).astype(bf16)
    wqb = jnp.pad(wq[:, :, NOPE_DIM:][:, :, partner],
                  ((0, 0), (0, 0), (NOPE_DIM, HEAD_PAD - QK_DIM))).reshape(Q_LORA, H_A * HEAD_PAD).astype(bf16)
    wkv = w_ukv.reshape(KV_LORA, H_A, NOPE_DIM + V_DIM)
    wkn = jnp.pad(wkv[:, :, :NOPE_DIM], ((0, 0), (0, 0), (0, HEAD_PAD - NOPE_DIM))).reshape(
        KV_LORA, H_A * HEAD_PAD).astype(bf16)
    wv = jnp.pad(wkv[:, :, NOPE_DIM:], ((0, 0), (0, 0), (0, LANES - V_DIM))).reshape(
        KV_LORA, H_A * LANES).astype(bf16)

    def gain_rows(g):
        ga = _place(g[None, :], LANES, 0)
        gb = _place(g[None, NOPE_DIM:][:, partner], LANES, NOPE_DIM)
        return ga, gb

    gqa, gqb = gain_rows(q_norm_g)
    gq = jnp.concatenate([gqa, gqb, jnp.zeros((6, LANES), f32)], axis=0)
    gkn = _place(k_norm_g[None, :NOPE_DIM], LANES, 0)
    gka = _place(k_norm_g[None, NOPE_DIM:], LANES, NOPE_DIM)
    gkb = _place(k_norm_g[None, NOPE_DIM:][:, partner], LANES, NOPE_DIM)
    gk = jnp.concatenate([gkn, gka, gkb, jnp.zeros((5, LANES), f32)], axis=0)

    half = ROPE_DIM // 2
    inv = 1.0 / (ROPE_THETA ** (jnp.arange(half, dtype=f32) / half))
    ang = jnp.arange(S, dtype=f32)[:, None] * inv[None, :]
    c, s = jnp.cos(ang), jnp.sin(ang)
    cos = jnp.concatenate([jnp.ones((S, NOPE_DIM), f32), c, c, jnp.zeros((S, HEAD_PAD - QK_DIM), f32)], axis=1)
    sin = jnp.concatenate([jnp.zeros((S, NOPE_DIM), f32), -s, s, jnp.zeros((S, HEAD_PAD - QK_DIM), f32)], axis=1)

    zeros = jnp.zeros((W_LORA, C_R), f32)
    wl = jnp.stack([jnp.concatenate([jnp.concatenate([w_up[d], zeros], axis=1),
                                     jnp.concatenate([zeros, a_up[d]], axis=1)], axis=0) for d in range(2)]).astype(bf16)
    seg = jnp.arange(C_R) // HEAD_N
    jseg = (seg[:, None] == seg[None, :]).astype(bf16)

    return dict(
        g1=norm1_g[None, :], win=win, qlg=q_lat_g[None, :], wqa=wqa, wqb=wqb, kvlg=kv_lat_g[None, :], wkn=wkn, wv=wv,
        cos=cos, sin=sin, gq=gq, gk=gk,
        mup=mu_prev[None, :], mun=mu_next[None, :], wl=wl, w0=w0, a0=a0,
        gup=g_up.astype(bf16), kk=k_k[None, :], ka=k_a[None, :], jseg=jseg,
        lng=ln_x_g[None, :], lnb=ln_x_b[None, :], rk=r_k[None, :], aog=attn_out_g[None, :],
        wout=w_out.astype(bf16), g2=norm2_g[None, :], wpq=w_pq.astype(bf16), sk=sub_keys.astype(bf16),
    )


def _tile(n, pref):
    t = min(n, pref)
    assert n % t == 0, (n, t)
    return t


PEER_CHUNK = 4096


def _dense(x, w):
    B, S, _ = x.shape
    T = B * S
    xt = x.reshape(T, D_MODEL)
    tm = _tile(S, 256)
    q, k, v, z = _inproj(xt, w["g1"], w["win"], w["qlg"], w["wqa"], w["wqb"], w["kvlg"], w["wkn"], w["wv"],
                         w["cos"], w["sin"], w["gq"], w["gk"], B=B, S=S, tm=tm)
    attn = _attention(q, k, v, B=B, S=S, tq=_tile(S, 512), tk=_tile(S // 2, 512))
    r, vr, nkk, g, decf, kkaf, kf, decb, kkab, kb = _rwkv_prep(
        z, w["mup"], w["mun"], w["wl"], w["w0"], w["a0"], w["gup"], w["kk"], w["ka"], w["jseg"], B=B, S=S, tm=tm)
    fw = _wkv_chunks(r, nkk, decf, kkaf, kf, T=T, reverse=False)
    bw = _wkv_chunks(r, nkk, decb, kkab, kb, T=T, reverse=True)
    yf, yb = _chunk_scan(vr, fw, bw, B=B, S=S)
    h = _mix(xt, attn, yf, yb, r, kf, kb, vr, g, w["lng"], w["lnb"], w["rk"], w["aog"], w["wout"], w["jseg"],
             B=B, S=S, tm=tm)
    idx, gate = _route(h, w["g2"], w["wpq"], w["sk"], T=T, tm=_tile(T, 256))
    return h, idx, gate


def _peer(h, idx, gate, w, pu, pv, shape):
    T = h.shape[0]
    pk = P_HEADS * P_TOPK
    tc = _tile(T, PEER_CHUNK)
    outs = []
    for c in range(T // tc):
        sl = slice(c * tc, (c + 1) * tc)
        flat = idx[sl].reshape(tc * pk)
        gu = _gather_rows(pu, flat)
        gv = _gather_rows(pv, flat)
        outs.append(_experts(h[sl], w["g2"], gate[sl], gu, gv, tb=_tile(tc, 16)))
    return jnp.concatenate(outs, axis=0).reshape(shape)


def _layer(xs, w, expert_u, expert_v):
    pu = _pack_rows(expert_u)
    pv = _pack_rows(expert_v)
    outs = []
    pending = []
    for i, x in enumerate(xs):
        done = i - PIPE_LAG - 1
        if done >= 0:
            x, outs[done] = lax.optimization_barrier((x, outs[done]))
        pending.append(_dense(x, w) + (x.shape,))
        if len(pending) > PIPE_LAG:
            h, idx, gate, shape = pending.pop(0)
            outs.append(_peer(h, idx, gate, w, pu, pv, shape))
    for h, idx, gate, shape in pending:
        outs.append(_peer(h, idx, gate, w, pu, pv, shape))
    return outs


GROUP_BATCH = 1
PIPE_LAG = 2


def kernel(x_prompt, x_sample, norm1_g, w_in, q_lat_g, w_uq, kv_lat_g, w_ukv, q_norm_g, k_norm_g, attn_out_g, mu_prev, mu_next, w0, w_up, a0, a_up, g_up, k_k, k_a, r_k, ln_x_g, ln_x_b, w_out, norm2_g, w_pq, sub_keys, expert_u, expert_v):
    assert x_prompt.shape[1] == x_sample.shape[1]
    S = x_prompt.shape[1]
    groups = []
    for x in (x_prompt, x_sample):
        gb = GROUP_BATCH if x.shape[0] % GROUP_BATCH == 0 else x.shape[0]
        groups.append([x[i:i + gb] for i in range(0, x.shape[0], gb)])
    xs = groups[0] + groups[1]
    for l in range(norm1_g.shape[0]):
        w = _prepare(norm1_g[l], w_in[l], q_lat_g[l], w_uq[l], kv_lat_g[l], w_ukv[l], q_norm_g[l], k_norm_g[l],
                     attn_out_g[l], mu_prev[l], mu_next[l], w0[l], w_up[l], a0[l], a_up[l], g_up[l], k_k[l], k_a[l],
                     r_k[l], ln_x_g[l], ln_x_b[l], w_out[l], norm2_g[l], w_pq[l], sub_keys[l], S)
        xs = _layer(xs, w, expert_u[l], expert_v[l])
    n0 = len(groups[0])
    join = lambda parts: parts[0] if len(parts) == 1 else jnp.concatenate(parts, axis=0)
    return join(xs[:n0]), join(xs[n0:])
```

```python
import functools
import math

import jax
import jax.numpy as jnp
from jax import lax
from jax.experimental import pallas as pl
from jax.experimental.pallas import tpu as pltpu
from jax.experimental.pallas import tpu_sc as plsc

D_MODEL = 1024
H_A = 8
NOPE_DIM = 64
ROPE_DIM = 32
QK_DIM = NOPE_DIM + ROPE_DIM
V_DIM = 64
Q_LORA = 384
KV_LORA = 256
ROPE_THETA = 10000.0
H_R = 8
HEAD_N = 64
C_R = H_R * HEAD_N
W_LORA = 64
A_LORA = 64
G_LORA = 128
GN_EPS = 64e-5
RWKV_IN = 3 * C_R + W_LORA + A_LORA + G_LORA
OFF_Q = 0
OFF_KV = OFF_Q + Q_LORA
OFF_KR = OFF_KV + KV_LORA
OFF_RWKV = OFF_KR + ROPE_DIM
N_KEYS = 128
N_EXPERTS = N_KEYS * N_KEYS
P_HEADS = 8
P_TOPK = 16
D_KEY = 256
HALF_KEY = D_KEY // 2
NORM_EPS = 1e-6

LANES = 128
HEAD_PAD = LANES
Z_COLS = RWKV_IN
IN_PAD = Q_LORA + KV_LORA + Z_COLS + 2 * LANES
VMEM_LIMIT = 56 * 1024 * 1024

f32 = jnp.float32
bf16 = jnp.bfloat16


def _rms(x, g):
    return x * lax.rsqrt(jnp.mean(x * x, axis=-1, keepdims=True) + NORM_EPS) * g


def _dot(a, b):
    return jnp.dot(a, b, preferred_element_type=f32)


def _dot_nt(a, b):
    return lax.dot_general(a, b, (((1,), (1,)), ((), ())), preferred_element_type=f32)


def _split_dot(x, j):
    hi = x.astype(bf16)
    lo = (x - hi.astype(f32)).astype(bf16)
    return _dot(hi, j) + _dot(lo, j)


def _inproj_kernel(x_ref, g1_ref, win_ref, qlg_ref, wqa_ref, wqb_ref, kvlg_ref, wkn_ref, wv_ref,
                   cos_ref, sin_ref, gq_ref, gk_ref,
                   q_ref, k_ref, v_ref, z_ref):
    x = x_ref[...]
    xn = _rms(x, g1_ref[...]).astype(bf16)
    proj = _dot(xn, win_ref[...])
    o_kv, o_z, o_ra, o_rb = Q_LORA, Q_LORA + KV_LORA, Q_LORA + KV_LORA + Z_COLS, Q_LORA + KV_LORA + Z_COLS + LANES
    z_ref[...] = proj[:, o_z:o_ra]
    cos = cos_ref[...]
    sin = sin_ref[...]

    ql = _rms(proj[:, :o_kv], qlg_ref[...]).astype(bf16)
    qa = _dot(ql, wqa_ref[...])
    qb = _dot(ql, wqb_ref[...])
    ga, gb = gq_ref[0:1, :], gq_ref[1:2, :]
    scale = QK_DIM ** -0.5 * math.log2(math.e)
    for h in range(H_A):
        a = qa[:, h * LANES:(h + 1) * LANES]
        b = qb[:, h * LANES:(h + 1) * LANES]
        s = lax.rsqrt(jnp.sum(a * a, axis=-1, keepdims=True) * (1.0 / QK_DIM) + NORM_EPS) * scale
        q_ref[0, h] = (s * (a * ga * cos + b * gb * sin)).astype(bf16)

    kvl = _rms(proj[:, o_kv:o_z], kvlg_ref[...]).astype(bf16)
    kn = _dot(kvl, wkn_ref[...])
    vv = _dot(kvl, wv_ref[...])
    one = (lax.broadcasted_iota(jnp.int32, (1, LANES), 1) == V_DIM).astype(f32)
    for h in range(H_A):
        v_ref[0, h] = (vv[:, h * LANES:(h + 1) * LANES] + one).astype(bf16)
    kra = proj[:, o_ra:o_rb]
    krb = proj[:, o_rb:o_rb + LANES]
    gkn, gka, gkb = gk_ref[0:1, :], gk_ref[1:2, :], gk_ref[2:3, :]
    kr = kra * gka * cos + krb * gkb * sin
    ssr = jnp.sum(kra * kra, axis=-1, keepdims=True)
    for h in range(H_A):
        a = kn[:, h * LANES:(h + 1) * LANES]
        s = lax.rsqrt((jnp.sum(a * a, axis=-1, keepdims=True) + ssr) * (1.0 / QK_DIM) + NORM_EPS)
        k_ref[0, h] = (s * (a * gkn + kr)).astype(bf16)


def _inproj(x, g1, win, qlg, wqa, wqb, kvlg, wkn, wv, cos, sin, gq, gk, *, B, S, tm):
    nS = S // tm
    const = lambda shape: pl.BlockSpec(shape, lambda b, s: (0,) * len(shape))
    return pl.pallas_call(
        _inproj_kernel,
        grid=(B, nS),
        in_specs=[
            pl.BlockSpec((tm, D_MODEL), lambda b, s: (b * nS + s, 0)),
            const((1, D_MODEL)), const(win.shape), const((1, Q_LORA)), const(wqa.shape), const(wqb.shape),
            const((1, KV_LORA)), const(wkn.shape), const(wv.shape),
            pl.BlockSpec((tm, LANES), lambda b, s: (s, 0)),
            pl.BlockSpec((tm, LANES), lambda b, s: (s, 0)),
            const(gq.shape), const(gk.shape),
        ],
        out_specs=[
            pl.BlockSpec((1, H_A, tm, HEAD_PAD), lambda b, s: (b, 0, s, 0)),
            pl.BlockSpec((1, H_A, tm, HEAD_PAD), lambda b, s: (b, 0, s, 0)),
            pl.BlockSpec((1, H_A, tm, LANES), lambda b, s: (b, 0, s, 0)),
            pl.BlockSpec((tm, Z_COLS), lambda b, s: (b * nS + s, 0)),
        ],
        out_shape=[
            jax.ShapeDtypeStruct((B, H_A, S, HEAD_PAD), bf16),
            jax.ShapeDtypeStruct((B, H_A, S, HEAD_PAD), bf16),
            jax.ShapeDtypeStruct((B, H_A, S, LANES), bf16),
            jax.ShapeDtypeStruct((B * S, Z_COLS), f32),
        ],
        compiler_params=pltpu.CompilerParams(
            dimension_semantics=("parallel", "parallel"), vmem_limit_bytes=VMEM_LIMIT),
        name="in_proj",
    )(x, g1, win, qlg, wqa, wqb, kvlg, wkn, wv, cos, sin, gq, gk)


ATTN_ROWS = 32
ATTN_UNROLL = 16


def _attn_kernel(q_ref, k_ref, v_ref, o_ref, s_ref, p_ref, m_ref, al_ref, acc_ref, *, tk, nk):
    tq = q_ref.shape[2]
    nlt = tk // LANES

    def scores(h, q, j, buf):
        off = pl.multiple_of(jnp.minimum(j, nk - 1) * tk, tk)
        s_ref[buf] = _dot_nt(q, k_ref[0, h, pl.ds(off, tk), :])

    def softmax(buf):
        def chunk(c, carry):
            rows = pl.ds(pl.multiple_of(c * ATTN_ROWS, ATTN_ROWS), ATTN_ROWS)
            tiles = [s_ref[buf, rows, lt * LANES:(lt + 1) * LANES] for lt in range(nlt)]
            best = functools.reduce(jnp.maximum, tiles)
            m_old = m_ref[rows, :]
            m_new = jnp.maximum(m_old, jnp.broadcast_to(jnp.max(best, axis=-1, keepdims=True), m_old.shape))
            for lt in range(nlt):
                p_ref[buf, rows, lt * LANES:(lt + 1) * LANES] = jnp.exp2(tiles[lt] - m_new).astype(bf16)
            al_ref[rows, :] = jnp.exp2(m_old - m_new)
            m_ref[rows, :] = m_new
            return carry

        lax.fori_loop(0, tq // ATTN_ROWS, chunk, 0, unroll=ATTN_UNROLL)

    def accumulate(h, j, buf):
        off = pl.multiple_of(j * tk, tk)
        acc_ref[...] = al_ref[...] * acc_ref[...] + _dot(p_ref[buf], v_ref[0, h, pl.ds(off, tk), :])

    outs = []
    for h in range(2):
        q = q_ref[0, h]
        m_ref[...] = jnp.full(m_ref.shape, -jnp.inf, f32)
        acc_ref[...] = jnp.zeros(acc_ref.shape, f32)
        scores(h, q, 0, 0)

        def body(jj, carry, h=h, q=q):
            j = 2 * jj
            scores(h, q, j + 1, 1)
            softmax(0)
            accumulate(h, j, 0)
            scores(h, q, j + 2, 0)
            softmax(1)
            accumulate(h, j + 1, 1)
            return carry

        lax.fori_loop(0, nk // 2, body, 0)
        acc = acc_ref[...]
        outs.append(acc / acc[:, V_DIM:V_DIM + 1])
    lane = lax.broadcasted_iota(jnp.int32, (tq, LANES), 1)
    o_ref[0, 0] = jnp.where(lane < V_DIM, outs[0], pltpu.roll(outs[1], V_DIM, 1))


def _attention(q, k, v, *, B, S, tq, tk):
    return pl.pallas_call(
        functools.partial(_attn_kernel, tk=tk, nk=S // tk),
        grid=(B, H_A // 2, S // tq),
        in_specs=[
            pl.BlockSpec((1, 2, tq, HEAD_PAD), lambda b, p, i: (b, p, i, 0)),
            pl.BlockSpec((1, 2, S, HEAD_PAD), lambda b, p, i: (b, p, 0, 0)),
            pl.BlockSpec((1, 2, S, LANES), lambda b, p, i: (b, p, 0, 0)),
        ],
        out_specs=pl.BlockSpec((1, 1, tq, LANES), lambda b, p, i: (b, p, i, 0)),
        out_shape=jax.ShapeDtypeStruct((B, H_A // 2, S, LANES), f32),
        scratch_shapes=[pltpu.VMEM((2, tq, tk), f32), pltpu.VMEM((2, tq, tk), bf16), pltpu.VMEM((tq, LANES), f32),
                        pltpu.VMEM((tq, LANES), f32), pltpu.VMEM((tq, LANES), f32)],
        compiler_params=pltpu.CompilerParams(
            dimension_semantics=("parallel", "parallel", "parallel"), vmem_limit_bytes=VMEM_LIMIT),
        name="attention",
    )(q, k, v)


def _softplus(x):
    return jnp.maximum(x, 0.0) + jnp.log1p(jnp.exp(-jnp.abs(x)))


def _prep_kernel(z_ref, zp_ref, zn_ref, mup_ref, mun_ref, wl_ref, w0_ref, a0_ref, gup_ref, kk_ref, ka_ref, jseg_ref,
                 r_ref, v_ref, nkk_ref, g_ref,
                 decf_ref, kkaf_ref, kf_ref, decb_ref, kkab_ref, kb_ref):
    si = pl.program_id(1)
    ns = pl.num_programs(1)
    z = z_ref[...]
    tm = z.shape[0]
    row = lax.broadcasted_iota(jnp.int32, (tm, 1), 0)
    zp_row = jnp.where(si == 0, 0.0, zp_ref[7:8, :])
    zn_row = jnp.where(si == ns - 1, 0.0, zn_ref[0:1, :])
    z_prev = jnp.where(row == 0, zp_row, pltpu.roll(z, 1, 0))
    z_next = jnp.where(row == tm - 1, zn_row, pltpu.roll(z, tm - 1, 0))
    zm = z + mup_ref[...] * (z_prev - z) + mun_ref[...] * (z_next - z)

    o1, o2, o3 = C_R, 2 * C_R, 3 * C_R
    o4 = o3 + W_LORA + A_LORA
    o5 = o4 + G_LORA
    r = zm[:, :o1]
    kr = zm[:, o1:o2]
    vr = zm[:, o2:o3]
    lw = zm[:, o3:o4]
    zg = zm[:, o4:o5]
    r_ref[...] = r
    v_ref[...] = vr
    g_ref[...] = _dot(jax.nn.sigmoid(zg).astype(bf16), gup_ref[...])

    kk = kr * kk_ref[...]
    ssk = _split_dot(kk * kk, jseg_ref[...])
    kk = kk / jnp.maximum(jnp.sqrt(ssk), 1e-12)
    nkk_ref[...] = -kk

    lane = lax.broadcasted_iota(jnp.int32, lw.shape, 1)
    lin = jnp.where(lane < W_LORA, jnp.tanh(lw), lw).astype(bf16)
    ka = ka_ref[...]
    for d, (dec_ref, kka_ref, kd_ref) in enumerate(((decf_ref, kkaf_ref, kf_ref), (decb_ref, kkab_ref, kb_ref))):
        lo = _dot(lin, wl_ref[d])
        w = -_softplus(-(w0_ref[d:d + 1, :] + lo[:, :C_R])) - 0.5
        dec_ref[...] = jnp.exp(-jnp.exp(w))
        a = jax.nn.sigmoid(a0_ref[d:d + 1, :] + lo[:, C_R:])
        kd_ref[...] = kr * (1.0 + (a - 1.0) * ka)
        kka_ref[...] = kk * a


def _rwkv_prep(z, mup, mun, wl, w0, a0, gup, kk, ka, jseg, *, B, S, tm):
    nS = S // tm
    T = B * S
    nb8 = tm // 8
    const = lambda shape: pl.BlockSpec(shape, lambda b, s: (0,) * len(shape))
    row_spec = pl.BlockSpec((tm, C_R), lambda b, s: (b * nS + s, 0))
    return pl.pallas_call(
        _prep_kernel,
        grid=(B, nS),
        in_specs=[
            pl.BlockSpec((tm, Z_COLS), lambda b, s: (b * nS + s, 0)),
            pl.BlockSpec((8, Z_COLS), lambda b, s: (jnp.maximum((b * nS + s) * nb8 - 1, 0), 0)),
            pl.BlockSpec((8, Z_COLS), lambda b, s: (jnp.minimum((b * nS + s + 1) * nb8, T // 8 - 1), 0)),
            const((1, Z_COLS)), const((1, Z_COLS)), const(wl.shape), const(w0.shape), const(a0.shape),
            const(gup.shape), const((1, C_R)), const((1, C_R)), const(jseg.shape),
        ],
        out_specs=[row_spec] * 10,
        out_shape=[jax.ShapeDtypeStruct((T, C_R), f32)] * 10,
        compiler_params=pltpu.CompilerParams(
            dimension_semantics=("parallel", "parallel"), vmem_limit_bytes=VMEM_LIMIT),
        name="rwkv_prep",
    )(z, z, z, mup, mun, wl, w0, a0, gup, kk, ka, jseg)


N_PAIRS = H_R // 2
CHUNK = 64


def _halves(x, low):
    zero = jnp.zeros_like(x)
    return jnp.concatenate([jnp.where(low, x, zero), jnp.where(low, zero, x)], axis=0)


def _chunks_kernel(r_ref, nkk_ref, dec_ref, kka_ref, k_ref, tri_ref,
                   at_ref, rt_ref, aak_ref, tt_ref, arb_ref, ark_ref, bt_ref, kt_ref, wc_ref, *, reverse):
    lane = lax.broadcasted_iota(jnp.int32, (CHUNK, LANES), 1)
    low = lane < HEAD_N
    jj = lax.broadcasted_iota(jnp.int32, (CHUNK, LANES), 0)
    ii = lane & (HEAD_N - 1)
    strict = (ii > jj) if reverse else (ii < jj)
    incl = (ii >= jj) if reverse else (ii <= jj)
    rr = lax.broadcasted_iota(jnp.int32, (LANES, LANES), 0)
    cc = lax.broadcasted_iota(jnp.int32, (LANES, LANES), 1)
    eye = (rr == cc).astype(f32)
    last = 0 if reverse else CHUNK - 1
    swap = lambda x: pltpu.roll(x, HEAD_N, 1)
    zero = jnp.zeros((CHUNK, LANES), f32)
    units = [(slice(q * CHUNK, (q + 1) * CHUNK), slice(p * LANES, (p + 1) * LANES))
             for q in range(r_ref.shape[0] // CHUNK) for p in range(N_PAIRS)]
    cums = []
    for rows, cols in units:
        lw = jnp.log(dec_ref[rows, cols])
        hi = lw.astype(bf16)
        r1 = lw - hi.astype(f32)
        mid = r1.astype(bf16)
        lo = (r1 - mid.astype(f32)).astype(bf16)
        cums.append((lw, _dot(tri_ref[...], jnp.concatenate([hi, mid, lo], axis=0))))
    outs = []
    for (rows, cols), (lw, cum) in zip(units, cums):
        w_in = jnp.exp(cum)
        w_ex = jnp.exp(cum - lw)
        w_inv = jnp.exp(-cum)
        wc = w_in[last:last + 1, :]
        at = nkk_ref[rows, cols] * w_ex
        rt = r_ref[rows, cols] * w_in
        bt = kka_ref[rows, cols] * w_inv
        kt = k_ref[rows, cols] * w_inv
        at_ref[rows, cols] = at.astype(bf16)
        rt_ref[rows, cols] = rt.astype(bf16)
        bt_ref[rows, cols] = (bt * wc).astype(bf16)
        kt_ref[rows, cols] = (kt * wc).astype(bf16)
        wc_ref[rows, cols] = jnp.broadcast_to(wc, (CHUNK, LANES))
        lhs = jnp.concatenate([_halves(at, low), _halves(rt, low)], axis=0).astype(bf16)
        rhs = jnp.concatenate([bt, kt], axis=0).astype(bf16)
        outs.append(_dot_nt(lhs, rhs))
    pws, ts = [], []
    for (rows, cols), o in zip(units, outs):
        o0, o1, o2, o3 = (o[q * CHUNK:(q + 1) * CHUNK, :] for q in range(4))
        aab = jnp.where(strict, jnp.where(low, o0, swap(o1)), zero)
        aak_ref[rows, cols] = jnp.where(strict, jnp.where(low, swap(o0), o1), zero).astype(bf16)
        arb_ref[rows, cols] = jnp.where(incl, jnp.where(low, o2, swap(o3)), zero).astype(bf16)
        ark_ref[rows, cols] = jnp.where(incl, jnp.where(low, swap(o2), o3), zero).astype(bf16)
        pws.append(_halves(aab, low))
        ts.append(eye + pws[-1])
    for _ in range(int(math.log2(CHUNK)) - 1):
        pbs = [pw.astype(bf16) for pw in pws]
        pws = [_dot(pb, pb) for pb in pbs]
        ts = [t + _dot(t.astype(bf16), pw.astype(bf16)) for t, pw in zip(ts, pws)]
    for (rows, cols), t in zip(units, ts):
        tt_ref[rows, cols] = jnp.where(low, t[:CHUNK, :], t[CHUNK:, :]).astype(bf16)


CHUNKS_PER_STEP = 4


def _wkv_chunks(r, nkk, dec, kka, k, *, T, reverse):
    tri_i = jnp.arange(CHUNK)
    tri = (tri_i[None, :] >= tri_i[:, None]) if reverse else (tri_i[None, :] <= tri_i[:, None])
    tri3 = jnp.tile(tri.astype(bf16), (1, 3))
    rows = CHUNK * CHUNKS_PER_STEP
    assert T % rows == 0
    row = pl.BlockSpec((rows, C_R), lambda i: (i, 0))
    return pl.pallas_call(
        functools.partial(_chunks_kernel, reverse=reverse),
        grid=(T // rows,),
        in_specs=[row] * 5 + [pl.BlockSpec(tri3.shape, lambda i: (0, 0))],
        out_specs=[row] * 9,
        out_shape=[jax.ShapeDtypeStruct((T, C_R), bf16)] * 8 + [jax.ShapeDtypeStruct((T, C_R), f32)],
        compiler_params=pltpu.CompilerParams(dimension_semantics=("parallel",), vmem_limit_bytes=VMEM_LIMIT),
        name="wkv_chunks",
    )(r, nkk, dec, kka, k, tri3)


def _chunk_scan_kernel(vf_ref, vb_ref, *refs, nb):
    vs = (vf_ref, vb_ref)
    ins = (refs[0:9], refs[9:18])
    ys = refs[18:20]
    st_ref = refs[20]

    @pl.when(pl.program_id(1) == 0)
    def _():
        st_ref[...] = jnp.zeros_like(st_ref)

    low = lax.broadcasted_iota(jnp.int32, (CHUNK, LANES), 1) < HEAD_N
    rr = lax.broadcasted_iota(jnp.int32, (LANES, LANES), 0) < HEAD_N
    cc = lax.broadcasted_iota(jnp.int32, (LANES, LANES), 1) < HEAD_N
    same_head = rr == cc
    chains = [(bb, d, p) for bb in range(nb) for d in range(2) for p in range(N_PAIRS)]
    part = lambda bb, d, p, which: ins[d][which][bb, :, p * LANES:(p + 1) * LANES]
    s0s = [st_ref[(bb * 2 + d) * N_PAIRS + p] for bb, d, p in chains]
    vals = [vs[d][bb, :, p * LANES:(p + 1) * LANES] for bb, d, p in chains]
    vbds = [_halves(v, low).astype(bf16) for v in vals]
    gs = [_dot(jnp.concatenate([part(*ch, 0), part(*ch, 1)], axis=0), s0.astype(bf16))
          for ch, s0 in zip(chains, s0s)]
    xs = [g[:CHUNK, :] + _dot(part(*ch, 2), vbd) for ch, g, vbd in zip(chains, gs, vbds)]
    sas = [_dot(part(*ch, 3), _halves(x, low).astype(bf16)) for ch, x in zip(chains, xs)]
    for (bb, d, p), g, sa, vbd in zip(chains, gs, sas, vbds):
        ys[d][bb, :, p * LANES:(p + 1) * LANES] = (
            g[CHUNK:, :] + _dot(part(bb, d, p, 4), _halves(sa, low).astype(bf16)) + _dot(part(bb, d, p, 5), vbd))
    for (bb, d, p), s0, sa, v in zip(chains, s0s, sas, vals):
        upd = lax.dot_general(
            jnp.concatenate([part(bb, d, p, 6), part(bb, d, p, 7)], axis=0),
            jnp.concatenate([sa, v], axis=0).astype(bf16),
            (((0,), (0,)), ((), ())), preferred_element_type=f32)
        wc = part(bb, d, p, 8)
        wcol = jnp.concatenate([wc, wc], axis=0).T
        st_ref[(bb * 2 + d) * N_PAIRS + p] = jnp.where(same_head, wcol * s0 + upd, jnp.zeros_like(s0))


CHUNK_SCAN_BATCH = 4


def _chunk_scan(v, fwd_parts, bwd_parts, *, B, S):
    nC = S // CHUNK
    nb = CHUNK_SCAN_BATCH if B % CHUNK_SCAN_BATCH == 0 else 1
    as3 = lambda a: a.reshape(B, S, C_R)
    fwd = pl.BlockSpec((nb, CHUNK, C_R), lambda b, i: (b, i, 0))
    bwd = pl.BlockSpec((nb, CHUNK, C_R), lambda b, i: (b, nC - 1 - i, 0))
    yf, yb = pl.pallas_call(
        functools.partial(_chunk_scan_kernel, nb=nb),
        grid=(B // nb, nC),
        in_specs=[fwd, bwd] + [fwd] * 9 + [bwd] * 9,
        out_specs=[fwd, bwd],
        out_shape=[jax.ShapeDtypeStruct((B, S, C_R), f32)] * 2,
        scratch_shapes=[pltpu.VMEM((nb * 2 * N_PAIRS, LANES, LANES), f32)],
        compiler_params=pltpu.CompilerParams(
            dimension_semantics=("parallel", "arbitrary"), vmem_limit_bytes=VMEM_LIMIT),
        name="wkv_scan",
    )(as3(v), as3(v), *[as3(a) for a in fwd_parts], *[as3(a) for a in bwd_parts])
    return yf.reshape(B * S, C_R), yb.reshape(B * S, C_R)

def _mix_kernel(x_ref, attn_ref, yf_ref, yb_ref, r_ref, kf_ref, kb_ref, v_ref, g_ref,
                lng_ref, lnb_ref, rk_ref, aog_ref, wout_ref, jseg_ref, h_ref):
    jseg = jseg_ref[...]
    y = yf_ref[...] + yb_ref[...]
    mu = _split_dot(y, jseg) * (1.0 / HEAD_N)
    dlt = y - mu
    var = _split_dot(dlt * dlt, jseg) * (1.0 / HEAD_N)
    yn = dlt * lax.rsqrt(var + GN_EPS) * lng_ref[...] + lnb_ref[...]
    r = r_ref[...]
    k_mean = 0.5 * (kf_ref[...] + kb_ref[...])
    bonus = _split_dot(r * k_mean * rk_ref[...], jseg) * v_ref[...]
    rw = ((yn + bonus) * g_ref[...]).astype(bf16)
    attn = jnp.concatenate([attn_ref[0, p] for p in range(H_A // 2)], axis=-1)
    an = _rms(attn, aog_ref[...]).astype(bf16)
    ha = H_A * V_DIM
    h_ref[...] = x_ref[...] + _dot(an, wout_ref[0:ha, :]) + _dot(rw, wout_ref[ha:, :])


def _mix(x, attn, yf, yb, r, kf, kb, v, g, lng, lnb, rk, aog, wout, jseg, *, B, S, tm):
    nS = S // tm
    const = lambda shape: pl.BlockSpec(shape, lambda b, s: (0,) * len(shape))
    row = lambda c: pl.BlockSpec((tm, c), lambda b, s: (b * nS + s, 0))
    return pl.pallas_call(
        _mix_kernel,
        grid=(B, nS),
        in_specs=[row(D_MODEL), pl.BlockSpec((1, H_A // 2, tm, LANES), lambda b, s: (b, 0, s, 0))]
                 + [row(C_R)] * 7
                 + [const((1, C_R))] * 3 + [const((1, H_A * V_DIM)), const(wout.shape), const(jseg.shape)],
        out_specs=row(D_MODEL),
        out_shape=jax.ShapeDtypeStruct((B * S, D_MODEL), f32),
        compiler_params=pltpu.CompilerParams(
            dimension_semantics=("parallel", "parallel"), vmem_limit_bytes=VMEM_LIMIT),
        name="mix",
    )(x, attn, yf, yb, r, kf, kb, v, g, lng, lnb, rk, aog, wout, jseg)


def _take_max(s, ids):
    m = jnp.max(s, axis=0, keepdims=True)
    pick = jnp.min(jnp.where(s == m, ids, jnp.int32(2 ** 30)), axis=0, keepdims=True)
    return m, pick, jnp.where(ids == pick, -jnp.inf, s)


CAND_COLS = tuple(P_TOPK // (a + 1) for a in range(P_TOPK))
N_CAND = sum(CAND_COLS)
N_CAND_PAD = -(-N_CAND // 8) * 8


def _route_kernel(h_ref, g2_ref, wpq_ref, sk_ref, pos_ref, idx_ref, gate_ref, v1_ref, i1_ref, v2_ref, i2_ref, cs_ref,
                  ci_ref, bs_ref, bi_ref):
    tm = h_ref.shape[0]
    hn = _rms(h_ref[...], g2_ref[...]).astype(bf16)
    pq = _dot(hn, wpq_ref[...])
    key_ids = lax.broadcasted_iota(jnp.int32, (N_KEYS, tm), 0)
    pos_ids = pos_ref[...]
    for p in range(P_HEADS):
        for side, (vs_ref, is_ref) in enumerate(((v1_ref, i1_ref), (v2_ref, i2_ref))):
            qh = pq[:, p * D_KEY + side * HALF_KEY:p * D_KEY + (side + 1) * HALF_KEY].astype(bf16)
            s = _dot_nt(sk_ref[side], qh)
            for j in range(P_TOPK):
                m, pick, s = _take_max(s, key_ids)
                vs_ref[j:j + 1, :] = m
                is_ref[j:j + 1, :] = pick
        v2 = v2_ref[...]
        i2 = i2_ref[...]
        off = 0
        for a, nb in enumerate(CAND_COLS):
            cs_ref[off:off + nb, :] = v1_ref[a:a + 1, :] + v2[:nb, :]
            ci_ref[off:off + nb, :] = i1_ref[a:a + 1, :] * N_KEYS + i2[:nb, :]
            off += nb
        cs_ref[N_CAND:, :] = jnp.full((N_CAND_PAD - N_CAND, tm), -jnp.inf, f32)
        ci_ref[N_CAND:, :] = jnp.zeros((N_CAND_PAD - N_CAND, tm), jnp.int32)
        s = cs_ref[...]
        ci = ci_ref[...]
        for j in range(P_TOPK):
            m, pick, s = _take_max(s, pos_ids)
            bs_ref[j:j + 1, :] = m
            bi_ref[p * P_TOPK + j:p * P_TOPK + j + 1, :] = jnp.max(
                jnp.where(pos_ids == pick, ci, -1), axis=0, keepdims=True)
        bs = bs_ref[...]
        e = jnp.exp(bs - bs[0:1, :])
        gate_ref[:, p * P_TOPK:(p + 1) * P_TOPK] = (e / jnp.sum(e, axis=0, keepdims=True)).T
    idx_ref[...] = bi_ref[...].T


def _route(h, g2, wpq, sk, *, T, tm):
    const = lambda shape: pl.BlockSpec(shape, lambda i: (0,) * len(shape))
    pk = P_HEADS * P_TOPK
    flat = [a * P_TOPK + b for a, nb in enumerate(CAND_COLS) for b in range(nb)] + [2 ** 20] * (N_CAND_PAD - N_CAND)
    pos = jnp.broadcast_to(jnp.array(flat, jnp.int32)[:, None], (N_CAND_PAD, tm))
    return pl.pallas_call(
        _route_kernel,
        grid=(T // tm,),
        in_specs=[pl.BlockSpec((tm, D_MODEL), lambda i: (i, 0)), const((1, D_MODEL)), const(wpq.shape),
                  const(sk.shape), const((N_CAND_PAD, tm))],
        out_specs=[pl.BlockSpec((tm, pk), lambda i: (i, 0))] * 2,
        out_shape=[jax.ShapeDtypeStruct((T, pk), jnp.int32), jax.ShapeDtypeStruct((T, pk), f32)],
        scratch_shapes=[pltpu.VMEM((P_TOPK, tm), f32), pltpu.VMEM((P_TOPK, tm), jnp.int32),
                        pltpu.VMEM((P_TOPK, tm), f32), pltpu.VMEM((P_TOPK, tm), jnp.int32),
                        pltpu.VMEM((N_CAND_PAD, tm), f32), pltpu.VMEM((N_CAND_PAD, tm), jnp.int32),
                        pltpu.VMEM((P_TOPK, tm), f32), pltpu.VMEM((pk, tm), jnp.int32)],
        compiler_params=pltpu.CompilerParams(dimension_semantics=("parallel",), vmem_limit_bytes=VMEM_LIMIT),
        name="route",
    )(h, g2, wpq, sk, pos)


GATHER_WINDOW = 64
SC_CORES = 2
SC_SUBCORES = 16
SC_WORKERS = SC_CORES * SC_SUBCORES


def _gather_rows(table, idx):
    n = idx.shape[0]
    d = table.shape[1]
    win = GATHER_WINDOW
    n_it = n // (SC_WORKERS * win)
    assert n_it * SC_WORKERS * win == n and n_it % 2 == 0, (n, n_it)
    mesh = plsc.VectorSubcoreMesh(core_axis_name="core", subcore_axis_name="subcore")

    @functools.partial(
        pl.kernel, out_type=jax.ShapeDtypeStruct((n, d), table.dtype), mesh=mesh,
        scratch_types=[pltpu.VMEM((n_it, win), jnp.int32), pltpu.VMEM((2, win, d), table.dtype),
                       pltpu.SemaphoreType.DMA, pltpu.SemaphoreType.DMA,
                       pltpu.SemaphoreType.DMA, pltpu.SemaphoreType.DMA])
    def gather(tab_hbm, idx_hbm, out_hbm, idx_v, rows_v, gsem0, gsem1, wsem0, wsem1):
        wid = lax.axis_index("subcore") * SC_CORES + lax.axis_index("core")
        base = wid * (n_it * win)
        gsem = (gsem0, gsem1)
        wsem = (wsem0, wsem1)
        pltpu.sync_copy(idx_hbm.at[wid], idx_v)

        def fetch(j, b):
            return pltpu.make_async_copy(tab_hbm.at[idx_v.at[j]], rows_v.at[b], gsem[b])

        def flush(j, b):
            return pltpu.make_async_copy(rows_v.at[b], out_hbm.at[pl.ds(base + j * win, win)], wsem[b])

        fetch(0, 0).start()

        @pl.loop(0, n_it, step=2)
        def _(j0):
            for b in range(2):
                j = j0 + b

                @pl.when(j >= 1)
                def _():
                    flush(j - 1, 1 - b).wait()

                @pl.when(j + 1 < n_it)
                def _():
                    fetch(j + 1, 1 - b).start()

                fetch(j, b).wait()
                flush(j, b).start()

        flush(n_it - 1, 1).wait()

    return gather(table, idx.reshape(SC_WORKERS, n_it, win))


HALF_D = D_MODEL // 2


def _pack_rows(table):
    bits = lax.bitcast_convert_type(table.astype(bf16), jnp.uint16).astype(jnp.uint32)
    return bits[:, :HALF_D] | (bits[:, HALF_D:] << 16)


def _unpack_rows(words):
    lo = lax.bitcast_convert_type(words << 16, f32).astype(bf16)
    hi = lax.bitcast_convert_type(words & jnp.uint32(0xFFFF0000), f32).astype(bf16)
    return lo, hi


def _expert_kernel(h_ref, g2_ref, gate_ref, gu_ref, gv_ref, o_ref):
    h = h_ref[...]
    tb = h.shape[0]
    pk = P_HEADS * P_TOPK
    hn = _rms(h, g2_ref[...]).astype(bf16)
    row = lax.broadcasted_iota(jnp.int32, (8, pk), 0)
    for grp in range(tb // 8):
        rows = slice(grp * 8, (grp + 1) * 8)
        hn8 = hn[rows, :]
        act = jnp.zeros((8, pk), f32)
        for j in range(8):
            t = grp * 8 + j
            ulo, uhi = _unpack_rows(gu_ref[t * pk:(t + 1) * pk, :])
            act = jnp.where(row == j, _dot_nt(hn8[:, :HALF_D], ulo) + _dot_nt(hn8[:, HALF_D:], uhi), act)
        gelu = 0.5 * act * (1.0 + lax.erf(act * (2.0 ** -0.5)))
        w = (gate_ref[rows, :] * gelu).astype(bf16)
        out_lo = jnp.zeros((8, HALF_D), f32)
        out_hi = jnp.zeros((8, HALF_D), f32)
        for j in range(8):
            t = grp * 8 + j
            vlo, vhi = _unpack_rows(gv_ref[t * pk:(t + 1) * pk, :])
            wj = jnp.where(row == j, w, jnp.zeros_like(w))
            out_lo = out_lo + _dot(wj, vlo)
            out_hi = out_hi + _dot(wj, vhi)
        o_ref[rows, :HALF_D] = h[rows, :HALF_D] + out_lo
        o_ref[rows, HALF_D:] = h[rows, HALF_D:] + out_hi


def _experts(h, g2, gate, gu, gv, *, tb):
    n = h.shape[0]
    pk = P_HEADS * P_TOPK
    const = lambda shape: pl.BlockSpec(shape, lambda i: (0,) * len(shape))
    return pl.pallas_call(
        _expert_kernel,
        grid=(n // tb,),
        in_specs=[pl.BlockSpec((tb, D_MODEL), lambda i: (i, 0)), const((1, D_MODEL)),
                  pl.BlockSpec((tb, pk), lambda i: (i, 0)),
                  pl.BlockSpec((tb * pk, HALF_D), lambda i: (i, 0)),
                  pl.BlockSpec((tb * pk, HALF_D), lambda i: (i, 0))],
        out_specs=pl.BlockSpec((tb, D_MODEL), lambda i: (i, 0)),
        out_shape=jax.ShapeDtypeStruct((n, D_MODEL), f32),
        compiler_params=pltpu.CompilerParams(dimension_semantics=("parallel",), vmem_limit_bytes=VMEM_LIMIT),
        name="experts",
    )(h, g2, gate, gu, gv)


def _place(cols, width, offset):
    return jnp.pad(cols, ((0, 0), (offset, width - offset - cols.shape[1])))


def _rope_partner():
    half = ROPE_DIM // 2
    return jnp.concatenate([jnp.arange(half, ROPE_DIM), jnp.arange(0, half)])


def _prepare(norm1_g, w_in, q_lat_g, w_uq, kv_lat_g, w_ukv, q_norm_g, k_norm_g, attn_out_g, mu_prev, mu_next,
             w0, w_up, a0, a_up, g_up, k_k, k_a, r_k, ln_x_g, ln_x_b, w_out, norm2_g, w_pq, sub_keys, S):
    partner = _rope_partner()
    wz = w_in[:, OFF_RWKV:]
    wkr = w_in[:, OFF_KR:OFF_RWKV]
    win = jnp.concatenate([
        w_in[:, OFF_Q:OFF_KV], w_in[:, OFF_KV:OFF_KR], wz,
        _place(wkr, LANES, NOPE_DIM), _place(wkr[:, partner], LANES, NOPE_DIM)], axis=1).astype(bf16)

    wq = w_uq.reshape(Q_LORA, H_A, QK_DIM)
    wqa = jnp.pad(wq, ((0, 0), (0, 0), (0, HEAD_PAD - QK_DIM))).reshape(Q_LORA, H_A * HEAD_PAD).astype(bf16)
    wqb = jnp.pad(wq[:, :, NOPE_DIM:][:, :, partner],
                  ((0, 0), (0, 0), (NOPE_DIM, HEAD_PAD - QK_DIM))).reshape(Q_LORA, H_A * HEAD_PAD).astype(bf16)
    wkv = w_ukv.reshape(KV_LORA, H_A, NOPE_DIM + V_DIM)
    wkn = jnp.pad(wkv[:, :, :NOPE_DIM], ((0, 0), (0, 0), (0, HEAD_PAD - NOPE_DIM))).reshape(
        KV_LORA, H_A * HEAD_PAD).astype(bf16)
    wv = jnp.pad(wkv[:, :, NOPE_DIM:], ((0, 0), (0, 0), (0, LANES - V_DIM))).reshape(
        KV_LORA, H_A * LANES).astype(bf16)

    def gain_rows(g):
        ga = _place(g[None, :], LANES, 0)
        gb = _place(g[None, NOPE_DIM:][:, partner], LANES, NOPE_DIM)
        return ga, gb

    gqa, gqb = gain_rows(q_norm_g)
    gq = jnp.concatenate([gqa, gqb, jnp.zeros((6, LANES), f32)], axis=0)
    gkn = _place(k_norm_g[None, :NOPE_DIM], LANES, 0)
    gka = _place(k_norm_g[None, NOPE_DIM:], LANES, NOPE_DIM)
    gkb = _place(k_norm_g[None, NOPE_DIM:][:, partner], LANES, NOPE_DIM)
    gk = jnp.concatenate([gkn, gka, gkb, jnp.zeros((5, LANES), f32)], axis=0)

    half = ROPE_DIM // 2
    inv = 1.0 / (ROPE_THETA ** (jnp.arange(half, dtype=f32) / half))
    ang = jnp.arange(S, dtype=f32)[:, None] * inv[None, :]
    c, s = jnp.cos(ang), jnp.sin(ang)
    cos = jnp.concatenate([jnp.ones((S, NOPE_DIM), f32), c, c, jnp.zeros((S, HEAD_PAD - QK_DIM), f32)], axis=1)
    sin = jnp.concatenate([jnp.zeros((S, NOPE_DIM), f32), -s, s, jnp.zeros((S, HEAD_PAD - QK_DIM), f32)], axis=1)

    zeros = jnp.zeros((W_LORA, C_R), f32)
    wl = jnp.stack([jnp.concatenate([jnp.concatenate([w_up[d], zeros], axis=1),
                                     jnp.concatenate([zeros, a_up[d]], axis=1)], axis=0) for d in range(2)]).astype(bf16)
    seg = jnp.arange(C_R) // HEAD_N
    jseg = (seg[:, None] == seg[None, :]).astype(bf16)

    return dict(
        g1=norm1_g[None, :], win=win, qlg=q_lat_g[None, :], wqa=wqa, wqb=wqb, kvlg=kv_lat_g[None, :], wkn=wkn, wv=wv,
        cos=cos, sin=sin, gq=gq, gk=gk,
        mup=mu_prev[None, :], mun=mu_next[None, :], wl=wl, w0=w0, a0=a0,
        gup=g_up.astype(bf16), kk=k_k[None, :], ka=k_a[None, :], jseg=jseg,
        lng=ln_x_g[None, :], lnb=ln_x_b[None, :], rk=r_k[None, :], aog=attn_out_g[None, :],
        wout=w_out.astype(bf16), g2=norm2_g[None, :], wpq=w_pq.astype(bf16), sk=sub_keys.astype(bf16),
    )


def _tile(n, pref):
    t = min(n, pref)
    assert n % t == 0, (n, t)
    return t


PEER_CHUNK = 4096
EXPERT_TOKENS = 32


def _dense(x, w):
    B, S, _ = x.shape
    T = B * S
    xt = x.reshape(T, D_MODEL)
    tm = _tile(S, 256)
    q, k, v, z = _inproj(xt, w["g1"], w["win"], w["qlg"], w["wqa"], w["wqb"], w["kvlg"], w["wkn"], w["wv"],
                         w["cos"], w["sin"], w["gq"], w["gk"], B=B, S=S, tm=tm)
    attn = _attention(q, k, v, B=B, S=S, tq=_tile(S, 512), tk=_tile(S // 2, 512))
    r, vr, nkk, g, decf, kkaf, kf, decb, kkab, kb = _rwkv_prep(
        z, w["mup"], w["mun"], w["wl"], w["w0"], w["a0"], w["gup"], w["kk"], w["ka"], w["jseg"], B=B, S=S, tm=tm)
    fw = _wkv_chunks(r, nkk, decf, kkaf, kf, T=T, reverse=False)
    bw = _wkv_chunks(r, nkk, decb, kkab, kb, T=T, reverse=True)
    yf, yb = _chunk_scan(vr, fw, bw, B=B, S=S)
    h = _mix(xt, attn, yf, yb, r, kf, kb, vr, g, w["lng"], w["lnb"], w["rk"], w["aog"], w["wout"], w["jseg"],
             B=B, S=S, tm=tm)
    idx, gate = _route(h, w["g2"], w["wpq"], w["sk"], T=T, tm=_tile(T, 256))
    return h, idx, gate


def _peer(h, idx, gate, w, pu, pv, shape):
    T = h.shape[0]
    pk = P_HEADS * P_TOPK
    tc = _tile(T, PEER_CHUNK)
    outs = []
    for c in range(T // tc):
        sl = slice(c * tc, (c + 1) * tc)
        flat = idx[sl].reshape(tc * pk)
        gu = _gather_rows(pu, flat)
        gv = _gather_rows(pv, flat)
        outs.append(_experts(h[sl], w["g2"], gate[sl], gu, gv, tb=_tile(tc, EXPERT_TOKENS)))
    return jnp.concatenate(outs, axis=0).reshape(shape)


def _layer(xs, w, expert_u, expert_v):
    pu = _pack_rows(expert_u)
    pv = _pack_rows(expert_v)
    outs = []
    pending = None
    for x in xs:
        cur = _dense(x, w) + (x.shape,)
        if pending is not None:
            outs.append(_peer(*pending[:3], w, pu, pv, pending[3]))
        pending = cur
    outs.append(_peer(*pending[:3], w, pu, pv, pending[3]))
    return outs


GROUP_BATCH = 1


def kernel(x_prompt, x_sample, norm1_g, w_in, q_lat_g, w_uq, kv_lat_g, w_ukv, q_norm_g, k_norm_g, attn_out_g, mu_prev, mu_next, w0, w_up, a0, a_up, g_up, k_k, k_a, r_k, ln_x_g, ln_x_b, w_out, norm2_g, w_pq, sub_keys, expert_u, expert_v):
    assert x_prompt.shape[1] == x_sample.shape[1]
    S = x_prompt.shape[1]
    groups = []
    for x in (x_prompt, x_sample):
        gb = GROUP_BATCH if x.shape[0] % GROUP_BATCH == 0 else x.shape[0]
        groups.append([x[i:i + gb] for i in range(0, x.shape[0], gb)])
    xs = groups[0] + groups[1]
    for l in range(norm1_g.shape[0]):
        w = _prepare(norm1_g[l], w_in[l], q_lat_g[l], w_uq[l], kv_lat_g[l], w_ukv[l], q_norm_g[l], k_norm_g[l],
                     attn_out_g[l], mu_prev[l], mu_next[l], w0[l], w_up[l], a0[l], a_up[l], g_up[l], k_k[l], k_a[l],
                     r_k[l], ln_x_g[l], ln_x_b[l], w_out[l], norm2_g[l], w_pq[l], sub_keys[l], S)
        xs = _layer(xs, w, expert_u[l], expert_v[l])
    n0 = len(groups[0])
    join = lambda parts: parts[0] if len(parts) == 1 else jnp.concatenate(parts, axis=0)
    return join(xs[:n0]), join(xs[n0:])
```

```python
import functools
import math

import jax
import jax.numpy as jnp
from jax import lax
from jax.experimental import pallas as pl
from jax.experimental.pallas import tpu as pltpu
from jax.experimental.pallas import tpu_sc as plsc

D_MODEL = 1024
H_A = 8
NOPE_DIM = 64
ROPE_DIM = 32
QK_DIM = NOPE_DIM + ROPE_DIM
V_DIM = 64
Q_LORA = 384
KV_LORA = 256
ROPE_THETA = 10000.0
H_R = 8
HEAD_N = 64
C_R = H_R * HEAD_N
W_LORA = 64
A_LORA = 64
G_LORA = 128
GN_EPS = 64e-5
RWKV_IN = 3 * C_R + W_LORA + A_LORA + G_LORA
OFF_Q = 0
OFF_KV = OFF_Q + Q_LORA
OFF_KR = OFF_KV + KV_LORA
OFF_RWKV = OFF_KR + ROPE_DIM
N_KEYS = 128
N_EXPERTS = N_KEYS * N_KEYS
P_HEADS = 8
P_TOPK = 16
D_KEY = 256
HALF_KEY = D_KEY // 2
NORM_EPS = 1e-6

LANES = 128
HEAD_PAD = LANES
Z_COLS = RWKV_IN
IN_PAD = Q_LORA + KV_LORA + Z_COLS + 2 * LANES
VMEM_LIMIT = 56 * 1024 * 1024

f32 = jnp.float32
bf16 = jnp.bfloat16


def _rms(x, g):
    return x * lax.rsqrt(jnp.mean(x * x, axis=-1, keepdims=True) + NORM_EPS) * g


def _dot(a, b):
    return jnp.dot(a, b, preferred_element_type=f32)


def _dot_nt(a, b):
    return lax.dot_general(a, b, (((1,), (1,)), ((), ())), preferred_element_type=f32)


def _split_dot(x, j):
    hi = x.astype(bf16)
    lo = (x - hi.astype(f32)).astype(bf16)
    return _dot(hi, j) + _dot(lo, j)


def _inproj_kernel(x_ref, g1_ref, win_ref, qlg_ref, wqa_ref, wqb_ref, kvlg_ref, wkn_ref, wv_ref,
                   cos_ref, sin_ref, gq_ref, gk_ref,
                   q_ref, k_ref, v_ref, z_ref):
    x = x_ref[...]
    xn = _rms(x, g1_ref[...]).astype(bf16)
    proj = _dot(xn, win_ref[...])
    o_kv, o_z, o_ra, o_rb = Q_LORA, Q_LORA + KV_LORA, Q_LORA + KV_LORA + Z_COLS, Q_LORA + KV_LORA + Z_COLS + LANES
    z_ref[...] = proj[:, o_z:o_ra]
    cos = cos_ref[...]
    sin = sin_ref[...]

    ql = _rms(proj[:, :o_kv], qlg_ref[...]).astype(bf16)
    qa = _dot(ql, wqa_ref[...])
    qb = _dot(ql, wqb_ref[...])
    ga, gb = gq_ref[0:1, :], gq_ref[1:2, :]
    scale = QK_DIM ** -0.5 * math.log2(math.e)
    for h in range(H_A):
        a = qa[:, h * LANES:(h + 1) * LANES]
        b = qb[:, h * LANES:(h + 1) * LANES]
        s = lax.rsqrt(jnp.sum(a * a, axis=-1, keepdims=True) * (1.0 / QK_DIM) + NORM_EPS) * scale
        q_ref[0, h] = (s * (a * ga * cos + b * gb * sin)).astype(bf16)

    kvl = _rms(proj[:, o_kv:o_z], kvlg_ref[...]).astype(bf16)
    kn = _dot(kvl, wkn_ref[...])
    vv = _dot(kvl, wv_ref[...])
    one = (lax.broadcasted_iota(jnp.int32, (1, LANES), 1) == V_DIM).astype(f32)
    for h in range(H_A):
        v_ref[0, h] = (vv[:, h * LANES:(h + 1) * LANES] + one).astype(bf16)
    kra = proj[:, o_ra:o_rb]
    krb = proj[:, o_rb:o_rb + LANES]
    gkn, gka, gkb = gk_ref[0:1, :], gk_ref[1:2, :], gk_ref[2:3, :]
    kr = kra * gka * cos + krb * gkb * sin
    ssr = jnp.sum(kra * kra, axis=-1, keepdims=True)
    for h in range(H_A):
        a = kn[:, h * LANES:(h + 1) * LANES]
        s = lax.rsqrt((jnp.sum(a * a, axis=-1, keepdims=True) + ssr) * (1.0 / QK_DIM) + NORM_EPS)
        k_ref[0, h] = (s * (a * gkn + kr)).astype(bf16)


def _inproj(x, g1, win, qlg, wqa, wqb, kvlg, wkn, wv, cos, sin, gq, gk, *, B, S, tm):
    nS = S // tm
    const = lambda shape: pl.BlockSpec(shape, lambda b, s: (0,) * len(shape))
    return pl.pallas_call(
        _inproj_kernel,
        grid=(B, nS),
        in_specs=[
            pl.BlockSpec((tm, D_MODEL), lambda b, s: (b * nS + s, 0)),
            const((1, D_MODEL)), const(win.shape), const((1, Q_LORA)), const(wqa.shape), const(wqb.shape),
            const((1, KV_LORA)), const(wkn.shape), const(wv.shape),
            pl.BlockSpec((tm, LANES), lambda b, s: (s, 0)),
            pl.BlockSpec((tm, LANES), lambda b, s: (s, 0)),
            const(gq.shape), const(gk.shape),
        ],
        out_specs=[
            pl.BlockSpec((1, H_A, tm, HEAD_PAD), lambda b, s: (b, 0, s, 0)),
            pl.BlockSpec((1, H_A, tm, HEAD_PAD), lambda b, s: (b, 0, s, 0)),
            pl.BlockSpec((1, H_A, tm, LANES), lambda b, s: (b, 0, s, 0)),
            pl.BlockSpec((tm, Z_COLS), lambda b, s: (b * nS + s, 0)),
        ],
        out_shape=[
            jax.ShapeDtypeStruct((B, H_A, S, HEAD_PAD), bf16),
            jax.ShapeDtypeStruct((B, H_A, S, HEAD_PAD), bf16),
            jax.ShapeDtypeStruct((B, H_A, S, LANES), bf16),
            jax.ShapeDtypeStruct((B * S, Z_COLS), f32),
        ],
        compiler_params=pltpu.CompilerParams(
            dimension_semantics=("parallel", "parallel"), vmem_limit_bytes=VMEM_LIMIT),
        name="in_proj",
    )(x, g1, win, qlg, wqa, wqb, kvlg, wkn, wv, cos, sin, gq, gk)


ATTN_ROWS = 32
ATTN_UNROLL = 16


def _attn_kernel(q_ref, k_ref, v_ref, o_ref, s_ref, p_ref, m_ref, al_ref, acc_ref, *, tk, nk):
    tq = q_ref.shape[2]
    nlt = tk // LANES

    def scores(h, q, j, buf):
        off = pl.multiple_of(jnp.minimum(j, nk - 1) * tk, tk)
        s_ref[buf] = _dot_nt(q, k_ref[0, h, pl.ds(off, tk), :])

    def softmax(buf):
        def chunk(c, carry):
            rows = pl.ds(pl.multiple_of(c * ATTN_ROWS, ATTN_ROWS), ATTN_ROWS)
            tiles = [s_ref[buf, rows, lt * LANES:(lt + 1) * LANES] for lt in range(nlt)]
            best = functools.reduce(jnp.maximum, tiles)
            m_old = m_ref[rows, :]
            m_new = jnp.maximum(m_old, jnp.broadcast_to(jnp.max(best, axis=-1, keepdims=True), m_old.shape))
            for lt in range(nlt):
                p_ref[buf, rows, lt * LANES:(lt + 1) * LANES] = jnp.exp2(tiles[lt] - m_new).astype(bf16)
            al_ref[rows, :] = jnp.exp2(m_old - m_new)
            m_ref[rows, :] = m_new
            return carry

        lax.fori_loop(0, tq // ATTN_ROWS, chunk, 0, unroll=ATTN_UNROLL)

    def accumulate(h, j, buf):
        off = pl.multiple_of(j * tk, tk)
        acc_ref[...] = al_ref[...] * acc_ref[...] + _dot(p_ref[buf], v_ref[0, h, pl.ds(off, tk), :])

    outs = []
    for h in range(2):
        q = q_ref[0, h]
        m_ref[...] = jnp.full(m_ref.shape, -jnp.inf, f32)
        acc_ref[...] = jnp.zeros(acc_ref.shape, f32)
        scores(h, q, 0, 0)

        def body(jj, carry, h=h, q=q):
            j = 2 * jj
            scores(h, q, j + 1, 1)
            softmax(0)
            accumulate(h, j, 0)
            scores(h, q, j + 2, 0)
            softmax(1)
            accumulate(h, j + 1, 1)
            return carry

        lax.fori_loop(0, nk // 2, body, 0)
        acc = acc_ref[...]
        outs.append(acc / acc[:, V_DIM:V_DIM + 1])
    lane = lax.broadcasted_iota(jnp.int32, (tq, LANES), 1)
    o_ref[0, 0] = jnp.where(lane < V_DIM, outs[0], pltpu.roll(outs[1], V_DIM, 1))


def _attention(q, k, v, *, B, S, tq, tk):
    return pl.pallas_call(
        functools.partial(_attn_kernel, tk=tk, nk=S // tk),
        grid=(B, H_A // 2, S // tq),
        in_specs=[
            pl.BlockSpec((1, 2, tq, HEAD_PAD), lambda b, p, i: (b, p, i, 0)),
            pl.BlockSpec((1, 2, S, HEAD_PAD), lambda b, p, i: (b, p, 0, 0)),
            pl.BlockSpec((1, 2, S, LANES), lambda b, p, i: (b, p, 0, 0)),
        ],
        out_specs=pl.BlockSpec((1, 1, tq, LANES), lambda b, p, i: (b, p, i, 0)),
        out_shape=jax.ShapeDtypeStruct((B, H_A // 2, S, LANES), f32),
        scratch_shapes=[pltpu.VMEM((2, tq, tk), f32), pltpu.VMEM((2, tq, tk), bf16), pltpu.VMEM((tq, LANES), f32),
                        pltpu.VMEM((tq, LANES), f32), pltpu.VMEM((tq, LANES), f32)],
        compiler_params=pltpu.CompilerParams(
            dimension_semantics=("parallel", "parallel", "parallel"), vmem_limit_bytes=VMEM_LIMIT),
        name="attention",
    )(q, k, v)


def _softplus(x):
    return jnp.maximum(x, 0.0) + jnp.log1p(jnp.exp(-jnp.abs(x)))


def _prep_kernel(z_ref, zp_ref, zn_ref, mup_ref, mun_ref, wl_ref, w0_ref, a0_ref, gup_ref, kk_ref, ka_ref, jseg_ref,
                 r_ref, v_ref, nkk_ref, g_ref,
                 decf_ref, kkaf_ref, kf_ref, decb_ref, kkab_ref, kb_ref):
    si = pl.program_id(1)
    ns = pl.num_programs(1)
    z = z_ref[...]
    tm = z.shape[0]
    row = lax.broadcasted_iota(jnp.int32, (tm, 1), 0)
    zp_row = jnp.where(si == 0, 0.0, zp_ref[7:8, :])
    zn_row = jnp.where(si == ns - 1, 0.0, zn_ref[0:1, :])
    z_prev = jnp.where(row == 0, zp_row, pltpu.roll(z, 1, 0))
    z_next = jnp.where(row == tm - 1, zn_row, pltpu.roll(z, tm - 1, 0))
    zm = z + mup_ref[...] * (z_prev - z) + mun_ref[...] * (z_next - z)

    o1, o2, o3 = C_R, 2 * C_R, 3 * C_R
    o4 = o3 + W_LORA + A_LORA
    o5 = o4 + G_LORA
    r = zm[:, :o1]
    kr = zm[:, o1:o2]
    vr = zm[:, o2:o3]
    lw = zm[:, o3:o4]
    zg = zm[:, o4:o5]
    r_ref[...] = r
    v_ref[...] = vr
    g_ref[...] = _dot(jax.nn.sigmoid(zg).astype(bf16), gup_ref[...])

    kk = kr * kk_ref[...]
    ssk = _split_dot(kk * kk, jseg_ref[...])
    kk = kk / jnp.maximum(jnp.sqrt(ssk), 1e-12)
    nkk_ref[...] = -kk

    lane = lax.broadcasted_iota(jnp.int32, lw.shape, 1)
    lin = jnp.where(lane < W_LORA, jnp.tanh(lw), lw).astype(bf16)
    ka = ka_ref[...]
    for d, (dec_ref, kka_ref, kd_ref) in enumerate(((decf_ref, kkaf_ref, kf_ref), (decb_ref, kkab_ref, kb_ref))):
        lo = _dot(lin, wl_ref[d])
        w = -_softplus(-(w0_ref[d:d + 1, :] + lo[:, :C_R])) - 0.5
        dec_ref[...] = jnp.exp(-jnp.exp(w))
        a = jax.nn.sigmoid(a0_ref[d:d + 1, :] + lo[:, C_R:])
        kd_ref[...] = kr * (1.0 + (a - 1.0) * ka)
        kka_ref[...] = kk * a


def _rwkv_prep(z, mup, mun, wl, w0, a0, gup, kk, ka, jseg, *, B, S, tm):
    nS = S // tm
    T = B * S
    nb8 = tm // 8
    const = lambda shape: pl.BlockSpec(shape, lambda b, s: (0,) * len(shape))
    row_spec = pl.BlockSpec((tm, C_R), lambda b, s: (b * nS + s, 0))
    return pl.pallas_call(
        _prep_kernel,
        grid=(B, nS),
        in_specs=[
            pl.BlockSpec((tm, Z_COLS), lambda b, s: (b * nS + s, 0)),
            pl.BlockSpec((8, Z_COLS), lambda b, s: (jnp.maximum((b * nS + s) * nb8 - 1, 0), 0)),
            pl.BlockSpec((8, Z_COLS), lambda b, s: (jnp.minimum((b * nS + s + 1) * nb8, T // 8 - 1), 0)),
            const((1, Z_COLS)), const((1, Z_COLS)), const(wl.shape), const(w0.shape), const(a0.shape),
            const(gup.shape), const((1, C_R)), const((1, C_R)), const(jseg.shape),
        ],
        out_specs=[row_spec] * 10,
        out_shape=[jax.ShapeDtypeStruct((T, C_R), f32)] * 10,
        compiler_params=pltpu.CompilerParams(
            dimension_semantics=("parallel", "parallel"), vmem_limit_bytes=VMEM_LIMIT),
        name="rwkv_prep",
    )(z, z, z, mup, mun, wl, w0, a0, gup, kk, ka, jseg)


N_PAIRS = H_R // 2
CHUNK = 64


def _halves(x, low):
    zero = jnp.zeros_like(x)
    return jnp.concatenate([jnp.where(low, x, zero), jnp.where(low, zero, x)], axis=0)


def _chunks_kernel(r_ref, nkk_ref, dec_ref, kka_ref, k_ref, tri_ref,
                   at_ref, rt_ref, aak_ref, tt_ref, arb_ref, ark_ref, bt_ref, kt_ref, wc_ref, *, reverse):
    lane = lax.broadcasted_iota(jnp.int32, (CHUNK, LANES), 1)
    low = lane < HEAD_N
    jj = lax.broadcasted_iota(jnp.int32, (CHUNK, LANES), 0)
    ii = lane & (HEAD_N - 1)
    strict = (ii > jj) if reverse else (ii < jj)
    incl = (ii >= jj) if reverse else (ii <= jj)
    rr = lax.broadcasted_iota(jnp.int32, (LANES, LANES), 0)
    cc = lax.broadcasted_iota(jnp.int32, (LANES, LANES), 1)
    eye = (rr == cc).astype(f32)
    last = 0 if reverse else CHUNK - 1
    swap = lambda x: pltpu.roll(x, HEAD_N, 1)
    zero = jnp.zeros((CHUNK, LANES), f32)
    units = [(slice(q * CHUNK, (q + 1) * CHUNK), slice(p * LANES, (p + 1) * LANES))
             for q in range(r_ref.shape[0] // CHUNK) for p in range(N_PAIRS)]
    cums = []
    for rows, cols in units:
        lw = jnp.log(dec_ref[rows, cols])
        hi = lw.astype(bf16)
        r1 = lw - hi.astype(f32)
        mid = r1.astype(bf16)
        lo = (r1 - mid.astype(f32)).astype(bf16)
        cums.append((lw, _dot(tri_ref[...], jnp.concatenate([hi, mid, lo], axis=0))))
    outs = []
    for (rows, cols), (lw, cum) in zip(units, cums):
        w_in = jnp.exp(cum)
        w_ex = jnp.exp(cum - lw)
        w_inv = jnp.exp(-cum)
        wc = w_in[last:last + 1, :]
        at = nkk_ref[rows, cols] * w_ex
        rt = r_ref[rows, cols] * w_in
        bt = kka_ref[rows, cols] * w_inv
        kt = k_ref[rows, cols] * w_inv
        at_ref[rows, cols] = at.astype(bf16)
        rt_ref[rows, cols] = rt.astype(bf16)
        bt_ref[rows, cols] = (bt * wc).astype(bf16)
        kt_ref[rows, cols] = (kt * wc).astype(bf16)
        wc_ref[rows, cols] = jnp.broadcast_to(wc, (CHUNK, LANES))
        lhs = jnp.concatenate([_halves(at, low), _halves(rt, low)], axis=0).astype(bf16)
        rhs = jnp.concatenate([bt, kt], axis=0).astype(bf16)
        outs.append(_dot_nt(lhs, rhs))
    pws, ts = [], []
    for (rows, cols), o in zip(units, outs):
        o0, o1, o2, o3 = (o[q * CHUNK:(q + 1) * CHUNK, :] for q in range(4))
        aab = jnp.where(strict, jnp.where(low, o0, swap(o1)), zero)
        aak_ref[rows, cols] = jnp.where(strict, jnp.where(low, swap(o0), o1), zero).astype(bf16)
        arb_ref[rows, cols] = jnp.where(incl, jnp.where(low, o2, swap(o3)), zero).astype(bf16)
        ark_ref[rows, cols] = jnp.where(incl, jnp.where(low, swap(o2), o3), zero).astype(bf16)
        pws.append(_halves(aab, low))
        ts.append(eye + pws[-1])
    for _ in range(int(math.log2(CHUNK)) - 1):
        pbs = [pw.astype(bf16) for pw in pws]
        pws = [_dot(pb, pb) for pb in pbs]
        ts = [t + _dot(t.astype(bf16), pw.astype(bf16)) for t, pw in zip(ts, pws)]
    for (rows, cols), t in zip(units, ts):
        tt_ref[rows, cols] = jnp.where(low, t[:CHUNK, :], t[CHUNK:, :]).astype(bf16)


CHUNKS_PER_STEP = 4


def _wkv_chunks(r, nkk, dec, kka, k, *, T, reverse):
    tri_i = jnp.arange(CHUNK)
    tri = (tri_i[None, :] >= tri_i[:, None]) if reverse else (tri_i[None, :] <= tri_i[:, None])
    tri3 = jnp.tile(tri.astype(bf16), (1, 3))
    rows = CHUNK * CHUNKS_PER_STEP
    assert T % rows == 0
    row = pl.BlockSpec((rows, C_R), lambda i: (i, 0))
    return pl.pallas_call(
        functools.partial(_chunks_kernel, reverse=reverse),
        grid=(T // rows,),
        in_specs=[row] * 5 + [pl.BlockSpec(tri3.shape, lambda i: (0, 0))],
        out_specs=[row] * 9,
        out_shape=[jax.ShapeDtypeStruct((T, C_R), bf16)] * 8 + [jax.ShapeDtypeStruct((T, C_R), f32)],
        compiler_params=pltpu.CompilerParams(dimension_semantics=("parallel",), vmem_limit_bytes=VMEM_LIMIT),
        name="wkv_chunks",
    )(r, nkk, dec, kka, k, tri3)


def _chunk_scan_kernel(vf_ref, vb_ref, *refs, nb):
    vs = (vf_ref, vb_ref)
    ins = (refs[0:9], refs[9:18])
    ys = refs[18:20]
    st_ref = refs[20]

    @pl.when(pl.program_id(1) == 0)
    def _():
        st_ref[...] = jnp.zeros_like(st_ref)

    low = lax.broadcasted_iota(jnp.int32, (CHUNK, LANES), 1) < HEAD_N
    rr = lax.broadcasted_iota(jnp.int32, (LANES, LANES), 0) < HEAD_N
    cc = lax.broadcasted_iota(jnp.int32, (LANES, LANES), 1) < HEAD_N
    same_head = rr == cc
    chains = [(bb, d, p) for bb in range(nb) for d in range(2) for p in range(N_PAIRS)]
    part = lambda bb, d, p, which: ins[d][which][bb, :, p * LANES:(p + 1) * LANES]
    s0s = [st_ref[(bb * 2 + d) * N_PAIRS + p] for bb, d, p in chains]
    vals = [vs[d][bb, :, p * LANES:(p + 1) * LANES] for bb, d, p in chains]
    vbds = [_halves(v, low).astype(bf16) for v in vals]
    gs = [_dot(jnp.concatenate([part(*ch, 0), part(*ch, 1)], axis=0), s0.astype(bf16))
          for ch, s0 in zip(chains, s0s)]
    xs = [g[:CHUNK, :] + _dot(part(*ch, 2), vbd) for ch, g, vbd in zip(chains, gs, vbds)]
    sas = [_dot(part(*ch, 3), _halves(x, low).astype(bf16)) for ch, x in zip(chains, xs)]
    for (bb, d, p), g, sa, vbd in zip(chains, gs, sas, vbds):
        ys[d][bb, :, p * LANES:(p + 1) * LANES] = (
            g[CHUNK:, :] + _dot(part(bb, d, p, 4), _halves(sa, low).astype(bf16)) + _dot(part(bb, d, p, 5), vbd))
    for (bb, d, p), s0, sa, v in zip(chains, s0s, sas, vals):
        upd = lax.dot_general(
            jnp.concatenate([part(bb, d, p, 6), part(bb, d, p, 7)], axis=0),
            jnp.concatenate([sa, v], axis=0).astype(bf16),
            (((0,), (0,)), ((), ())), preferred_element_type=f32)
        wc = part(bb, d, p, 8)
        wcol = jnp.concatenate([wc, wc], axis=0).T
        st_ref[(bb * 2 + d) * N_PAIRS + p] = jnp.where(same_head, wcol * s0 + upd, jnp.zeros_like(s0))


CHUNK_SCAN_BATCH = 4


def _chunk_scan(v, fwd_parts, bwd_parts, *, B, S):
    nC = S // CHUNK
    nb = CHUNK_SCAN_BATCH if B % CHUNK_SCAN_BATCH == 0 else 1
    as3 = lambda a: a.reshape(B, S, C_R)
    fwd = pl.BlockSpec((nb, CHUNK, C_R), lambda b, i: (b, i, 0))
    bwd = pl.BlockSpec((nb, CHUNK, C_R), lambda b, i: (b, nC - 1 - i, 0))
    yf, yb = pl.pallas_call(
        functools.partial(_chunk_scan_kernel, nb=nb),
        grid=(B // nb, nC),
        in_specs=[fwd, bwd] + [fwd] * 9 + [bwd] * 9,
        out_specs=[fwd, bwd],
        out_shape=[jax.ShapeDtypeStruct((B, S, C_R), f32)] * 2,
        scratch_shapes=[pltpu.VMEM((nb * 2 * N_PAIRS, LANES, LANES), f32)],
        compiler_params=pltpu.CompilerParams(
            dimension_semantics=("parallel", "arbitrary"), vmem_limit_bytes=VMEM_LIMIT),
        name="wkv_scan",
    )(as3(v), as3(v), *[as3(a) for a in fwd_parts], *[as3(a) for a in bwd_parts])
    return yf.reshape(B * S, C_R), yb.reshape(B * S, C_R)

def _mix_kernel(x_ref, attn_ref, yf_ref, yb_ref, r_ref, kf_ref, kb_ref, v_ref, g_ref,
                lng_ref, lnb_ref, rk_ref, aog_ref, wout_ref, jseg_ref, h_ref):
    jseg = jseg_ref[...]
    y = yf_ref[...] + yb_ref[...]
    mu = _split_dot(y, jseg) * (1.0 / HEAD_N)
    dlt = y - mu
    var = _split_dot(dlt * dlt, jseg) * (1.0 / HEAD_N)
    yn = dlt * lax.rsqrt(var + GN_EPS) * lng_ref[...] + lnb_ref[...]
    r = r_ref[...]
    k_mean = 0.5 * (kf_ref[...] + kb_ref[...])
    bonus = _split_dot(r * k_mean * rk_ref[...], jseg) * v_ref[...]
    rw = ((yn + bonus) * g_ref[...]).astype(bf16)
    attn = jnp.concatenate([attn_ref[0, p] for p in range(H_A // 2)], axis=-1)
    an = _rms(attn, aog_ref[...]).astype(bf16)
    ha = H_A * V_DIM
    h_ref[...] = x_ref[...] + _dot(an, wout_ref[0:ha, :]) + _dot(rw, wout_ref[ha:, :])


def _mix(x, attn, yf, yb, r, kf, kb, v, g, lng, lnb, rk, aog, wout, jseg, *, B, S, tm):
    nS = S // tm
    const = lambda shape: pl.BlockSpec(shape, lambda b, s: (0,) * len(shape))
    row = lambda c: pl.BlockSpec((tm, c), lambda b, s: (b * nS + s, 0))
    return pl.pallas_call(
        _mix_kernel,
        grid=(B, nS),
        in_specs=[row(D_MODEL), pl.BlockSpec((1, H_A // 2, tm, LANES), lambda b, s: (b, 0, s, 0))]
                 + [row(C_R)] * 7
                 + [const((1, C_R))] * 3 + [const((1, H_A * V_DIM)), const(wout.shape), const(jseg.shape)],
        out_specs=row(D_MODEL),
        out_shape=jax.ShapeDtypeStruct((B * S, D_MODEL), f32),
        compiler_params=pltpu.CompilerParams(
            dimension_semantics=("parallel", "parallel"), vmem_limit_bytes=VMEM_LIMIT),
        name="mix",
    )(x, attn, yf, yb, r, kf, kb, v, g, lng, lnb, rk, aog, wout, jseg)


def _take_max(s, ids):
    m = jnp.max(s, axis=0, keepdims=True)
    pick = jnp.min(jnp.where(s == m, ids, jnp.int32(2 ** 30)), axis=0, keepdims=True)
    return m, pick, jnp.where(ids == pick, -jnp.inf, s)


CAND_COLS = tuple(P_TOPK // (a + 1) for a in range(P_TOPK))
N_CAND = sum(CAND_COLS)
N_CAND_PAD = -(-N_CAND // 8) * 8


def _route_kernel(h_ref, g2_ref, wpq_ref, sk_ref, pos_ref, idx_ref, gate_ref, v1_ref, i1_ref, v2_ref, i2_ref, cs_ref,
                  ci_ref, bs_ref, bi_ref):
    tm = h_ref.shape[0]
    hn = _rms(h_ref[...], g2_ref[...]).astype(bf16)
    pq = _dot(hn, wpq_ref[...])
    key_ids = lax.broadcasted_iota(jnp.int32, (N_KEYS, tm), 0)
    pos_ids = pos_ref[...]
    for p in range(P_HEADS):
        for side, (vs_ref, is_ref) in enumerate(((v1_ref, i1_ref), (v2_ref, i2_ref))):
            qh = pq[:, p * D_KEY + side * HALF_KEY:p * D_KEY + (side + 1) * HALF_KEY].astype(bf16)
            s = _dot_nt(sk_ref[side], qh)
            for j in range(P_TOPK):
                m, pick, s = _take_max(s, key_ids)
                vs_ref[j:j + 1, :] = m
                is_ref[j:j + 1, :] = pick
        v2 = v2_ref[...]
        i2 = i2_ref[...]
        off = 0
        for a, nb in enumerate(CAND_COLS):
            cs_ref[off:off + nb, :] = v1_ref[a:a + 1, :] + v2[:nb, :]
            ci_ref[off:off + nb, :] = i1_ref[a:a + 1, :] * N_KEYS + i2[:nb, :]
            off += nb
        cs_ref[N_CAND:, :] = jnp.full((N_CAND_PAD - N_CAND, tm), -jnp.inf, f32)
        ci_ref[N_CAND:, :] = jnp.zeros((N_CAND_PAD - N_CAND, tm), jnp.int32)
        s = cs_ref[...]
        ci = ci_ref[...]
        for j in range(P_TOPK):
            m, pick, s = _take_max(s, pos_ids)
            bs_ref[j:j + 1, :] = m
            bi_ref[p * P_TOPK + j:p * P_TOPK + j + 1, :] = jnp.max(
                jnp.where(pos_ids == pick, ci, -1), axis=0, keepdims=True)
        bs = bs_ref[...]
        e = jnp.exp(bs - bs[0:1, :])
        gate_ref[:, p * P_TOPK:(p + 1) * P_TOPK] = (e / jnp.sum(e, axis=0, keepdims=True)).T
    idx_ref[...] = bi_ref[...].T


def _route(h, g2, wpq, sk, *, T, tm):
    const = lambda shape: pl.BlockSpec(shape, lambda i: (0,) * len(shape))
    pk = P_HEADS * P_TOPK
    flat = [a * P_TOPK + b for a, nb in enumerate(CAND_COLS) for b in range(nb)] + [2 ** 20] * (N_CAND_PAD - N_CAND)
    pos = jnp.broadcast_to(jnp.array(flat, jnp.int32)[:, None], (N_CAND_PAD, tm))
    return pl.pallas_call(
        _route_kernel,
        grid=(T // tm,),
        in_specs=[pl.BlockSpec((tm, D_MODEL), lambda i: (i, 0)), const((1, D_MODEL)), const(wpq.shape),
                  const(sk.shape), const((N_CAND_PAD, tm))],
        out_specs=[pl.BlockSpec((tm, pk), lambda i: (i, 0))] * 2,
        out_shape=[jax.ShapeDtypeStruct((T, pk), jnp.int32), jax.ShapeDtypeStruct((T, pk), f32)],
        scratch_shapes=[pltpu.VMEM((P_TOPK, tm), f32), pltpu.VMEM((P_TOPK, tm), jnp.int32),
                        pltpu.VMEM((P_TOPK, tm), f32), pltpu.VMEM((P_TOPK, tm), jnp.int32),
                        pltpu.VMEM((N_CAND_PAD, tm), f32), pltpu.VMEM((N_CAND_PAD, tm), jnp.int32),
                        pltpu.VMEM((P_TOPK, tm), f32), pltpu.VMEM((pk, tm), jnp.int32)],
        compiler_params=pltpu.CompilerParams(dimension_semantics=("parallel",), vmem_limit_bytes=VMEM_LIMIT),
        name="route",
    )(h, g2, wpq, sk, pos)


GATHER_WINDOW = 32
GATHER_RING = 4
GATHER_PARTS = 4
SC_CORES = 2
SC_SUBCORES = 16
SC_WORKERS = SC_CORES * SC_SUBCORES


def _gather_rows(table, idx):
    n = idx.shape[0]
    d = table.shape[1]
    win, ring, parts = GATHER_WINDOW, GATHER_RING, GATHER_PARTS
    m = n // (SC_WORKERS * win * parts)
    assert m * SC_WORKERS * win * parts == n and m % ring == 0, (n, m)
    mesh = plsc.VectorSubcoreMesh(core_axis_name="core", subcore_axis_name="subcore")

    @functools.partial(
        pl.kernel, out_type=jax.ShapeDtypeStruct((n, d), table.dtype), mesh=mesh,
        scratch_types=[pltpu.VMEM((m, win), jnp.int32), pltpu.VMEM((ring, win, d), table.dtype)]
                      + [pltpu.SemaphoreType.DMA] * (2 * ring))
    def gather(tab_hbm, idx_hbm, out_hbm, idx_v, rows_v, *sems):
        gsem, wsem = sems[:ring], sems[ring:]
        wid = lax.axis_index("subcore") * SC_CORES + lax.axis_index("core")

        @pl.loop(0, parts)
        def _(q):
            base = (wid * parts + q) * (m * win)
            pltpu.sync_copy(idx_hbm.at[wid, q], idx_v)

            def fetch(j, b):
                return pltpu.make_async_copy(tab_hbm.at[idx_v.at[j]], rows_v.at[b], gsem[b])

            def flush(j, b):
                return pltpu.make_async_copy(rows_v.at[b], out_hbm.at[pl.ds(base + j * win, win)], wsem[b])

            for b in range(ring - 1):
                fetch(b, b).start()

            @pl.loop(0, m, step=ring)
            def _(j0):
                for b in range(ring):
                    j = j0 + b
                    prev = (b - 1) % ring
                    fetch(j, b).wait()
                    flush(j, b).start()

                    @pl.when(j >= 1)
                    def _():
                        flush(j - 1, prev).wait()

                    @pl.when(j + ring - 1 < m)
                    def _():
                        fetch(j + ring - 1, prev).start()

            flush(m - 1, (m - 1) % ring).wait()

    return gather(table, idx.reshape(SC_WORKERS, parts, m, win))


HALF_D = D_MODEL // 2


def _pack_rows(table):
    bits = lax.bitcast_convert_type(table.astype(bf16), jnp.uint16).astype(jnp.uint32)
    return bits[:, :HALF_D] | (bits[:, HALF_D:] << 16)


def _unpack_rows(words):
    lo = lax.bitcast_convert_type(words << 16, f32).astype(bf16)
    hi = lax.bitcast_convert_type(words & jnp.uint32(0xFFFF0000), f32).astype(bf16)
    return lo, hi


def _expert_kernel(h_ref, g2_ref, gate_ref, gu_ref, gv_ref, o_ref):
    h = h_ref[...]
    tb = h.shape[0]
    pk = P_HEADS * P_TOPK
    hn = _rms(h, g2_ref[...]).astype(bf16)
    row = lax.broadcasted_iota(jnp.int32, (8, pk), 0)
    for grp in range(tb // 8):
        rows = slice(grp * 8, (grp + 1) * 8)
        hn8 = hn[rows, :]
        act = jnp.zeros((8, pk), f32)
        for j in range(8):
            t = grp * 8 + j
            ulo, uhi = _unpack_rows(gu_ref[t * pk:(t + 1) * pk, :])
            act = jnp.where(row == j, _dot_nt(hn8[:, :HALF_D], ulo) + _dot_nt(hn8[:, HALF_D:], uhi), act)
        gelu = 0.5 * act * (1.0 + lax.erf(act * (2.0 ** -0.5)))
        w = (gate_ref[rows, :] * gelu).astype(bf16)
        out_lo = jnp.zeros((8, HALF_D), f32)
        out_hi = jnp.zeros((8, HALF_D), f32)
        for j in range(8):
            t = grp * 8 + j
            vlo, vhi = _unpack_rows(gv_ref[t * pk:(t + 1) * pk, :])
            wj = jnp.where(row == j, w, jnp.zeros_like(w))
            out_lo = out_lo + _dot(wj, vlo)
            out_hi = out_hi + _dot(wj, vhi)
        o_ref[rows, :HALF_D] = h[rows, :HALF_D] + out_lo
        o_ref[rows, HALF_D:] = h[rows, HALF_D:] + out_hi


def _experts(h, g2, gate, gu, gv, *, tb):
    n = h.shape[0]
    pk = P_HEADS * P_TOPK
    const = lambda shape: pl.BlockSpec(shape, lambda i: (0,) * len(shape))
    return pl.pallas_call(
        _expert_kernel,
        grid=(n // tb,),
        in_specs=[pl.BlockSpec((tb, D_MODEL), lambda i: (i, 0)), const((1, D_MODEL)),
                  pl.BlockSpec((tb, pk), lambda i: (i, 0)),
                  pl.BlockSpec((tb * pk, HALF_D), lambda i: (i, 0)),
                  pl.BlockSpec((tb * pk, HALF_D), lambda i: (i, 0))],
        out_specs=pl.BlockSpec((tb, D_MODEL), lambda i: (i, 0)),
        out_shape=jax.ShapeDtypeStruct((n, D_MODEL), f32),
        compiler_params=pltpu.CompilerParams(dimension_semantics=("parallel",), vmem_limit_bytes=VMEM_LIMIT),
        name="experts",
    )(h, g2, gate, gu, gv)


def _place(cols, width, offset):
    return jnp.pad(cols, ((0, 0), (offset, width - offset - cols.shape[1])))


def _rope_partner():
    half = ROPE_DIM // 2
    return jnp.concatenate([jnp.arange(half, ROPE_DIM), jnp.arange(0, half)])


def _prepare(norm1_g, w_in, q_lat_g, w_uq, kv_lat_g, w_ukv, q_norm_g, k_norm_g, attn_out_g, mu_prev, mu_next,
             w0, w_up, a0, a_up, g_up, k_k, k_a, r_k, ln_x_g, ln_x_b, w_out, norm2_g, w_pq, sub_keys, S):
    partner = _rope_partner()
    wz = w_in[:, OFF_RWKV:]
    wkr = w_in[:, OFF_KR:OFF_RWKV]
    win = jnp.concatenate([
        w_in[:, OFF_Q:OFF_KV], w_in[:, OFF_KV:OFF_KR], wz,
        _place(wkr, LANES, NOPE_DIM), _place(wkr[:, partner], LANES, NOPE_DIM)], axis=1).astype(bf16)

    wq = w_uq.reshape(Q_LORA, H_A, QK_DIM)
    wqa = jnp.pad(wq, ((0, 0), (0, 0), (0, HEAD_PAD - QK_DIM))).reshape(Q_LORA, H_A * HEAD_PAD).astype(bf16)
    wqb = jnp.pad(wq[:, :, NOPE_DIM:][:, :, partner],
                  ((0, 0), (0, 0), (NOPE_DIM, HEAD_PAD - QK_DIM))).reshape(Q_LORA, H_A * HEAD_PAD).astype(bf16)
    wkv = w_ukv.reshape(KV_LORA, H_A, NOPE_DIM + V_DIM)
    wkn = jnp.pad(wkv[:, :, :NOPE_DIM], ((0, 0), (0, 0), (0, HEAD_PAD - NOPE_DIM))).reshape(
        KV_LORA, H_A * HEAD_PAD).astype(bf16)
    wv = jnp.pad(wkv[:, :, NOPE_DIM:], ((0, 0), (0, 0), (0, LANES - V_DIM))).reshape(
        KV_LORA, H_A * LANES).astype(bf16)

    def gain_rows(g):
        ga = _place(g[None, :], LANES, 0)
        gb = _place(g[None, NOPE_DIM:][:, partner], LANES, NOPE_DIM)
        return ga, gb

    gqa, gqb = gain_rows(q_norm_g)
    gq = jnp.concatenate([gqa, gqb, jnp.zeros((6, LANES), f32)], axis=0)
    gkn = _place(k_norm_g[None, :NOPE_DIM], LANES, 0)
    gka = _place(k_norm_g[None, NOPE_DIM:], LANES, NOPE_DIM)
    gkb = _place(k_norm_g[None, NOPE_DIM:][:, partner], LANES, NOPE_DIM)
    gk = jnp.concatenate([gkn, gka, gkb, jnp.zeros((5, LANES), f32)], axis=0)

    half = ROPE_DIM // 2
    inv = 1.0 / (ROPE_THETA ** (jnp.arange(half, dtype=f32) / half))
    ang = jnp.arange(S, dtype=f32)[:, None] * inv[None, :]
    c, s = jnp.cos(ang), jnp.sin(ang)
    cos = jnp.concatenate([jnp.ones((S, NOPE_DIM), f32), c, c, jnp.zeros((S, HEAD_PAD - QK_DIM), f32)], axis=1)
    sin = jnp.concatenate([jnp.zeros((S, NOPE_DIM), f32), -s, s, jnp.zeros((S, HEAD_PAD - QK_DIM), f32)], axis=1)

    zeros = jnp.zeros((W_LORA, C_R), f32)
    wl = jnp.stack([jnp.concatenate([jnp.concatenate([w_up[d], zeros], axis=1),
                                     jnp.concatenate([zeros, a_up[d]], axis=1)], axis=0) for d in range(2)]).astype(bf16)
    seg = jnp.arange(C_R) // HEAD_N
    jseg = (seg[:, None] == seg[None, :]).astype(bf16)

    return dict(
        g1=norm1_g[None, :], win=win, qlg=q_lat_g[None, :], wqa=wqa, wqb=wqb, kvlg=kv_lat_g[None, :], wkn=wkn, wv=wv,
        cos=cos, sin=sin, gq=gq, gk=gk,
        mup=mu_prev[None, :], mun=mu_next[None, :], wl=wl, w0=w0, a0=a0,
        gup=g_up.astype(bf16), kk=k_k[None, :], ka=k_a[None, :], jseg=jseg,
        lng=ln_x_g[None, :], lnb=ln_x_b[None, :], rk=r_k[None, :], aog=attn_out_g[None, :],
        wout=w_out.astype(bf16), g2=norm2_g[None, :], wpq=w_pq.astype(bf16), sk=sub_keys.astype(bf16),
    )


def _tile(n, pref):
    t = min(n, pref)
    assert n % t == 0, (n, t)
    return t


PEER_CHUNK = 4096
EXPERT_TOKENS = 32


def _dense(x, w):
    B, S, _ = x.shape
    T = B * S
    xt = x.reshape(T, D_MODEL)
    tm = _tile(S, 256)
    q, k, v, z = _inproj(xt, w["g1"], w["win"], w["qlg"], w["wqa"], w["wqb"], w["kvlg"], w["wkn"], w["wv"],
                         w["cos"], w["sin"], w["gq"], w["gk"], B=B, S=S, tm=tm)
    attn = _attention(q, k, v, B=B, S=S, tq=_tile(S, 512), tk=_tile(S // 2, 512))
    r, vr, nkk, g, decf, kkaf, kf, decb, kkab, kb = _rwkv_prep(
        z, w["mup"], w["mun"], w["wl"], w["w0"], w["a0"], w["gup"], w["kk"], w["ka"], w["jseg"], B=B, S=S, tm=tm)
    fw = _wkv_chunks(r, nkk, decf, kkaf, kf, T=T, reverse=False)
    bw = _wkv_chunks(r, nkk, decb, kkab, kb, T=T, reverse=True)
    yf, yb = _chunk_scan(vr, fw, bw, B=B, S=S)
    h = _mix(xt, attn, yf, yb, r, kf, kb, vr, g, w["lng"], w["lnb"], w["rk"], w["aog"], w["wout"], w["jseg"],
             B=B, S=S, tm=tm)
    idx, gate = _route(h, w["g2"], w["wpq"], w["sk"], T=T, tm=_tile(T, 256))
    return h, idx, gate


def _peer(h, idx, gate, w, pu, pv, shape):
    T = h.shape[0]
    pk = P_HEADS * P_TOPK
    tc = _tile(T, PEER_CHUNK)
    outs = []
    for c in range(T // tc):
        sl = slice(c * tc, (c + 1) * tc)
        flat = idx[sl].reshape(tc * pk)
        gu = _gather_rows(pu, flat)
        gv = _gather_rows(pv, flat)
        outs.append(_experts(h[sl], w["g2"], gate[sl], gu, gv, tb=_tile(tc, EXPERT_TOKENS)))
    return jnp.concatenate(outs, axis=0).reshape(shape)


def _layer(xs, w, expert_u, expert_v):
    pu = _pack_rows(expert_u)
    pv = _pack_rows(expert_v)
    outs = []
    pending = None
    for x in xs:
        cur = _dense(x, w) + (x.shape,)
        if pending is not None:
            outs.append(_peer(*pending[:3], w, pu, pv, pending[3]))
        pending = cur
    outs.append(_peer(*pending[:3], w, pu, pv, pending[3]))
    return outs


GROUP_BATCH = 1


def kernel(x_prompt, x_sample, norm1_g, w_in, q_lat_g, w_uq, kv_lat_g, w_ukv, q_norm_g, k_norm_g, attn_out_g, mu_prev, mu_next, w0, w_up, a0, a_up, g_up, k_k, k_a, r_k, ln_x_g, ln_x_b, w_out, norm2_g, w_pq, sub_keys, expert_u, expert_v):
    assert x_prompt.shape[1] == x_sample.shape[1]
    S = x_prompt.shape[1]
    groups = []
    for x in (x_prompt, x_sample):
        gb = GROUP_BATCH if x.shape[0] % GROUP_BATCH == 0 else x.shape[0]
        groups.append([x[i:i + gb] for i in range(0, x.shape[0], gb)])
    xs = groups[0] + groups[1]
    for l in range(norm1_g.shape[0]):
        w = _prepare(norm1_g[l], w_in[l], q_lat_g[l], w_uq[l], kv_lat_g[l], w_ukv[l], q_norm_g[l], k_norm_g[l],
                     attn_out_g[l], mu_prev[l], mu_next[l], w0[l], w_up[l], a0[l], a_up[l], g_up[l], k_k[l], k_a[l],
                     r_k[l], ln_x_g[l], ln_x_b[l], w_out[l], norm2_g[l], w_pq[l], sub_keys[l], S)
        xs = _layer(xs, w, expert_u[l], expert_v[l])
    n0 = len(groups[0])
    join = lambda parts: parts[0] if len(parts) == 1 else jnp.concatenate(parts, axis=0)
    return join(xs[:n0]), join(xs[n0:])
```

```python
import functools
import math

import jax
import jax.numpy as jnp
from jax import lax
from jax.experimental import pallas as pl
from jax.experimental.pallas import tpu as pltpu
from jax.experimental.pallas import tpu_sc as plsc

D_MODEL = 1024
H_A = 8
NOPE_DIM = 64
ROPE_DIM = 32
QK_DIM = NOPE_DIM + ROPE_DIM
V_DIM = 64
Q_LORA = 384
KV_LORA = 256
ROPE_THETA = 10000.0
H_R = 8
HEAD_N = 64
C_R = H_R * HEAD_N
W_LORA = 64
A_LORA = 64
G_LORA = 128
GN_EPS = 64e-5
RWKV_IN = 3 * C_R + W_LORA + A_LORA + G_LORA
OFF_Q = 0
OFF_KV = OFF_Q + Q_LORA
OFF_KR = OFF_KV + KV_LORA
OFF_RWKV = OFF_KR + ROPE_DIM
N_KEYS = 128
N_EXPERTS = N_KEYS * N_KEYS
P_HEADS = 8
P_TOPK = 16
D_KEY = 256
HALF_KEY = D_KEY // 2
NORM_EPS = 1e-6

LANES = 128
HEAD_PAD = LANES
Z_COLS = RWKV_IN
IN_PAD = Q_LORA + KV_LORA + Z_COLS + 2 * LANES
VMEM_LIMIT = 56 * 1024 * 1024

f32 = jnp.float32
bf16 = jnp.bfloat16


def _rms(x, g):
    return x * lax.rsqrt(jnp.mean(x * x, axis=-1, keepdims=True) + NORM_EPS) * g


def _dot(a, b):
    return jnp.dot(a, b, preferred_element_type=f32)


def _dot_nt(a, b):
    return lax.dot_general(a, b, (((1,), (1,)), ((), ())), preferred_element_type=f32)


def _split_dot(x, j):
    hi = x.astype(bf16)
    lo = (x - hi.astype(f32)).astype(bf16)
    return _dot(hi, j) + _dot(lo, j)


def _inproj_kernel(x_ref, g1_ref, win_ref, qlg_ref, wqa_ref, wqb_ref, kvlg_ref, wkn_ref, wv_ref,
                   cos_ref, sin_ref, gq_ref, gk_ref,
                   q_ref, k_ref, v_ref, z_ref):
    x = x_ref[...]
    xn = _rms(x, g1_ref[...]).astype(bf16)
    proj = _dot(xn, win_ref[...])
    o_kv, o_z, o_ra, o_rb = Q_LORA, Q_LORA + KV_LORA, Q_LORA + KV_LORA + Z_COLS, Q_LORA + KV_LORA + Z_COLS + LANES
    z_ref[...] = proj[:, o_z:o_ra]
    cos = cos_ref[...]
    sin = sin_ref[...]

    ql = _rms(proj[:, :o_kv], qlg_ref[...]).astype(bf16)
    qa = _dot(ql, wqa_ref[...])
    qb = _dot(ql, wqb_ref[...])
    ga, gb = gq_ref[0:1, :], gq_ref[1:2, :]
    scale = QK_DIM ** -0.5 * math.log2(math.e)
    for h in range(H_A):
        a = qa[:, h * LANES:(h + 1) * LANES]
        b = qb[:, h * LANES:(h + 1) * LANES]
        s = lax.rsqrt(jnp.sum(a * a, axis=-1, keepdims=True) * (1.0 / QK_DIM) + NORM_EPS) * scale
        q_ref[0, h] = (s * (a * ga * cos + b * gb * sin)).astype(bf16)

    kvl = _rms(proj[:, o_kv:o_z], kvlg_ref[...]).astype(bf16)
    kn = _dot(kvl, wkn_ref[...])
    vv = _dot(kvl, wv_ref[...])
    one = (lax.broadcasted_iota(jnp.int32, (1, LANES), 1) == V_DIM).astype(f32)
    for h in range(H_A):
        v_ref[0, h] = (vv[:, h * LANES:(h + 1) * LANES] + one).astype(bf16)
    kra = proj[:, o_ra:o_rb]
    krb = proj[:, o_rb:o_rb + LANES]
    gkn, gka, gkb = gk_ref[0:1, :], gk_ref[1:2, :], gk_ref[2:3, :]
    kr = kra * gka * cos + krb * gkb * sin
    ssr = jnp.sum(kra * kra, axis=-1, keepdims=True)
    for h in range(H_A):
        a = kn[:, h * LANES:(h + 1) * LANES]
        s = lax.rsqrt((jnp.sum(a * a, axis=-1, keepdims=True) + ssr) * (1.0 / QK_DIM) + NORM_EPS)
        k_ref[0, h] = (s * (a * gkn + kr)).astype(bf16)


def _inproj(x, g1, win, qlg, wqa, wqb, kvlg, wkn, wv, cos, sin, gq, gk, *, B, S, tm):
    nS = S // tm
    const = lambda shape: pl.BlockSpec(shape, lambda b, s: (0,) * len(shape))
    return pl.pallas_call(
        _inproj_kernel,
        grid=(B, nS),
        in_specs=[
            pl.BlockSpec((tm, D_MODEL), lambda b, s: (b * nS + s, 0)),
            const((1, D_MODEL)), const(win.shape), const((1, Q_LORA)), const(wqa.shape), const(wqb.shape),
            const((1, KV_LORA)), const(wkn.shape), const(wv.shape),
            pl.BlockSpec((tm, LANES), lambda b, s: (s, 0)),
            pl.BlockSpec((tm, LANES), lambda b, s: (s, 0)),
            const(gq.shape), const(gk.shape),
        ],
        out_specs=[
            pl.BlockSpec((1, H_A, tm, HEAD_PAD), lambda b, s: (b, 0, s, 0)),
            pl.BlockSpec((1, H_A, tm, HEAD_PAD), lambda b, s: (b, 0, s, 0)),
            pl.BlockSpec((1, H_A, tm, LANES), lambda b, s: (b, 0, s, 0)),
            pl.BlockSpec((tm, Z_COLS), lambda b, s: (b * nS + s, 0)),
        ],
        out_shape=[
            jax.ShapeDtypeStruct((B, H_A, S, HEAD_PAD), bf16),
            jax.ShapeDtypeStruct((B, H_A, S, HEAD_PAD), bf16),
            jax.ShapeDtypeStruct((B, H_A, S, LANES), bf16),
            jax.ShapeDtypeStruct((B * S, Z_COLS), f32),
        ],
        compiler_params=pltpu.CompilerParams(
            dimension_semantics=("parallel", "parallel"), vmem_limit_bytes=VMEM_LIMIT),
        name="in_proj",
    )(x, g1, win, qlg, wqa, wqb, kvlg, wkn, wv, cos, sin, gq, gk)


ATTN_ROWS = 32
ATTN_UNROLL = 16


def _attn_kernel(q_ref, k_ref, v_ref, o_ref, s_ref, p_ref, m_ref, al_ref, acc_ref, *, tk, nk):
    tq = q_ref.shape[2]
    nlt = tk // LANES

    def scores(h, q, j, buf):
        off = pl.multiple_of(jnp.minimum(j, nk - 1) * tk, tk)
        s_ref[buf] = _dot_nt(q, k_ref[0, h, pl.ds(off, tk), :])

    def softmax(buf):
        def chunk(c, carry):
            rows = pl.ds(pl.multiple_of(c * ATTN_ROWS, ATTN_ROWS), ATTN_ROWS)
            tiles = [s_ref[buf, rows, lt * LANES:(lt + 1) * LANES] for lt in range(nlt)]
            best = functools.reduce(jnp.maximum, tiles)
            m_old = m_ref[rows, :]
            m_new = jnp.maximum(m_old, jnp.broadcast_to(jnp.max(best, axis=-1, keepdims=True), m_old.shape))
            for lt in range(nlt):
                p_ref[buf, rows, lt * LANES:(lt + 1) * LANES] = jnp.exp2(tiles[lt] - m_new).astype(bf16)
            al_ref[rows, :] = jnp.exp2(m_old - m_new)
            m_ref[rows, :] = m_new
            return carry

        lax.fori_loop(0, tq // ATTN_ROWS, chunk, 0, unroll=ATTN_UNROLL)

    def accumulate(h, j, buf):
        off = pl.multiple_of(j * tk, tk)
        acc_ref[...] = al_ref[...] * acc_ref[...] + _dot(p_ref[buf], v_ref[0, h, pl.ds(off, tk), :])

    outs = []
    for h in range(2):
        q = q_ref[0, h]
        m_ref[...] = jnp.full(m_ref.shape, -jnp.inf, f32)
        acc_ref[...] = jnp.zeros(acc_ref.shape, f32)
        scores(h, q, 0, 0)

        def body(jj, carry, h=h, q=q):
            j = 2 * jj
            scores(h, q, j + 1, 1)
            softmax(0)
            accumulate(h, j, 0)
            scores(h, q, j + 2, 0)
            softmax(1)
            accumulate(h, j + 1, 1)
            return carry

        lax.fori_loop(0, nk // 2, body, 0)
        acc = acc_ref[...]
        outs.append(acc / acc[:, V_DIM:V_DIM + 1])
    lane = lax.broadcasted_iota(jnp.int32, (tq, LANES), 1)
    o_ref[0, 0] = jnp.where(lane < V_DIM, outs[0], pltpu.roll(outs[1], V_DIM, 1))


def _attention(q, k, v, *, B, S, tq, tk):
    return pl.pallas_call(
        functools.partial(_attn_kernel, tk=tk, nk=S // tk),
        grid=(B, H_A // 2, S // tq),
        in_specs=[
            pl.BlockSpec((1, 2, tq, HEAD_PAD), lambda b, p, i: (b, p, i, 0)),
            pl.BlockSpec((1, 2, S, HEAD_PAD), lambda b, p, i: (b, p, 0, 0)),
            pl.BlockSpec((1, 2, S, LANES), lambda b, p, i: (b, p, 0, 0)),
        ],
        out_specs=pl.BlockSpec((1, 1, tq, LANES), lambda b, p, i: (b, p, i, 0)),
        out_shape=jax.ShapeDtypeStruct((B, H_A // 2, S, LANES), f32),
        scratch_shapes=[pltpu.VMEM((2, tq, tk), f32), pltpu.VMEM((2, tq, tk), bf16), pltpu.VMEM((tq, LANES), f32),
                        pltpu.VMEM((tq, LANES), f32), pltpu.VMEM((tq, LANES), f32)],
        compiler_params=pltpu.CompilerParams(
            dimension_semantics=("parallel", "parallel", "parallel"), vmem_limit_bytes=VMEM_LIMIT),
        name="attention",
    )(q, k, v)


def _softplus(x):
    return jnp.maximum(x, 0.0) + jnp.log1p(jnp.exp(-jnp.abs(x)))


def _prep_kernel(z_ref, zp_ref, zn_ref, mup_ref, mun_ref, wl_ref, w0_ref, a0_ref, gup_ref, kk_ref, ka_ref, jseg_ref,
                 r_ref, v_ref, nkk_ref, g_ref,
                 decf_ref, kkaf_ref, kf_ref, decb_ref, kkab_ref, kb_ref):
    si = pl.program_id(1)
    ns = pl.num_programs(1)
    z = z_ref[...]
    tm = z.shape[0]
    row = lax.broadcasted_iota(jnp.int32, (tm, 1), 0)
    zp_row = jnp.where(si == 0, 0.0, zp_ref[7:8, :])
    zn_row = jnp.where(si == ns - 1, 0.0, zn_ref[0:1, :])
    z_prev = jnp.where(row == 0, zp_row, pltpu.roll(z, 1, 0))
    z_next = jnp.where(row == tm - 1, zn_row, pltpu.roll(z, tm - 1, 0))
    zm = z + mup_ref[...] * (z_prev - z) + mun_ref[...] * (z_next - z)

    o1, o2, o3 = C_R, 2 * C_R, 3 * C_R
    o4 = o3 + W_LORA + A_LORA
    o5 = o4 + G_LORA
    r = zm[:, :o1]
    kr = zm[:, o1:o2]
    vr = zm[:, o2:o3]
    lw = zm[:, o3:o4]
    zg = zm[:, o4:o5]
    r_ref[...] = r
    v_ref[...] = vr
    g_ref[...] = _dot(jax.nn.sigmoid(zg).astype(bf16), gup_ref[...])

    kk = kr * kk_ref[...]
    ssk = _split_dot(kk * kk, jseg_ref[...])
    kk = kk / jnp.maximum(jnp.sqrt(ssk), 1e-12)
    nkk_ref[...] = -kk

    lane = lax.broadcasted_iota(jnp.int32, lw.shape, 1)
    lin = jnp.where(lane < W_LORA, jnp.tanh(lw), lw).astype(bf16)
    ka = ka_ref[...]
    for d, (dec_ref, kka_ref, kd_ref) in enumerate(((decf_ref, kkaf_ref, kf_ref), (decb_ref, kkab_ref, kb_ref))):
        lo = _dot(lin, wl_ref[d])
        w = -_softplus(-(w0_ref[d:d + 1, :] + lo[:, :C_R])) - 0.5
        dec_ref[...] = jnp.exp(-jnp.exp(w))
        a = jax.nn.sigmoid(a0_ref[d:d + 1, :] + lo[:, C_R:])
        kd_ref[...] = kr * (1.0 + (a - 1.0) * ka)
        kka_ref[...] = kk * a


def _rwkv_prep(z, mup, mun, wl, w0, a0, gup, kk, ka, jseg, *, B, S, tm):
    nS = S // tm
    T = B * S
    nb8 = tm // 8
    const = lambda shape: pl.BlockSpec(shape, lambda b, s: (0,) * len(shape))
    row_spec = pl.BlockSpec((tm, C_R), lambda b, s: (b * nS + s, 0))
    return pl.pallas_call(
        _prep_kernel,
        grid=(B, nS),
        in_specs=[
            pl.BlockSpec((tm, Z_COLS), lambda b, s: (b * nS + s, 0)),
            pl.BlockSpec((8, Z_COLS), lambda b, s: (jnp.maximum((b * nS + s) * nb8 - 1, 0), 0)),
            pl.BlockSpec((8, Z_COLS), lambda b, s: (jnp.minimum((b * nS + s + 1) * nb8, T // 8 - 1), 0)),
            const((1, Z_COLS)), const((1, Z_COLS)), const(wl.shape), const(w0.shape), const(a0.shape),
            const(gup.shape), const((1, C_R)), const((1, C_R)), const(jseg.shape),
        ],
        out_specs=[row_spec] * 10,
        out_shape=[jax.ShapeDtypeStruct((T, C_R), f32)] * 10,
        compiler_params=pltpu.CompilerParams(
            dimension_semantics=("parallel", "parallel"), vmem_limit_bytes=VMEM_LIMIT),
        name="rwkv_prep",
    )(z, z, z, mup, mun, wl, w0, a0, gup, kk, ka, jseg)


N_PAIRS = H_R // 2
CHUNK = 64


def _halves(x, low):
    zero = jnp.zeros_like(x)
    return jnp.concatenate([jnp.where(low, x, zero), jnp.where(low, zero, x)], axis=0)


def _chunks_kernel(r_ref, nkk_ref, dec_ref, kka_ref, k_ref, tri_ref,
                   at_ref, rt_ref, aak_ref, tt_ref, arb_ref, ark_ref, bt_ref, kt_ref, wc_ref, *, reverse):
    lane = lax.broadcasted_iota(jnp.int32, (CHUNK, LANES), 1)
    low = lane < HEAD_N
    jj = lax.broadcasted_iota(jnp.int32, (CHUNK, LANES), 0)
    ii = lane & (HEAD_N - 1)
    strict = (ii > jj) if reverse else (ii < jj)
    incl = (ii >= jj) if reverse else (ii <= jj)
    rr = lax.broadcasted_iota(jnp.int32, (LANES, LANES), 0)
    cc = lax.broadcasted_iota(jnp.int32, (LANES, LANES), 1)
    eye = (rr == cc).astype(f32)
    last = 0 if reverse else CHUNK - 1
    swap = lambda x: pltpu.roll(x, HEAD_N, 1)
    zero = jnp.zeros((CHUNK, LANES), f32)
    units = [(slice(q * CHUNK, (q + 1) * CHUNK), slice(p * LANES, (p + 1) * LANES))
             for q in range(r_ref.shape[0] // CHUNK) for p in range(N_PAIRS)]
    cums = []
    for rows, cols in units:
        lw = jnp.log(dec_ref[rows, cols])
        hi = lw.astype(bf16)
        r1 = lw - hi.astype(f32)
        mid = r1.astype(bf16)
        lo = (r1 - mid.astype(f32)).astype(bf16)
        cums.append((lw, _dot(tri_ref[...], jnp.concatenate([hi, mid, lo], axis=0))))
    outs = []
    for (rows, cols), (lw, cum) in zip(units, cums):
        w_in = jnp.exp(cum)
        w_ex = jnp.exp(cum - lw)
        w_inv = jnp.exp(-cum)
        wc = w_in[last:last + 1, :]
        at = nkk_ref[rows, cols] * w_ex
        rt = r_ref[rows, cols] * w_in
        bt = kka_ref[rows, cols] * w_inv
        kt = k_ref[rows, cols] * w_inv
        at_ref[rows, cols] = at.astype(bf16)
        rt_ref[rows, cols] = rt.astype(bf16)
        bt_ref[rows, cols] = (bt * wc).astype(bf16)
        kt_ref[rows, cols] = (kt * wc).astype(bf16)
        wc_ref[rows, cols] = jnp.broadcast_to(wc, (CHUNK, LANES))
        lhs = jnp.concatenate([_halves(at, low), _halves(rt, low)], axis=0).astype(bf16)
        rhs = jnp.concatenate([bt, kt], axis=0).astype(bf16)
        outs.append(_dot_nt(lhs, rhs))
    pws, ts = [], []
    for (rows, cols), o in zip(units, outs):
        o0, o1, o2, o3 = (o[q * CHUNK:(q + 1) * CHUNK, :] for q in range(4))
        aab = jnp.where(strict, jnp.where(low, o0, swap(o1)), zero)
        aak_ref[rows, cols] = jnp.where(strict, jnp.where(low, swap(o0), o1), zero).astype(bf16)
        arb_ref[rows, cols] = jnp.where(incl, jnp.where(low, o2, swap(o3)), zero).astype(bf16)
        ark_ref[rows, cols] = jnp.where(incl, jnp.where(low, swap(o2), o3), zero).astype(bf16)
        pws.append(_halves(aab, low))
        ts.append(eye + pws[-1])
    for _ in range(int(math.log2(CHUNK)) - 1):
        pbs = [pw.astype(bf16) for pw in pws]
        pws = [_dot(pb, pb) for pb in pbs]
        ts = [t + _dot(t.astype(bf16), pw.astype(bf16)) for t, pw in zip(ts, pws)]
    for (rows, cols), t in zip(units, ts):
        tt_ref[rows, cols] = jnp.where(low, t[:CHUNK, :], t[CHUNK:, :]).astype(bf16)


CHUNKS_PER_STEP = 4


def _wkv_chunks(r, nkk, dec, kka, k, *, T, reverse):
    tri_i = jnp.arange(CHUNK)
    tri = (tri_i[None, :] >= tri_i[:, None]) if reverse else (tri_i[None, :] <= tri_i[:, None])
    tri3 = jnp.tile(tri.astype(bf16), (1, 3))
    rows = CHUNK * CHUNKS_PER_STEP
    assert T % rows == 0
    row = pl.BlockSpec((rows, C_R), lambda i: (i, 0))
    return pl.pallas_call(
        functools.partial(_chunks_kernel, reverse=reverse),
        grid=(T // rows,),
        in_specs=[row] * 5 + [pl.BlockSpec(tri3.shape, lambda i: (0, 0))],
        out_specs=[row] * 9,
        out_shape=[jax.ShapeDtypeStruct((T, C_R), bf16)] * 8 + [jax.ShapeDtypeStruct((T, C_R), f32)],
        compiler_params=pltpu.CompilerParams(dimension_semantics=("parallel",), vmem_limit_bytes=VMEM_LIMIT),
        name="wkv_chunks",
    )(r, nkk, dec, kka, k, tri3)


def _chunk_scan_kernel(vf_ref, vb_ref, *refs, nb):
    vs = (vf_ref, vb_ref)
    ins = (refs[0:9], refs[9:18])
    ys = refs[18:20]
    st_ref = refs[20]

    @pl.when(pl.program_id(1) == 0)
    def _():
        st_ref[...] = jnp.zeros_like(st_ref)

    low = lax.broadcasted_iota(jnp.int32, (CHUNK, LANES), 1) < HEAD_N
    rr = lax.broadcasted_iota(jnp.int32, (LANES, LANES), 0) < HEAD_N
    cc = lax.broadcasted_iota(jnp.int32, (LANES, LANES), 1) < HEAD_N
    same_head = rr == cc
    chains = [(bb, d, p) for bb in range(nb) for d in range(2) for p in range(N_PAIRS)]
    part = lambda bb, d, p, which: ins[d][which][bb, :, p * LANES:(p + 1) * LANES]
    s0s = [st_ref[(bb * 2 + d) * N_PAIRS + p] for bb, d, p in chains]
    vals = [vs[d][bb, :, p * LANES:(p + 1) * LANES] for bb, d, p in chains]
    vbds = [_halves(v, low).astype(bf16) for v in vals]
    gs = [_dot(jnp.concatenate([part(*ch, 0), part(*ch, 1)], axis=0), s0.astype(bf16))
          for ch, s0 in zip(chains, s0s)]
    xs = [g[:CHUNK, :] + _dot(part(*ch, 2), vbd) for ch, g, vbd in zip(chains, gs, vbds)]
    sas = [_dot(part(*ch, 3), _halves(x, low).astype(bf16)) for ch, x in zip(chains, xs)]
    for (bb, d, p), g, sa, vbd in zip(chains, gs, sas, vbds):
        ys[d][bb, :, p * LANES:(p + 1) * LANES] = (
            g[CHUNK:, :] + _dot(part(bb, d, p, 4), _halves(sa, low).astype(bf16)) + _dot(part(bb, d, p, 5), vbd))
    for (bb, d, p), s0, sa, v in zip(chains, s0s, sas, vals):
        upd = lax.dot_general(
            jnp.concatenate([part(bb, d, p, 6), part(bb, d, p, 7)], axis=0),
            jnp.concatenate([sa, v], axis=0).astype(bf16),
            (((0,), (0,)), ((), ())), preferred_element_type=f32)
        wc = part(bb, d, p, 8)
        wcol = jnp.concatenate([wc, wc], axis=0).T
        st_ref[(bb * 2 + d) * N_PAIRS + p] = jnp.where(same_head, wcol * s0 + upd, jnp.zeros_like(s0))


CHUNK_SCAN_BATCH = 4


def _chunk_scan(v, fwd_parts, bwd_parts, *, B, S):
    nC = S // CHUNK
    nb = CHUNK_SCAN_BATCH if B % CHUNK_SCAN_BATCH == 0 else 1
    as3 = lambda a: a.reshape(B, S, C_R)
    fwd = pl.BlockSpec((nb, CHUNK, C_R), lambda b, i: (b, i, 0))
    bwd = pl.BlockSpec((nb, CHUNK, C_R), lambda b, i: (b, nC - 1 - i, 0))
    yf, yb = pl.pallas_call(
        functools.partial(_chunk_scan_kernel, nb=nb),
        grid=(B // nb, nC),
        in_specs=[fwd, bwd] + [fwd] * 9 + [bwd] * 9,
        out_specs=[fwd, bwd],
        out_shape=[jax.ShapeDtypeStruct((B, S, C_R), f32)] * 2,
        scratch_shapes=[pltpu.VMEM((nb * 2 * N_PAIRS, LANES, LANES), f32)],
        compiler_params=pltpu.CompilerParams(
            dimension_semantics=("parallel", "arbitrary"), vmem_limit_bytes=VMEM_LIMIT),
        name="wkv_scan",
    )(as3(v), as3(v), *[as3(a) for a in fwd_parts], *[as3(a) for a in bwd_parts])
    return yf.reshape(B * S, C_R), yb.reshape(B * S, C_R)

def _mix_kernel(x_ref, attn_ref, yf_ref, yb_ref, r_ref, kf_ref, kb_ref, v_ref, g_ref,
                lng_ref, lnb_ref, rk_ref, aog_ref, wout_ref, jseg_ref, h_ref):
    jseg = jseg_ref[...]
    y = yf_ref[...] + yb_ref[...]
    mu = _split_dot(y, jseg) * (1.0 / HEAD_N)
    dlt = y - mu
    var = _split_dot(dlt * dlt, jseg) * (1.0 / HEAD_N)
    yn = dlt * lax.rsqrt(var + GN_EPS) * lng_ref[...] + lnb_ref[...]
    r = r_ref[...]
    k_mean = 0.5 * (kf_ref[...] + kb_ref[...])
    bonus = _split_dot(r * k_mean * rk_ref[...], jseg) * v_ref[...]
    rw = ((yn + bonus) * g_ref[...]).astype(bf16)
    attn = jnp.concatenate([attn_ref[0, p] for p in range(H_A // 2)], axis=-1)
    an = _rms(attn, aog_ref[...]).astype(bf16)
    ha = H_A * V_DIM
    h_ref[...] = x_ref[...] + _dot(an, wout_ref[0:ha, :]) + _dot(rw, wout_ref[ha:, :])


def _mix(x, attn, yf, yb, r, kf, kb, v, g, lng, lnb, rk, aog, wout, jseg, *, B, S, tm):
    nS = S // tm
    const = lambda shape: pl.BlockSpec(shape, lambda b, s: (0,) * len(shape))
    row = lambda c: pl.BlockSpec((tm, c), lambda b, s: (b * nS + s, 0))
    return pl.pallas_call(
        _mix_kernel,
        grid=(B, nS),
        in_specs=[row(D_MODEL), pl.BlockSpec((1, H_A // 2, tm, LANES), lambda b, s: (b, 0, s, 0))]
                 + [row(C_R)] * 7
                 + [const((1, C_R))] * 3 + [const((1, H_A * V_DIM)), const(wout.shape), const(jseg.shape)],
        out_specs=row(D_MODEL),
        out_shape=jax.ShapeDtypeStruct((B * S, D_MODEL), f32),
        compiler_params=pltpu.CompilerParams(
            dimension_semantics=("parallel", "parallel"), vmem_limit_bytes=VMEM_LIMIT),
        name="mix",
    )(x, attn, yf, yb, r, kf, kb, v, g, lng, lnb, rk, aog, wout, jseg)


def _take_max(s, ids):
    m = jnp.max(s, axis=0, keepdims=True)
    pick = jnp.min(jnp.where(s == m, ids, jnp.int32(2 ** 30)), axis=0, keepdims=True)
    return m, pick, jnp.where(ids == pick, -jnp.inf, s)


CAND_COLS = tuple(P_TOPK // (a + 1) for a in range(P_TOPK))
N_CAND = sum(CAND_COLS)
N_CAND_PAD = -(-N_CAND // 8) * 8


def _route_kernel(h_ref, g2_ref, wpq_ref, sk_ref, pos_ref, idx_ref, gate_ref, v1_ref, i1_ref, v2_ref, i2_ref, cs_ref,
                  ci_ref, bs_ref, bi_ref):
    tm = h_ref.shape[0]
    hn = _rms(h_ref[...], g2_ref[...]).astype(bf16)
    pq = _dot(hn, wpq_ref[...])
    key_ids = lax.broadcasted_iota(jnp.int32, (N_KEYS, tm), 0)
    pos_ids = pos_ref[...]
    for p in range(P_HEADS):
        for side, (vs_ref, is_ref) in enumerate(((v1_ref, i1_ref), (v2_ref, i2_ref))):
            qh = pq[:, p * D_KEY + side * HALF_KEY:p * D_KEY + (side + 1) * HALF_KEY].astype(bf16)
            s = _dot_nt(sk_ref[side], qh)
            for j in range(P_TOPK):
                m, pick, s = _take_max(s, key_ids)
                vs_ref[j:j + 1, :] = m
                is_ref[j:j + 1, :] = pick
        v2 = v2_ref[...]
        i2 = i2_ref[...]
        off = 0
        for a, nb in enumerate(CAND_COLS):
            cs_ref[off:off + nb, :] = v1_ref[a:a + 1, :] + v2[:nb, :]
            ci_ref[off:off + nb, :] = i1_ref[a:a + 1, :] * N_KEYS + i2[:nb, :]
            off += nb
        cs_ref[N_CAND:, :] = jnp.full((N_CAND_PAD - N_CAND, tm), -jnp.inf, f32)
        ci_ref[N_CAND:, :] = jnp.zeros((N_CAND_PAD - N_CAND, tm), jnp.int32)
        s = cs_ref[...]
        ci = ci_ref[...]
        for j in range(P_TOPK):
            m, pick, s = _take_max(s, pos_ids)
            bs_ref[j:j + 1, :] = m
            bi_ref[p * P_TOPK + j:p * P_TOPK + j + 1, :] = jnp.max(
                jnp.where(pos_ids == pick, ci, -1), axis=0, keepdims=True)
        bs = bs_ref[...]
        e = jnp.exp(bs - bs[0:1, :])
        gate_ref[:, p * P_TOPK:(p + 1) * P_TOPK] = (e / jnp.sum(e, axis=0, keepdims=True)).T
    idx_ref[...] = bi_ref[...].T


def _route(h, g2, wpq, sk, *, T, tm):
    const = lambda shape: pl.BlockSpec(shape, lambda i: (0,) * len(shape))
    pk = P_HEADS * P_TOPK
    flat = [a * P_TOPK + b for a, nb in enumerate(CAND_COLS) for b in range(nb)] + [2 ** 20] * (N_CAND_PAD - N_CAND)
    pos = jnp.broadcast_to(jnp.array(flat, jnp.int32)[:, None], (N_CAND_PAD, tm))
    return pl.pallas_call(
        _route_kernel,
        grid=(T // tm,),
        in_specs=[pl.BlockSpec((tm, D_MODEL), lambda i: (i, 0)), const((1, D_MODEL)), const(wpq.shape),
                  const(sk.shape), const((N_CAND_PAD, tm))],
        out_specs=[pl.BlockSpec((tm, pk), lambda i: (i, 0))] * 2,
        out_shape=[jax.ShapeDtypeStruct((T, pk), jnp.int32), jax.ShapeDtypeStruct((T, pk), f32)],
        scratch_shapes=[pltpu.VMEM((P_TOPK, tm), f32), pltpu.VMEM((P_TOPK, tm), jnp.int32),
                        pltpu.VMEM((P_TOPK, tm), f32), pltpu.VMEM((P_TOPK, tm), jnp.int32),
                        pltpu.VMEM((N_CAND_PAD, tm), f32), pltpu.VMEM((N_CAND_PAD, tm), jnp.int32),
                        pltpu.VMEM((P_TOPK, tm), f32), pltpu.VMEM((pk, tm), jnp.int32)],
        compiler_params=pltpu.CompilerParams(dimension_semantics=("parallel",), vmem_limit_bytes=VMEM_LIMIT),
        name="route",
    )(h, g2, wpq, sk, pos)


GATHER_WINDOW = 64
SC_CORES = 2
SC_SUBCORES = 16
SC_WORKERS = SC_CORES * SC_SUBCORES


def _gather_rows(table, idx):
    n = idx.shape[0]
    d = table.shape[1]
    win = GATHER_WINDOW
    n_it = n // (SC_WORKERS * win)
    assert n_it * SC_WORKERS * win == n and n_it % 2 == 0, (n, n_it)
    mesh = plsc.VectorSubcoreMesh(core_axis_name="core", subcore_axis_name="subcore")

    @functools.partial(
        pl.kernel, out_type=jax.ShapeDtypeStruct((n, d), table.dtype), mesh=mesh,
        scratch_types=[pltpu.VMEM((n_it, win), jnp.int32), pltpu.VMEM((2, win, d), table.dtype),
                       pltpu.SemaphoreType.DMA, pltpu.SemaphoreType.DMA,
                       pltpu.SemaphoreType.DMA, pltpu.SemaphoreType.DMA])
    def gather(tab_hbm, idx_hbm, out_hbm, idx_v, rows_v, gsem0, gsem1, wsem0, wsem1):
        wid = lax.axis_index("subcore") * SC_CORES + lax.axis_index("core")
        base = wid * (n_it * win)
        gsem = (gsem0, gsem1)
        wsem = (wsem0, wsem1)
        pltpu.sync_copy(idx_hbm.at[wid], idx_v)

        def fetch(j, b):
            return pltpu.make_async_copy(tab_hbm.at[idx_v.at[j]], rows_v.at[b], gsem[b])

        def flush(j, b):
            return pltpu.make_async_copy(rows_v.at[b], out_hbm.at[pl.ds(base + j * win, win)], wsem[b])

        fetch(0, 0).start()

        @pl.loop(0, n_it, step=2)
        def _(j0):
            for b in range(2):
                j = j0 + b

                @pl.when(j >= 1)
                def _():
                    flush(j - 1, 1 - b).wait()

                @pl.when(j + 1 < n_it)
                def _():
                    fetch(j + 1, 1 - b).start()

                fetch(j, b).wait()
                flush(j, b).start()

        flush(n_it - 1, 1).wait()

    return gather(table, idx.reshape(SC_WORKERS, n_it, win))


HALF_D = D_MODEL // 2


def _pack_rows(table):
    bits = lax.bitcast_convert_type(table.astype(bf16), jnp.uint16).astype(jnp.uint32)
    return bits[:, :HALF_D] | (bits[:, HALF_D:] << 16)


def _unpack_rows(words):
    lo = lax.bitcast_convert_type(words << 16, f32).astype(bf16)
    hi = lax.bitcast_convert_type(words & jnp.uint32(0xFFFF0000), f32).astype(bf16)
    return lo, hi


def _expert_kernel(h_ref, g2_ref, gate_ref, *refs):
    gu_refs, gv_refs, o_ref = refs[:EXPERT_SPLIT], refs[EXPERT_SPLIT:2 * EXPERT_SPLIT], refs[2 * EXPERT_SPLIT]
    h = h_ref[...]
    tb = h.shape[0]
    pk = P_HEADS * P_TOPK
    per = tb // EXPERT_SPLIT

    def token_rows(blocks, t):
        return blocks[t // per][(t % per) * pk:(t % per + 1) * pk, :]

    hn = _rms(h, g2_ref[...]).astype(bf16)
    row = lax.broadcasted_iota(jnp.int32, (8, pk), 0)
    for grp in range(tb // 8):
        rows = slice(grp * 8, (grp + 1) * 8)
        hn8 = hn[rows, :]
        act = jnp.zeros((8, pk), f32)
        for j in range(8):
            t = grp * 8 + j
            ulo, uhi = _unpack_rows(token_rows(gu_refs, t))
            act = jnp.where(row == j, _dot_nt(hn8[:, :HALF_D], ulo) + _dot_nt(hn8[:, HALF_D:], uhi), act)
        gelu = 0.5 * act * (1.0 + lax.erf(act * (2.0 ** -0.5)))
        w = (gate_ref[rows, :] * gelu).astype(bf16)
        out_lo = jnp.zeros((8, HALF_D), f32)
        out_hi = jnp.zeros((8, HALF_D), f32)
        for j in range(8):
            t = grp * 8 + j
            vlo, vhi = _unpack_rows(token_rows(gv_refs, t))
            wj = jnp.where(row == j, w, jnp.zeros_like(w))
            out_lo = out_lo + _dot(wj, vlo)
            out_hi = out_hi + _dot(wj, vhi)
        o_ref[rows, :HALF_D] = h[rows, :HALF_D] + out_lo
        o_ref[rows, HALF_D:] = h[rows, HALF_D:] + out_hi


def _experts(h, g2, gate, gu, gv, *, tb):
    n = h.shape[0]
    pk = P_HEADS * P_TOPK
    const = lambda shape: pl.BlockSpec(shape, lambda i: (0,) * len(shape))
    assert tb % (8 * EXPERT_SPLIT) == 0
    part = tb * pk // EXPERT_SPLIT
    row_specs = [pl.BlockSpec((part, HALF_D), functools.partial(lambda i, q: (i * EXPERT_SPLIT + q, 0), q=q))
                 for q in range(EXPERT_SPLIT)]
    return pl.pallas_call(
        _expert_kernel,
        grid=(n // tb,),
        in_specs=[pl.BlockSpec((tb, D_MODEL), lambda i: (i, 0)), const((1, D_MODEL)),
                  pl.BlockSpec((tb, pk), lambda i: (i, 0))] + row_specs + row_specs,
        out_specs=pl.BlockSpec((tb, D_MODEL), lambda i: (i, 0)),
        out_shape=jax.ShapeDtypeStruct((n, D_MODEL), f32),
        compiler_params=pltpu.CompilerParams(dimension_semantics=("parallel",), vmem_limit_bytes=VMEM_LIMIT),
        name="experts",
    )(h, g2, gate, *([gu] * EXPERT_SPLIT), *([gv] * EXPERT_SPLIT))


def _place(cols, width, offset):
    return jnp.pad(cols, ((0, 0), (offset, width - offset - cols.shape[1])))


def _rope_partner():
    half = ROPE_DIM // 2
    return jnp.concatenate([jnp.arange(half, ROPE_DIM), jnp.arange(0, half)])


def _prepare(norm1_g, w_in, q_lat_g, w_uq, kv_lat_g, w_ukv, q_norm_g, k_norm_g, attn_out_g, mu_prev, mu_next,
             w0, w_up, a0, a_up, g_up, k_k, k_a, r_k, ln_x_g, ln_x_b, w_out, norm2_g, w_pq, sub_keys, S):
    partner = _rope_partner()
    wz = w_in[:, OFF_RWKV:]
    wkr = w_in[:, OFF_KR:OFF_RWKV]
    win = jnp.concatenate([
        w_in[:, OFF_Q:OFF_KV], w_in[:, OFF_KV:OFF_KR], wz,
        _place(wkr, LANES, NOPE_DIM), _place(wkr[:, partner], LANES, NOPE_DIM)], axis=1).astype(bf16)

    wq = w_uq.reshape(Q_LORA, H_A, QK_DIM)
    wqa = jnp.pad(wq, ((0, 0), (0, 0), (0, HEAD_PAD - QK_DIM))).reshape(Q_LORA, H_A * HEAD_PAD).astype(bf16)
    wqb = jnp.pad(wq[:, :, NOPE_DIM:][:, :, partner],
                  ((0, 0), (0, 0), (NOPE_DIM, HEAD_PAD - QK_DIM))).reshape(Q_LORA, H_A * HEAD_PAD).astype(bf16)
    wkv = w_ukv.reshape(KV_LORA, H_A, NOPE_DIM + V_DIM)
    wkn = jnp.pad(wkv[:, :, :NOPE_DIM], ((0, 0), (0, 0), (0, HEAD_PAD - NOPE_DIM))).reshape(
        KV_LORA, H_A * HEAD_PAD).astype(bf16)
    wv = jnp.pad(wkv[:, :, NOPE_DIM:], ((0, 0), (0, 0), (0, LANES - V_DIM))).reshape(
        KV_LORA, H_A * LANES).astype(bf16)

    def gain_rows(g):
        ga = _place(g[None, :], LANES, 0)
        gb = _place(g[None, NOPE_DIM:][:, partner], LANES, NOPE_DIM)
        return ga, gb

    gqa, gqb = gain_rows(q_norm_g)
    gq = jnp.concatenate([gqa, gqb, jnp.zeros((6, LANES), f32)], axis=0)
    gkn = _place(k_norm_g[None, :NOPE_DIM], LANES, 0)
    gka = _place(k_norm_g[None, NOPE_DIM:], LANES, NOPE_DIM)
    gkb = _place(k_norm_g[None, NOPE_DIM:][:, partner], LANES, NOPE_DIM)
    gk = jnp.concatenate([gkn, gka, gkb, jnp.zeros((5, LANES), f32)], axis=0)

    half = ROPE_DIM // 2
    inv = 1.0 / (ROPE_THETA ** (jnp.arange(half, dtype=f32) / half))
    ang = jnp.arange(S, dtype=f32)[:, None] * inv[None, :]
    c, s = jnp.cos(ang), jnp.sin(ang)
    cos = jnp.concatenate([jnp.ones((S, NOPE_DIM), f32), c, c, jnp.zeros((S, HEAD_PAD - QK_DIM), f32)], axis=1)
    sin = jnp.concatenate([jnp.zeros((S, NOPE_DIM), f32), -s, s, jnp.zeros((S, HEAD_PAD - QK_DIM), f32)], axis=1)

    zeros = jnp.zeros((W_LORA, C_R), f32)
    wl = jnp.stack([jnp.concatenate([jnp.concatenate([w_up[d], zeros], axis=1),
                                     jnp.concatenate([zeros, a_up[d]], axis=1)], axis=0) for d in range(2)]).astype(bf16)
    seg = jnp.arange(C_R) // HEAD_N
    jseg = (seg[:, None] == seg[None, :]).astype(bf16)

    return dict(
        g1=norm1_g[None, :], win=win, qlg=q_lat_g[None, :], wqa=wqa, wqb=wqb, kvlg=kv_lat_g[None, :], wkn=wkn, wv=wv,
        cos=cos, sin=sin, gq=gq, gk=gk,
        mup=mu_prev[None, :], mun=mu_next[None, :], wl=wl, w0=w0, a0=a0,
        gup=g_up.astype(bf16), kk=k_k[None, :], ka=k_a[None, :], jseg=jseg,
        lng=ln_x_g[None, :], lnb=ln_x_b[None, :], rk=r_k[None, :], aog=attn_out_g[None, :],
        wout=w_out.astype(bf16), g2=norm2_g[None, :], wpq=w_pq.astype(bf16), sk=sub_keys.astype(bf16),
    )


def _tile(n, pref):
    t = min(n, pref)
    assert n % t == 0, (n, t)
    return t


PEER_CHUNK = 4096
EXPERT_TOKENS = 32
EXPERT_SPLIT = 4


def _dense(x, w):
    B, S, _ = x.shape
    T = B * S
    xt = x.reshape(T, D_MODEL)
    tm = _tile(S, 256)
    q, k, v, z = _inproj(xt, w["g1"], w["win"], w["qlg"], w["wqa"], w["wqb"], w["kvlg"], w["wkn"], w["wv"],
                         w["cos"], w["sin"], w["gq"], w["gk"], B=B, S=S, tm=tm)
    attn = _attention(q, k, v, B=B, S=S, tq=_tile(S, 512), tk=_tile(S // 2, 512))
    r, vr, nkk, g, decf, kkaf, kf, decb, kkab, kb = _rwkv_prep(
        z, w["mup"], w["mun"], w["wl"], w["w0"], w["a0"], w["gup"], w["kk"], w["ka"], w["jseg"], B=B, S=S, tm=tm)
    fw = _wkv_chunks(r, nkk, decf, kkaf, kf, T=T, reverse=False)
    bw = _wkv_chunks(r, nkk, decb, kkab, kb, T=T, reverse=True)
    yf, yb = _chunk_scan(vr, fw, bw, B=B, S=S)
    h = _mix(xt, attn, yf, yb, r, kf, kb, vr, g, w["lng"], w["lnb"], w["rk"], w["aog"], w["wout"], w["jseg"],
             B=B, S=S, tm=tm)
    idx, gate = _route(h, w["g2"], w["wpq"], w["sk"], T=T, tm=_tile(T, 256))
    return h, idx, gate


def _peer(h, idx, gate, w, pu, pv, shape):
    T = h.shape[0]
    pk = P_HEADS * P_TOPK
    tc = _tile(T, PEER_CHUNK)
    outs = []
    for c in range(T // tc):
        sl = slice(c * tc, (c + 1) * tc)
        flat = idx[sl].reshape(tc * pk)
        gu = _gather_rows(pu, flat)
        gv = _gather_rows(pv, flat)
        outs.append(_experts(h[sl], w["g2"], gate[sl], gu, gv, tb=_tile(tc, EXPERT_TOKENS)))
    return jnp.concatenate(outs, axis=0).reshape(shape)


def _layer(xs, w, expert_u, expert_v):
    pu = _pack_rows(expert_u)
    pv = _pack_rows(expert_v)
    outs = []
    pending = None
    for x in xs:
        cur = _dense(x, w) + (x.shape,)
        if pending is not None:
            outs.append(_peer(*pending[:3], w, pu, pv, pending[3]))
        pending = cur
    outs.append(_peer(*pending[:3], w, pu, pv, pending[3]))
    return outs


GROUP_BATCH = 1


def kernel(x_prompt, x_sample, norm1_g, w_in, q_lat_g, w_uq, kv_lat_g, w_ukv, q_norm_g, k_norm_g, attn_out_g, mu_prev, mu_next, w0, w_up, a0, a_up, g_up, k_k, k_a, r_k, ln_x_g, ln_x_b, w_out, norm2_g, w_pq, sub_keys, expert_u, expert_v):
    assert x_prompt.shape[1] == x_sample.shape[1]
    S = x_prompt.shape[1]
    groups = []
    for x in (x_prompt, x_sample):
        gb = GROUP_BATCH if x.shape[0] % GROUP_BATCH == 0 else x.shape[0]
        groups.append([x[i:i + gb] for i in range(0, x.shape[0], gb)])
    xs = groups[0] + groups[1]
    for l in range(norm1_g.shape[0]):
        w = _prepare(norm1_g[l], w_in[l], q_lat_g[l], w_uq[l], kv_lat_g[l], w_ukv[l], q_norm_g[l], k_norm_g[l],
                     attn_out_g[l], mu_prev[l], mu_next[l], w0[l], w_up[l], a0[l], a_up[l], g_up[l], k_k[l], k_a[l],
                     r_k[l], ln_x_g[l], ln_x_b[l], w_out[l], norm2_g[l], w_pq[l], sub_keys[l], S)
        xs = _layer(xs, w, expert_u[l], expert_v[l])
    n0 = len(groups[0])
    join = lambda parts: parts[0] if len(parts) == 1 else jnp.concatenate(parts, axis=0)
    return join(xs[:n0]), join(xs[n0:])
```

```python
import functools
import math

import jax
import jax.numpy as jnp
from jax import lax
from jax.experimental import pallas as pl
from jax.experimental.pallas import tpu as pltpu
from jax.experimental.pallas import tpu_sc as plsc

D_MODEL = 1024
H_A = 8
NOPE_DIM = 64
ROPE_DIM = 32
QK_DIM = NOPE_DIM + ROPE_DIM
V_DIM = 64
Q_LORA = 384
KV_LORA = 256
ROPE_THETA = 10000.0
H_R = 8
HEAD_N = 64
C_R = H_R * HEAD_N
W_LORA = 64
A_LORA = 64
G_LORA = 128
GN_EPS = 64e-5
RWKV_IN = 3 * C_R + W_LORA + A_LORA + G_LORA
OFF_Q = 0
OFF_KV = OFF_Q + Q_LORA
OFF_KR = OFF_KV + KV_LORA
OFF_RWKV = OFF_KR + ROPE_DIM
N_KEYS = 128
N_EXPERTS = N_KEYS * N_KEYS
P_HEADS = 8
P_TOPK = 16
D_KEY = 256
HALF_KEY = D_KEY // 2
NORM_EPS = 1e-6

LANES = 128
HEAD_PAD = LANES
Z_COLS = RWKV_IN
IN_PAD = Q_LORA + KV_LORA + Z_COLS + 2 * LANES
VMEM_LIMIT = 56 * 1024 * 1024

f32 = jnp.float32
bf16 = jnp.bfloat16


def _rms(x, g):
    return x * lax.rsqrt(jnp.mean(x * x, axis=-1, keepdims=True) + NORM_EPS) * g


def _dot(a, b):
    return jnp.dot(a, b, preferred_element_type=f32)


def _dot_nt(a, b):
    return lax.dot_general(a, b, (((1,), (1,)), ((), ())), preferred_element_type=f32)


def _split_dot(x, j):
    hi = x.astype(bf16)
    lo = (x - hi.astype(f32)).astype(bf16)
    return _dot(hi, j) + _dot(lo, j)


def _inproj_kernel(x_ref, g1_ref, win_ref, qlg_ref, wqa_ref, wqb_ref, kvlg_ref, wkn_ref, wv_ref,
                   cos_ref, sin_ref, gq_ref, gk_ref,
                   q_ref, k_ref, v_ref, z_ref):
    x = x_ref[...]
    xn = _rms(x, g1_ref[...]).astype(bf16)
    proj = _dot(xn, win_ref[...])
    o_kv, o_z, o_ra, o_rb = Q_LORA, Q_LORA + KV_LORA, Q_LORA + KV_LORA + Z_COLS, Q_LORA + KV_LORA + Z_COLS + LANES
    z_ref[...] = proj[:, o_z:o_ra]
    cos = cos_ref[...]
    sin = sin_ref[...]

    ql = _rms(proj[:, :o_kv], qlg_ref[...]).astype(bf16)
    qa = _dot(ql, wqa_ref[...])
    qb = _dot(ql, wqb_ref[...])
    ga, gb = gq_ref[0:1, :], gq_ref[1:2, :]
    scale = QK_DIM ** -0.5 * math.log2(math.e)
    for h in range(H_A):
        a = qa[:, h * LANES:(h + 1) * LANES]
        b = qb[:, h * LANES:(h + 1) * LANES]
        s = lax.rsqrt(jnp.sum(a * a, axis=-1, keepdims=True) * (1.0 / QK_DIM) + NORM_EPS) * scale
        q_ref[0, h] = (s * (a * ga * cos + b * gb * sin)).astype(bf16)

    kvl = _rms(proj[:, o_kv:o_z], kvlg_ref[...]).astype(bf16)
    kn = _dot(kvl, wkn_ref[...])
    vv = _dot(kvl, wv_ref[...])
    one = (lax.broadcasted_iota(jnp.int32, (1, LANES), 1) == V_DIM).astype(f32)
    for h in range(H_A):
        v_ref[0, h] = (vv[:, h * LANES:(h + 1) * LANES] + one).astype(bf16)
    kra = proj[:, o_ra:o_rb]
    krb = proj[:, o_rb:o_rb + LANES]
    gkn, gka, gkb = gk_ref[0:1, :], gk_ref[1:2, :], gk_ref[2:3, :]
    kr = kra * gka * cos + krb * gkb * sin
    ssr = jnp.sum(kra * kra, axis=-1, keepdims=True)
    for h in range(H_A):
        a = kn[:, h * LANES:(h + 1) * LANES]
        s = lax.rsqrt((jnp.sum(a * a, axis=-1, keepdims=True) + ssr) * (1.0 / QK_DIM) + NORM_EPS)
        k_ref[0, h] = (s * (a * gkn + kr)).astype(bf16)


def _inproj(x, g1, win, qlg, wqa, wqb, kvlg, wkn, wv, cos, sin, gq, gk, *, B, S, tm):
    nS = S // tm
    const = lambda shape: pl.BlockSpec(shape, lambda b, s: (0,) * len(shape))
    return pl.pallas_call(
        _inproj_kernel,
        grid=(B, nS),
        in_specs=[
            pl.BlockSpec((tm, D_MODEL), lambda b, s: (b * nS + s, 0)),
            const((1, D_MODEL)), const(win.shape), const((1, Q_LORA)), const(wqa.shape), const(wqb.shape),
            const((1, KV_LORA)), const(wkn.shape), const(wv.shape),
            pl.BlockSpec((tm, LANES), lambda b, s: (s, 0)),
            pl.BlockSpec((tm, LANES), lambda b, s: (s, 0)),
            const(gq.shape), const(gk.shape),
        ],
        out_specs=[
            pl.BlockSpec((1, H_A, tm, HEAD_PAD), lambda b, s: (b, 0, s, 0)),
            pl.BlockSpec((1, H_A, tm, HEAD_PAD), lambda b, s: (b, 0, s, 0)),
            pl.BlockSpec((1, H_A, tm, LANES), lambda b, s: (b, 0, s, 0)),
            pl.BlockSpec((tm, Z_COLS), lambda b, s: (b * nS + s, 0)),
        ],
        out_shape=[
            jax.ShapeDtypeStruct((B, H_A, S, HEAD_PAD), bf16),
            jax.ShapeDtypeStruct((B, H_A, S, HEAD_PAD), bf16),
            jax.ShapeDtypeStruct((B, H_A, S, LANES), bf16),
            jax.ShapeDtypeStruct((B * S, Z_COLS), f32),
        ],
        compiler_params=pltpu.CompilerParams(
            dimension_semantics=("parallel", "parallel"), vmem_limit_bytes=VMEM_LIMIT),
        name="in_proj",
    )(x, g1, win, qlg, wqa, wqb, kvlg, wkn, wv, cos, sin, gq, gk)


ATTN_ROWS = 32
ATTN_UNROLL = 16


def _attn_kernel(q_ref, k_ref, v_ref, o_ref, s_ref, p_ref, m_ref, al_ref, acc_ref, *, tk, nk):
    tq = q_ref.shape[2]
    nlt = tk // LANES

    def scores(h, q, j, buf):
        off = pl.multiple_of(jnp.minimum(j, nk - 1) * tk, tk)
        s_ref[buf] = _dot_nt(q, k_ref[0, h, pl.ds(off, tk), :])

    def softmax(buf):
        def chunk(c, carry):
            rows = pl.ds(pl.multiple_of(c * ATTN_ROWS, ATTN_ROWS), ATTN_ROWS)
            tiles = [s_ref[buf, rows, lt * LANES:(lt + 1) * LANES] for lt in range(nlt)]
            best = functools.reduce(jnp.maximum, tiles)
            m_old = m_ref[rows, :]
            m_new = jnp.maximum(m_old, jnp.broadcast_to(jnp.max(best, axis=-1, keepdims=True), m_old.shape))
            for lt in range(nlt):
                p_ref[buf, rows, lt * LANES:(lt + 1) * LANES] = jnp.exp2(tiles[lt] - m_new).astype(bf16)
            al_ref[rows, :] = jnp.exp2(m_old - m_new)
            m_ref[rows, :] = m_new
            return carry

        lax.fori_loop(0, tq // ATTN_ROWS, chunk, 0, unroll=ATTN_UNROLL)

    def accumulate(h, j, buf):
        off = pl.multiple_of(j * tk, tk)
        acc_ref[...] = al_ref[...] * acc_ref[...] + _dot(p_ref[buf], v_ref[0, h, pl.ds(off, tk), :])

    outs = []
    for h in range(2):
        q = q_ref[0, h]
        m_ref[...] = jnp.full(m_ref.shape, -jnp.inf, f32)
        acc_ref[...] = jnp.zeros(acc_ref.shape, f32)
        scores(h, q, 0, 0)

        def body(jj, carry, h=h, q=q):
            j = 2 * jj
            scores(h, q, j + 1, 1)
            softmax(0)
            accumulate(h, j, 0)
            scores(h, q, j + 2, 0)
            softmax(1)
            accumulate(h, j + 1, 1)
            return carry

        lax.fori_loop(0, nk // 2, body, 0)
        acc = acc_ref[...]
        outs.append(acc / acc[:, V_DIM:V_DIM + 1])
    lane = lax.broadcasted_iota(jnp.int32, (tq, LANES), 1)
    o_ref[0, 0] = jnp.where(lane < V_DIM, outs[0], pltpu.roll(outs[1], V_DIM, 1))


def _attention(q, k, v, *, B, S, tq, tk):
    return pl.pallas_call(
        functools.partial(_attn_kernel, tk=tk, nk=S // tk),
        grid=(B, H_A // 2, S // tq),
        in_specs=[
            pl.BlockSpec((1, 2, tq, HEAD_PAD), lambda b, p, i: (b, p, i, 0)),
            pl.BlockSpec((1, 2, S, HEAD_PAD), lambda b, p, i: (b, p, 0, 0)),
            pl.BlockSpec((1, 2, S, LANES), lambda b, p, i: (b, p, 0, 0)),
        ],
        out_specs=pl.BlockSpec((1, 1, tq, LANES), lambda b, p, i: (b, p, i, 0)),
        out_shape=jax.ShapeDtypeStruct((B, H_A // 2, S, LANES), f32),
        scratch_shapes=[pltpu.VMEM((2, tq, tk), f32), pltpu.VMEM((2, tq, tk), bf16), pltpu.VMEM((tq, LANES), f32),
                        pltpu.VMEM((tq, LANES), f32), pltpu.VMEM((tq, LANES), f32)],
        compiler_params=pltpu.CompilerParams(
            dimension_semantics=("parallel", "parallel", "parallel"), vmem_limit_bytes=VMEM_LIMIT),
        name="attention",
    )(q, k, v)


def _softplus(x):
    return jnp.maximum(x, 0.0) + jnp.log1p(jnp.exp(-jnp.abs(x)))


def _prep_kernel(z_ref, zp_ref, zn_ref, mup_ref, mun_ref, wl_ref, w0_ref, a0_ref, gup_ref, kk_ref, ka_ref, jseg_ref,
                 r_ref, v_ref, nkk_ref, g_ref,
                 decf_ref, kkaf_ref, kf_ref, decb_ref, kkab_ref, kb_ref):
    si = pl.program_id(1)
    ns = pl.num_programs(1)
    z = z_ref[...]
    tm = z.shape[0]
    row = lax.broadcasted_iota(jnp.int32, (tm, 1), 0)
    zp_row = jnp.where(si == 0, 0.0, zp_ref[7:8, :])
    zn_row = jnp.where(si == ns - 1, 0.0, zn_ref[0:1, :])
    z_prev = jnp.where(row == 0, zp_row, pltpu.roll(z, 1, 0))
    z_next = jnp.where(row == tm - 1, zn_row, pltpu.roll(z, tm - 1, 0))
    zm = z + mup_ref[...] * (z_prev - z) + mun_ref[...] * (z_next - z)

    o1, o2, o3 = C_R, 2 * C_R, 3 * C_R
    o4 = o3 + W_LORA + A_LORA
    o5 = o4 + G_LORA
    r = zm[:, :o1]
    kr = zm[:, o1:o2]
    vr = zm[:, o2:o3]
    lw = zm[:, o3:o4]
    zg = zm[:, o4:o5]
    r_ref[...] = r
    v_ref[...] = vr
    g_ref[...] = _dot(jax.nn.sigmoid(zg).astype(bf16), gup_ref[...])

    kk = kr * kk_ref[...]
    ssk = _split_dot(kk * kk, jseg_ref[...])
    kk = kk / jnp.maximum(jnp.sqrt(ssk), 1e-12)
    nkk_ref[...] = -kk

    lane = lax.broadcasted_iota(jnp.int32, lw.shape, 1)
    lin = jnp.where(lane < W_LORA, jnp.tanh(lw), lw).astype(bf16)
    ka = ka_ref[...]
    for d, (dec_ref, kka_ref, kd_ref) in enumerate(((decf_ref, kkaf_ref, kf_ref), (decb_ref, kkab_ref, kb_ref))):
        lo = _dot(lin, wl_ref[d])
        w = -_softplus(-(w0_ref[d:d + 1, :] + lo[:, :C_R])) - 0.5
        dec_ref[...] = jnp.exp(-jnp.exp(w))
        a = jax.nn.sigmoid(a0_ref[d:d + 1, :] + lo[:, C_R:])
        kd_ref[...] = kr * (1.0 + (a - 1.0) * ka)
        kka_ref[...] = kk * a


def _rwkv_prep(z, mup, mun, wl, w0, a0, gup, kk, ka, jseg, *, B, S, tm):
    nS = S // tm
    T = B * S
    nb8 = tm // 8
    const = lambda shape: pl.BlockSpec(shape, lambda b, s: (0,) * len(shape))
    row_spec = pl.BlockSpec((tm, C_R), lambda b, s: (b * nS + s, 0))
    return pl.pallas_call(
        _prep_kernel,
        grid=(B, nS),
        in_specs=[
            pl.BlockSpec((tm, Z_COLS), lambda b, s: (b * nS + s, 0)),
            pl.BlockSpec((8, Z_COLS), lambda b, s: (jnp.maximum((b * nS + s) * nb8 - 1, 0), 0)),
            pl.BlockSpec((8, Z_COLS), lambda b, s: (jnp.minimum((b * nS + s + 1) * nb8, T // 8 - 1), 0)),
            const((1, Z_COLS)), const((1, Z_COLS)), const(wl.shape), const(w0.shape), const(a0.shape),
            const(gup.shape), const((1, C_R)), const((1, C_R)), const(jseg.shape),
        ],
        out_specs=[row_spec] * 10,
        out_shape=[jax.ShapeDtypeStruct((T, C_R), f32)] * 10,
        compiler_params=pltpu.CompilerParams(
            dimension_semantics=("parallel", "parallel"), vmem_limit_bytes=VMEM_LIMIT),
        name="rwkv_prep",
    )(z, z, z, mup, mun, wl, w0, a0, gup, kk, ka, jseg)


N_PAIRS = H_R // 2
CHUNK = 64


def _halves(x, low):
    zero = jnp.zeros_like(x)
    return jnp.concatenate([jnp.where(low, x, zero), jnp.where(low, zero, x)], axis=0)


def _chunks_kernel(r_ref, nkk_ref, dec_ref, kka_ref, k_ref, tri_ref,
                   at_ref, rt_ref, aak_ref, tt_ref, arb_ref, ark_ref, bt_ref, kt_ref, wc_ref, *, reverse):
    lane = lax.broadcasted_iota(jnp.int32, (CHUNK, LANES), 1)
    low = lane < HEAD_N
    jj = lax.broadcasted_iota(jnp.int32, (CHUNK, LANES), 0)
    ii = lane & (HEAD_N - 1)
    strict = (ii > jj) if reverse else (ii < jj)
    incl = (ii >= jj) if reverse else (ii <= jj)
    rr = lax.broadcasted_iota(jnp.int32, (LANES, LANES), 0)
    cc = lax.broadcasted_iota(jnp.int32, (LANES, LANES), 1)
    eye = (rr == cc).astype(f32)
    last = 0 if reverse else CHUNK - 1
    swap = lambda x: pltpu.roll(x, HEAD_N, 1)
    zero = jnp.zeros((CHUNK, LANES), f32)
    units = [(slice(q * CHUNK, (q + 1) * CHUNK), slice(p * LANES, (p + 1) * LANES))
             for q in range(r_ref.shape[0] // CHUNK) for p in range(N_PAIRS)]
    cums = []
    for rows, cols in units:
        lw = jnp.log(dec_ref[rows, cols])
        hi = lw.astype(bf16)
        r1 = lw - hi.astype(f32)
        mid = r1.astype(bf16)
        lo = (r1 - mid.astype(f32)).astype(bf16)
        cums.append((lw, _dot(tri_ref[...], jnp.concatenate([hi, mid, lo], axis=0))))
    outs = []
    for (rows, cols), (lw, cum) in zip(units, cums):
        w_in = jnp.exp(cum)
        w_ex = jnp.exp(cum - lw)
        w_inv = jnp.exp(-cum)
        wc = w_in[last:last + 1, :]
        at = nkk_ref[rows, cols] * w_ex
        rt = r_ref[rows, cols] * w_in
        bt = kka_ref[rows, cols] * w_inv
        kt = k_ref[rows, cols] * w_inv
        at_ref[rows, cols] = at.astype(bf16)
        rt_ref[rows, cols] = rt.astype(bf16)
        bt_ref[rows, cols] = (bt * wc).astype(bf16)
        kt_ref[rows, cols] = (kt * wc).astype(bf16)
        wc_ref[rows, cols] = jnp.broadcast_to(wc, (CHUNK, LANES))
        lhs = jnp.concatenate([_halves(at, low), _halves(rt, low)], axis=0).astype(bf16)
        rhs = jnp.concatenate([bt, kt], axis=0).astype(bf16)
        outs.append(_dot_nt(lhs, rhs))
    pws, ts = [], []
    for (rows, cols), o in zip(units, outs):
        o0, o1, o2, o3 = (o[q * CHUNK:(q + 1) * CHUNK, :] for q in range(4))
        aab = jnp.where(strict, jnp.where(low, o0, swap(o1)), zero)
        aak_ref[rows, cols] = jnp.where(strict, jnp.where(low, swap(o0), o1), zero).astype(bf16)
        arb_ref[rows, cols] = jnp.where(incl, jnp.where(low, o2, swap(o3)), zero).astype(bf16)
        ark_ref[rows, cols] = jnp.where(incl, jnp.where(low, swap(o2), o3), zero).astype(bf16)
        pws.append(_halves(aab, low))
        ts.append(eye + pws[-1])
    for _ in range(int(math.log2(CHUNK)) - 1):
        pbs = [pw.astype(bf16) for pw in pws]
        pws = [_dot(pb, pb) for pb in pbs]
        ts = [t + _dot(t.astype(bf16), pw.astype(bf16)) for t, pw in zip(ts, pws)]
    for (rows, cols), t in zip(units, ts):
        tt_ref[rows, cols] = jnp.where(low, t[:CHUNK, :], t[CHUNK:, :]).astype(bf16)


CHUNKS_PER_STEP = 4


def _wkv_chunks(r, nkk, dec, kka, k, *, T, reverse):
    tri_i = jnp.arange(CHUNK)
    tri = (tri_i[None, :] >= tri_i[:, None]) if reverse else (tri_i[None, :] <= tri_i[:, None])
    tri3 = jnp.tile(tri.astype(bf16), (1, 3))
    rows = CHUNK * CHUNKS_PER_STEP
    assert T % rows == 0
    row = pl.BlockSpec((rows, C_R), lambda i: (i, 0))
    return pl.pallas_call(
        functools.partial(_chunks_kernel, reverse=reverse),
        grid=(T // rows,),
        in_specs=[row] * 5 + [pl.BlockSpec(tri3.shape, lambda i: (0, 0))],
        out_specs=[row] * 9,
        out_shape=[jax.ShapeDtypeStruct((T, C_R), bf16)] * 8 + [jax.ShapeDtypeStruct((T, C_R), f32)],
        compiler_params=pltpu.CompilerParams(dimension_semantics=("parallel",), vmem_limit_bytes=VMEM_LIMIT),
        name="wkv_chunks",
    )(r, nkk, dec, kka, k, tri3)


def _chunk_scan_kernel(vf_ref, vb_ref, *refs, nb):
    vs = (vf_ref, vb_ref)
    ins = (refs[0:9], refs[9:18])
    ys = refs[18:20]
    st_ref = refs[20]

    @pl.when(pl.program_id(1) == 0)
    def _():
        st_ref[...] = jnp.zeros_like(st_ref)

    low = lax.broadcasted_iota(jnp.int32, (CHUNK, LANES), 1) < HEAD_N
    rr = lax.broadcasted_iota(jnp.int32, (LANES, LANES), 0) < HEAD_N
    cc = lax.broadcasted_iota(jnp.int32, (LANES, LANES), 1) < HEAD_N
    same_head = rr == cc
    chains = [(bb, d, p) for bb in range(nb) for d in range(2) for p in range(N_PAIRS)]
    part = lambda bb, d, p, which: ins[d][which][bb, :, p * LANES:(p + 1) * LANES]
    s0s = [st_ref[(bb * 2 + d) * N_PAIRS + p] for bb, d, p in chains]
    vals = [vs[d][bb, :, p * LANES:(p + 1) * LANES] for bb, d, p in chains]
    vbds = [_halves(v, low).astype(bf16) for v in vals]
    gs = [_dot(jnp.concatenate([part(*ch, 0), part(*ch, 1)], axis=0), s0.astype(bf16))
          for ch, s0 in zip(chains, s0s)]
    xs = [g[:CHUNK, :] + _dot(part(*ch, 2), vbd) for ch, g, vbd in zip(chains, gs, vbds)]
    sas = [_dot(part(*ch, 3), _halves(x, low).astype(bf16)) for ch, x in zip(chains, xs)]
    for (bb, d, p), g, sa, vbd in zip(chains, gs, sas, vbds):
        ys[d][bb, :, p * LANES:(p + 1) * LANES] = (
            g[CHUNK:, :] + _dot(part(bb, d, p, 4), _halves(sa, low).astype(bf16)) + _dot(part(bb, d, p, 5), vbd))
    for (bb, d, p), s0, sa, v in zip(chains, s0s, sas, vals):
        upd = lax.dot_general(
            jnp.concatenate([part(bb, d, p, 6), part(bb, d, p, 7)], axis=0),
            jnp.concatenate([sa, v], axis=0).astype(bf16),
            (((0,), (0,)), ((), ())), preferred_element_type=f32)
        wc = part(bb, d, p, 8)
        wcol = jnp.concatenate([wc, wc], axis=0).T
        st_ref[(bb * 2 + d) * N_PAIRS + p] = jnp.where(same_head, wcol * s0 + upd, jnp.zeros_like(s0))


CHUNK_SCAN_BATCH = 4


def _chunk_scan(v, fwd_parts, bwd_parts, *, B, S):
    nC = S // CHUNK
    nb = CHUNK_SCAN_BATCH if B % CHUNK_SCAN_BATCH == 0 else 1
    as3 = lambda a: a.reshape(B, S, C_R)
    fwd = pl.BlockSpec((nb, CHUNK, C_R), lambda b, i: (b, i, 0))
    bwd = pl.BlockSpec((nb, CHUNK, C_R), lambda b, i: (b, nC - 1 - i, 0))
    yf, yb = pl.pallas_call(
        functools.partial(_chunk_scan_kernel, nb=nb),
        grid=(B // nb, nC),
        in_specs=[fwd, bwd] + [fwd] * 9 + [bwd] * 9,
        out_specs=[fwd, bwd],
        out_shape=[jax.ShapeDtypeStruct((B, S, C_R), f32)] * 2,
        scratch_shapes=[pltpu.VMEM((nb * 2 * N_PAIRS, LANES, LANES), f32)],
        compiler_params=pltpu.CompilerParams(
            dimension_semantics=("parallel", "arbitrary"), vmem_limit_bytes=VMEM_LIMIT),
        name="wkv_scan",
    )(as3(v), as3(v), *[as3(a) for a in fwd_parts], *[as3(a) for a in bwd_parts])
    return yf.reshape(B * S, C_R), yb.reshape(B * S, C_R)

def _mix_kernel(x_ref, attn_ref, yf_ref, yb_ref, r_ref, kf_ref, kb_ref, v_ref, g_ref,
                lng_ref, lnb_ref, rk_ref, aog_ref, wout_ref, jseg_ref, h_ref):
    jseg = jseg_ref[...]
    y = yf_ref[...] + yb_ref[...]
    mu = _split_dot(y, jseg) * (1.0 / HEAD_N)
    dlt = y - mu
    var = _split_dot(dlt * dlt, jseg) * (1.0 / HEAD_N)
    yn = dlt * lax.rsqrt(var + GN_EPS) * lng_ref[...] + lnb_ref[...]
    r = r_ref[...]
    k_mean = 0.5 * (kf_ref[...] + kb_ref[...])
    bonus = _split_dot(r * k_mean * rk_ref[...], jseg) * v_ref[...]
    rw = ((yn + bonus) * g_ref[...]).astype(bf16)
    attn = jnp.concatenate([attn_ref[0, p] for p in range(H_A // 2)], axis=-1)
    an = _rms(attn, aog_ref[...]).astype(bf16)
    ha = H_A * V_DIM
    h_ref[...] = x_ref[...] + _dot(an, wout_ref[0:ha, :]) + _dot(rw, wout_ref[ha:, :])


def _mix(x, attn, yf, yb, r, kf, kb, v, g, lng, lnb, rk, aog, wout, jseg, *, B, S, tm):
    nS = S // tm
    const = lambda shape: pl.BlockSpec(shape, lambda b, s: (0,) * len(shape))
    row = lambda c: pl.BlockSpec((tm, c), lambda b, s: (b * nS + s, 0))
    return pl.pallas_call(
        _mix_kernel,
        grid=(B, nS),
        in_specs=[row(D_MODEL), pl.BlockSpec((1, H_A // 2, tm, LANES), lambda b, s: (b, 0, s, 0))]
                 + [row(C_R)] * 7
                 + [const((1, C_R))] * 3 + [const((1, H_A * V_DIM)), const(wout.shape), const(jseg.shape)],
        out_specs=row(D_MODEL),
        out_shape=jax.ShapeDtypeStruct((B * S, D_MODEL), f32),
        compiler_params=pltpu.CompilerParams(
            dimension_semantics=("parallel", "parallel"), vmem_limit_bytes=VMEM_LIMIT),
        name="mix",
    )(x, attn, yf, yb, r, kf, kb, v, g, lng, lnb, rk, aog, wout, jseg)


def _take_max(s, ids):
    m = jnp.max(s, axis=0, keepdims=True)
    pick = jnp.min(jnp.where(s == m, ids, jnp.int32(2 ** 30)), axis=0, keepdims=True)
    return m, pick, jnp.where(ids == pick, -jnp.inf, s)


CAND_COLS = tuple(P_TOPK // (a + 1) for a in range(P_TOPK))
N_CAND = sum(CAND_COLS)
N_CAND_PAD = -(-N_CAND // 8) * 8


def _route_kernel(h_ref, g2_ref, wpq_ref, sk_ref, pos_ref, idx_ref, gate_ref, v1_ref, i1_ref, v2_ref, i2_ref, cs_ref,
                  ci_ref, bs_ref, bi_ref):
    tm = h_ref.shape[0]
    hn = _rms(h_ref[...], g2_ref[...]).astype(bf16)
    pq = _dot(hn, wpq_ref[...])
    key_ids = lax.broadcasted_iota(jnp.int32, (N_KEYS, tm), 0)
    pos_ids = pos_ref[...]
    for p in range(P_HEADS):
        for side, (vs_ref, is_ref) in enumerate(((v1_ref, i1_ref), (v2_ref, i2_ref))):
            qh = pq[:, p * D_KEY + side * HALF_KEY:p * D_KEY + (side + 1) * HALF_KEY].astype(bf16)
            s = _dot_nt(sk_ref[side], qh)
            for j in range(P_TOPK):
                m, pick, s = _take_max(s, key_ids)
                vs_ref[j:j + 1, :] = m
                is_ref[j:j + 1, :] = pick
        v2 = v2_ref[...]
        i2 = i2_ref[...]
        off = 0
        for a, nb in enumerate(CAND_COLS):
            cs_ref[off:off + nb, :] = v1_ref[a:a + 1, :] + v2[:nb, :]
            ci_ref[off:off + nb, :] = i1_ref[a:a + 1, :] * N_KEYS + i2[:nb, :]
            off += nb
        cs_ref[N_CAND:, :] = jnp.full((N_CAND_PAD - N_CAND, tm), -jnp.inf, f32)
        ci_ref[N_CAND:, :] = jnp.zeros((N_CAND_PAD - N_CAND, tm), jnp.int32)
        s = cs_ref[...]
        ci = ci_ref[...]
        for j in range(P_TOPK):
            m, pick, s = _take_max(s, pos_ids)
            bs_ref[j:j + 1, :] = m
            bi_ref[p * P_TOPK + j:p * P_TOPK + j + 1, :] = jnp.max(
                jnp.where(pos_ids == pick, ci, -1), axis=0, keepdims=True)
        bs = bs_ref[...]
        e = jnp.exp(bs - bs[0:1, :])
        gate_ref[:, p * P_TOPK:(p + 1) * P_TOPK] = (e / jnp.sum(e, axis=0, keepdims=True)).T
    idx_ref[...] = bi_ref[...].T


def _route(h, g2, wpq, sk, *, T, tm):
    const = lambda shape: pl.BlockSpec(shape, lambda i: (0,) * len(shape))
    pk = P_HEADS * P_TOPK
    flat = [a * P_TOPK + b for a, nb in enumerate(CAND_COLS) for b in range(nb)] + [2 ** 20] * (N_CAND_PAD - N_CAND)
    pos = jnp.broadcast_to(jnp.array(flat, jnp.int32)[:, None], (N_CAND_PAD, tm))
    return pl.pallas_call(
        _route_kernel,
        grid=(T // tm,),
        in_specs=[pl.BlockSpec((tm, D_MODEL), lambda i: (i, 0)), const((1, D_MODEL)), const(wpq.shape),
                  const(sk.shape), const((N_CAND_PAD, tm))],
        out_specs=[pl.BlockSpec((tm, pk), lambda i: (i, 0))] * 2,
        out_shape=[jax.ShapeDtypeStruct((T, pk), jnp.int32), jax.ShapeDtypeStruct((T, pk), f32)],
        scratch_shapes=[pltpu.VMEM((P_TOPK, tm), f32), pltpu.VMEM((P_TOPK, tm), jnp.int32),
                        pltpu.VMEM((P_TOPK, tm), f32), pltpu.VMEM((P_TOPK, tm), jnp.int32),
                        pltpu.VMEM((N_CAND_PAD, tm), f32), pltpu.VMEM((N_CAND_PAD, tm), jnp.int32),
                        pltpu.VMEM((P_TOPK, tm), f32), pltpu.VMEM((pk, tm), jnp.int32)],
        compiler_params=pltpu.CompilerParams(dimension_semantics=("parallel",), vmem_limit_bytes=VMEM_LIMIT),
        name="route",
    )(h, g2, wpq, sk, pos)


GATHER_WINDOW = 64
SC_CORES = 2
SC_SUBCORES = 16
SC_WORKERS = SC_CORES * SC_SUBCORES


def _gather_rows(table_a, table_b, idx):
    n = idx.shape[0]
    d = table_a.shape[1]
    assert table_b.shape == table_a.shape and table_b.dtype == table_a.dtype
    win = GATHER_WINDOW
    n_it = n // (SC_WORKERS * win)
    assert n_it * SC_WORKERS * win == n and n_it % 2 == 0, (n, n_it)
    mesh = plsc.VectorSubcoreMesh(core_axis_name="core", subcore_axis_name="subcore")
    rows = jax.ShapeDtypeStruct((n, d), table_a.dtype)

    @functools.partial(
        pl.kernel, out_type=(rows, rows), mesh=mesh,
        scratch_types=[pltpu.VMEM((n_it, win), jnp.int32), pltpu.VMEM((2, win, d), table_a.dtype),
                       pltpu.SemaphoreType.DMA, pltpu.SemaphoreType.DMA,
                       pltpu.SemaphoreType.DMA, pltpu.SemaphoreType.DMA])
    def gather(taba_hbm, tabb_hbm, idx_hbm, outa_hbm, outb_hbm, idx_v, rows_v, gsem0, gsem1, wsem0, wsem1):
        wid = lax.axis_index("subcore") * SC_CORES + lax.axis_index("core")
        base = wid * (n_it * win)
        gsem = (gsem0, gsem1)
        wsem = (wsem0, wsem1)
        pltpu.sync_copy(idx_hbm.at[wid], idx_v)

        for tab_hbm, out_hbm in ((taba_hbm, outa_hbm), (tabb_hbm, outb_hbm)):
            def fetch(j, b):
                return pltpu.make_async_copy(tab_hbm.at[idx_v.at[j]], rows_v.at[b], gsem[b])

            def flush(j, b):
                return pltpu.make_async_copy(rows_v.at[b], out_hbm.at[pl.ds(base + j * win, win)], wsem[b])

            fetch(0, 0).start()

            @pl.loop(0, n_it, step=2)
            def _(j0):
                for b in range(2):
                    j = j0 + b

                    @pl.when(j >= 1)
                    def _():
                        flush(j - 1, 1 - b).wait()

                    @pl.when(j + 1 < n_it)
                    def _():
                        fetch(j + 1, 1 - b).start()

                    fetch(j, b).wait()
                    flush(j, b).start()

            flush(n_it - 1, 1).wait()

    return gather(table_a, table_b, idx.reshape(SC_WORKERS, n_it, win))


HALF_D = D_MODEL // 2


def _pack_rows(table):
    bits = lax.bitcast_convert_type(table.astype(bf16), jnp.uint16).astype(jnp.uint32)
    return bits[:, :HALF_D] | (bits[:, HALF_D:] << 16)


def _unpack_rows(words):
    lo = lax.bitcast_convert_type(words << 16, f32).astype(bf16)
    hi = lax.bitcast_convert_type(words & jnp.uint32(0xFFFF0000), f32).astype(bf16)
    return lo, hi


def _expert_kernel(h_ref, g2_ref, gate_ref, gu_ref, gv_ref, o_ref):
    h = h_ref[...]
    tb = h.shape[0]
    pk = P_HEADS * P_TOPK
    hn = _rms(h, g2_ref[...]).astype(bf16)
    row = lax.broadcasted_iota(jnp.int32, (8, pk), 0)
    for grp in range(tb // 8):
        rows = slice(grp * 8, (grp + 1) * 8)
        hn8 = hn[rows, :]
        act = jnp.zeros((8, pk), f32)
        for j in range(8):
            t = grp * 8 + j
            ulo, uhi = _unpack_rows(gu_ref[t * pk:(t + 1) * pk, :])
            act = jnp.where(row == j, _dot_nt(hn8[:, :HALF_D], ulo) + _dot_nt(hn8[:, HALF_D:], uhi), act)
        gelu = 0.5 * act * (1.0 + lax.erf(act * (2.0 ** -0.5)))
        w = (gate_ref[rows, :] * gelu).astype(bf16)
        out_lo = jnp.zeros((8, HALF_D), f32)
        out_hi = jnp.zeros((8, HALF_D), f32)
        for j in range(8):
            t = grp * 8 + j
            vlo, vhi = _unpack_rows(gv_ref[t * pk:(t + 1) * pk, :])
            wj = jnp.where(row == j, w, jnp.zeros_like(w))
            out_lo = out_lo + _dot(wj, vlo)
            out_hi = out_hi + _dot(wj, vhi)
        o_ref[rows, :HALF_D] = h[rows, :HALF_D] + out_lo
        o_ref[rows, HALF_D:] = h[rows, HALF_D:] + out_hi


def _experts(h, g2, gate, gu, gv, *, tb):
    n = h.shape[0]
    pk = P_HEADS * P_TOPK
    const = lambda shape: pl.BlockSpec(shape, lambda i: (0,) * len(shape))
    return pl.pallas_call(
        _expert_kernel,
        grid=(n // tb,),
        in_specs=[pl.BlockSpec((tb, D_MODEL), lambda i: (i, 0)), const((1, D_MODEL)),
                  pl.BlockSpec((tb, pk), lambda i: (i, 0)),
                  pl.BlockSpec((tb * pk, HALF_D), lambda i: (i, 0)),
                  pl.BlockSpec((tb * pk, HALF_D), lambda i: (i, 0))],
        out_specs=pl.BlockSpec((tb, D_MODEL), lambda i: (i, 0)),
        out_shape=jax.ShapeDtypeStruct((n, D_MODEL), f32),
        compiler_params=pltpu.CompilerParams(dimension_semantics=("parallel",), vmem_limit_bytes=VMEM_LIMIT),
        name="experts",
    )(h, g2, gate, gu, gv)


def _place(cols, width, offset):
    return jnp.pad(cols, ((0, 0), (offset, width - offset - cols.shape[1])))


def _rope_partner():
    half = ROPE_DIM // 2
    return jnp.concatenate([jnp.arange(half, ROPE_DIM), jnp.arange(0, half)])


def _prepare(norm1_g, w_in, q_lat_g, w_uq, kv_lat_g, w_ukv, q_norm_g, k_norm_g, attn_out_g, mu_prev, mu_next,
             w0, w_up, a0, a_up, g_up, k_k, k_a, r_k, ln_x_g, ln_x_b, w_out, norm2_g, w_pq, sub_keys, S):
    partner = _rope_partner()
    wz = w_in[:, OFF_RWKV:]
    wkr = w_in[:, OFF_KR:OFF_RWKV]
    win = jnp.concatenate([
        w_in[:, OFF_Q:OFF_KV], w_in[:, OFF_KV:OFF_KR], wz,
        _place(wkr, LANES, NOPE_DIM), _place(wkr[:, partner], LANES, NOPE_DIM)], axis=1).astype(bf16)

    wq = w_uq.reshape(Q_LORA, H_A, QK_DIM)
    wqa = jnp.pad(wq, ((0, 0), (0, 0), (0, HEAD_PAD - QK_DIM))).reshape(Q_LORA, H_A * HEAD_PAD).astype(bf16)
    wqb = jnp.pad(wq[:, :, NOPE_DIM:][:, :, partner],
                  ((0, 0), (0, 0), (NOPE_DIM, HEAD_PAD - QK_DIM))).reshape(Q_LORA, H_A * HEAD_PAD).astype(bf16)
    wkv = w_ukv.reshape(KV_LORA, H_A, NOPE_DIM + V_DIM)
    wkn = jnp.pad(wkv[:, :, :NOPE_DIM], ((0, 0), (0, 0), (0, HEAD_PAD - NOPE_DIM))).reshape(
        KV_LORA, H_A * HEAD_PAD).astype(bf16)
    wv = jnp.pad(wkv[:, :, NOPE_DIM:], ((0, 0), (0, 0), (0, LANES - V_DIM))).reshape(
        KV_LORA, H_A * LANES).astype(bf16)

    def gain_rows(g):
        ga = _place(g[None, :], LANES, 0)
        gb = _place(g[None, NOPE_DIM:][:, partner], LANES, NOPE_DIM)
        return ga, gb

    gqa, gqb = gain_rows(q_norm_g)
    gq = jnp.concatenate([gqa, gqb, jnp.zeros((6, LANES), f32)], axis=0)
    gkn = _place(k_norm_g[None, :NOPE_DIM], LANES, 0)
    gka = _place(k_norm_g[None, NOPE_DIM:], LANES, NOPE_DIM)
    gkb = _place(k_norm_g[None, NOPE_DIM:][:, partner], LANES, NOPE_DIM)
    gk = jnp.concatenate([gkn, gka, gkb, jnp.zeros((5, LANES), f32)], axis=0)

    half = ROPE_DIM // 2
    inv = 1.0 / (ROPE_THETA ** (jnp.arange(half, dtype=f32) / half))
    ang = jnp.arange(S, dtype=f32)[:, None] * inv[None, :]
    c, s = jnp.cos(ang), jnp.sin(ang)
    cos = jnp.concatenate([jnp.ones((S, NOPE_DIM), f32), c, c, jnp.zeros((S, HEAD_PAD - QK_DIM), f32)], axis=1)
    sin = jnp.concatenate([jnp.zeros((S, NOPE_DIM), f32), -s, s, jnp.zeros((S, HEAD_PAD - QK_DIM), f32)], axis=1)

    zeros = jnp.zeros((W_LORA, C_R), f32)
    wl = jnp.stack([jnp.concatenate([jnp.concatenate([w_up[d], zeros], axis=1),
                                     jnp.concatenate([zeros, a_up[d]], axis=1)], axis=0) for d in range(2)]).astype(bf16)
    seg = jnp.arange(C_R) // HEAD_N
    jseg = (seg[:, None] == seg[None, :]).astype(bf16)

    return dict(
        g1=norm1_g[None, :], win=win, qlg=q_lat_g[None, :], wqa=wqa, wqb=wqb, kvlg=kv_lat_g[None, :], wkn=wkn, wv=wv,
        cos=cos, sin=sin, gq=gq, gk=gk,
        mup=mu_prev[None, :], mun=mu_next[None, :], wl=wl, w0=w0, a0=a0,
        gup=g_up.astype(bf16), kk=k_k[None, :], ka=k_a[None, :], jseg=jseg,
        lng=ln_x_g[None, :], lnb=ln_x_b[None, :], rk=r_k[None, :], aog=attn_out_g[None, :],
        wout=w_out.astype(bf16), g2=norm2_g[None, :], wpq=w_pq.astype(bf16), sk=sub_keys.astype(bf16),
    )


def _tile(n, pref):
    t = min(n, pref)
    assert n % t == 0, (n, t)
    return t


PEER_CHUNK = 4096
EXPERT_TOKENS = 32


def _dense(x, w):
    B, S, _ = x.shape
    T = B * S
    xt = x.reshape(T, D_MODEL)
    tm = _tile(S, 256)
    q, k, v, z = _inproj(xt, w["g1"], w["win"], w["qlg"], w["wqa"], w["wqb"], w["kvlg"], w["wkn"], w["wv"],
                         w["cos"], w["sin"], w["gq"], w["gk"], B=B, S=S, tm=tm)
    attn = _attention(q, k, v, B=B, S=S, tq=_tile(S, 512), tk=_tile(S // 2, 512))
    r, vr, nkk, g, decf, kkaf, kf, decb, kkab, kb = _rwkv_prep(
        z, w["mup"], w["mun"], w["wl"], w["w0"], w["a0"], w["gup"], w["kk"], w["ka"], w["jseg"], B=B, S=S, tm=tm)
    fw = _wkv_chunks(r, nkk, decf, kkaf, kf, T=T, reverse=False)
    bw = _wkv_chunks(r, nkk, decb, kkab, kb, T=T, reverse=True)
    yf, yb = _chunk_scan(vr, fw, bw, B=B, S=S)
    h = _mix(xt, attn, yf, yb, r, kf, kb, vr, g, w["lng"], w["lnb"], w["rk"], w["aog"], w["wout"], w["jseg"],
             B=B, S=S, tm=tm)
    idx, gate = _route(h, w["g2"], w["wpq"], w["sk"], T=T, tm=_tile(T, 256))
    return h, idx, gate


def _peer(h, idx, gate, w, pu, pv, shape):
    T = h.shape[0]
    pk = P_HEADS * P_TOPK
    tc = _tile(T, PEER_CHUNK)
    outs = []
    for c in range(T // tc):
        sl = slice(c * tc, (c + 1) * tc)
        flat = idx[sl].reshape(tc * pk)
        gu, gv = _gather_rows(pu, pv, flat)
        outs.append(_experts(h[sl], w["g2"], gate[sl], gu, gv, tb=_tile(tc, EXPERT_TOKENS)))
    return jnp.concatenate(outs, axis=0).reshape(shape)


def _layer(xs, w, expert_u, expert_v):
    pu = _pack_rows(expert_u)
    pv = _pack_rows(expert_v)
    outs = []
    pending = None
    for x in xs:
        cur = _dense(x, w) + (x.shape,)
        if pending is not None:
            outs.append(_peer(*pending[:3], w, pu, pv, pending[3]))
        pending = cur
    outs.append(_peer(*pending[:3], w, pu, pv, pending[3]))
    return outs


GROUP_BATCH = 1


def kernel(x_prompt, x_sample, norm1_g, w_in, q_lat_g, w_uq, kv_lat_g, w_ukv, q_norm_g, k_norm_g, attn_out_g, mu_prev, mu_next, w0, w_up, a0, a_up, g_up, k_k, k_a, r_k, ln_x_g, ln_x_b, w_out, norm2_g, w_pq, sub_keys, expert_u, expert_v):
    assert x_prompt.shape[1] == x_sample.shape[1]
    S = x_prompt.shape[1]
    groups = []
    for x in (x_prompt, x_sample):
        gb = GROUP_BATCH if x.shape[0] % GROUP_BATCH == 0 else x.shape[0]
        groups.append([x[i:i + gb] for i in range(0, x.shape[0], gb)])
    xs = groups[0] + groups[1]
    for l in range(norm1_g.shape[0]):
        w = _prepare(norm1_g[l], w_in[l], q_lat_g[l], w_uq[l], kv_lat_g[l], w_ukv[l], q_norm_g[l], k_norm_g[l],
                     attn_out_g[l], mu_prev[l], mu_next[l], w0[l], w_up[l], a0[l], a_up[l], g_up[l], k_k[l], k_a[l],
                     r_k[l], ln_x_g[l], ln_x_b[l], w_out[l], norm2_g[l], w_pq[l], sub_keys[l], S)
        xs = _layer(xs, w, expert_u[l], expert_v[l])
    n0 = len(groups[0])
    join = lambda parts: parts[0] if len(parts) == 1 else jnp.concatenate(parts, axis=0)
    return join(xs[:n0]), join(xs[n0:])
```

```python
import functools
import math

import jax
import jax.numpy as jnp
from jax import lax
from jax.experimental import pallas as pl
from jax.experimental.pallas import tpu as pltpu
from jax.experimental.pallas import tpu_sc as plsc

D_MODEL = 1024
H_A = 8
NOPE_DIM = 64
ROPE_DIM = 32
QK_DIM = NOPE_DIM + ROPE_DIM
V_DIM = 64
Q_LORA = 384
KV_LORA = 256
ROPE_THETA = 10000.0
H_R = 8
HEAD_N = 64
C_R = H_R * HEAD_N
W_LORA = 64
A_LORA = 64
G_LORA = 128
GN_EPS = 64e-5
RWKV_IN = 3 * C_R + W_LORA + A_LORA + G_LORA
OFF_Q = 0
OFF_KV = OFF_Q + Q_LORA
OFF_KR = OFF_KV + KV_LORA
OFF_RWKV = OFF_KR + ROPE_DIM
N_KEYS = 128
N_EXPERTS = N_KEYS * N_KEYS
P_HEADS = 8
P_TOPK = 16
D_KEY = 256
HALF_KEY = D_KEY // 2
NORM_EPS = 1e-6

LANES = 128
HEAD_PAD = LANES
Z_COLS = RWKV_IN
IN_PAD = Q_LORA + KV_LORA + Z_COLS + 2 * LANES
VMEM_LIMIT = 56 * 1024 * 1024

f32 = jnp.float32
bf16 = jnp.bfloat16


def _rms(x, g):
    return x * lax.rsqrt(jnp.mean(x * x, axis=-1, keepdims=True) + NORM_EPS) * g


def _dot(a, b):
    return jnp.dot(a, b, preferred_element_type=f32)


def _dot_nt(a, b):
    return lax.dot_general(a, b, (((1,), (1,)), ((), ())), preferred_element_type=f32)


def _split_dot(x, j):
    hi = x.astype(bf16)
    lo = (x - hi.astype(f32)).astype(bf16)
    return _dot(hi, j) + _dot(lo, j)


def _inproj_kernel(x_ref, g1_ref, win_ref, qlg_ref, wqa_ref, wqb_ref, kvlg_ref, wkn_ref, wv_ref,
                   cos_ref, sin_ref, gq_ref, gk_ref,
                   q_ref, k_ref, v_ref, z_ref):
    x = x_ref[...]
    xn = _rms(x, g1_ref[...]).astype(bf16)
    proj = _dot(xn, win_ref[...])
    o_kv, o_z, o_ra, o_rb = Q_LORA, Q_LORA + KV_LORA, Q_LORA + KV_LORA + Z_COLS, Q_LORA + KV_LORA + Z_COLS + LANES
    z_ref[...] = proj[:, o_z:o_ra]
    cos = cos_ref[...]
    sin = sin_ref[...]

    ql = _rms(proj[:, :o_kv], qlg_ref[...]).astype(bf16)
    qa = _dot(ql, wqa_ref[...])
    qb = _dot(ql, wqb_ref[...])
    ga, gb = gq_ref[0:1, :], gq_ref[1:2, :]
    scale = QK_DIM ** -0.5 * math.log2(math.e)
    for h in range(H_A):
        a = qa[:, h * LANES:(h + 1) * LANES]
        b = qb[:, h * LANES:(h + 1) * LANES]
        s = lax.rsqrt(jnp.sum(a * a, axis=-1, keepdims=True) * (1.0 / QK_DIM) + NORM_EPS) * scale
        q_ref[0, h] = (s * (a * ga * cos + b * gb * sin)).astype(bf16)

    kvl = _rms(proj[:, o_kv:o_z], kvlg_ref[...]).astype(bf16)
    kn = _dot(kvl, wkn_ref[...])
    vv = _dot(kvl, wv_ref[...])
    one = (lax.broadcasted_iota(jnp.int32, (1, LANES), 1) == V_DIM).astype(f32)
    for h in range(H_A):
        v_ref[0, h] = (vv[:, h * LANES:(h + 1) * LANES] + one).astype(bf16)
    kra = proj[:, o_ra:o_rb]
    krb = proj[:, o_rb:o_rb + LANES]
    gkn, gka, gkb = gk_ref[0:1, :], gk_ref[1:2, :], gk_ref[2:3, :]
    kr = kra * gka * cos + krb * gkb * sin
    ssr = jnp.sum(kra * kra, axis=-1, keepdims=True)
    for h in range(H_A):
        a = kn[:, h * LANES:(h + 1) * LANES]
        s = lax.rsqrt((jnp.sum(a * a, axis=-1, keepdims=True) + ssr) * (1.0 / QK_DIM) + NORM_EPS)
        k_ref[0, h] = (s * (a * gkn + kr)).astype(bf16)


def _inproj(x, g1, win, qlg, wqa, wqb, kvlg, wkn, wv, cos, sin, gq, gk, *, B, S, tm):
    nS = S // tm
    const = lambda shape: pl.BlockSpec(shape, lambda b, s: (0,) * len(shape))
    return pl.pallas_call(
        _inproj_kernel,
        grid=(B, nS),
        in_specs=[
            pl.BlockSpec((tm, D_MODEL), lambda b, s: (b * nS + s, 0)),
            const((1, D_MODEL)), const(win.shape), const((1, Q_LORA)), const(wqa.shape), const(wqb.shape),
            const((1, KV_LORA)), const(wkn.shape), const(wv.shape),
            pl.BlockSpec((tm, LANES), lambda b, s: (s, 0)),
            pl.BlockSpec((tm, LANES), lambda b, s: (s, 0)),
            const(gq.shape), const(gk.shape),
        ],
        out_specs=[
            pl.BlockSpec((1, H_A, tm, HEAD_PAD), lambda b, s: (b, 0, s, 0)),
            pl.BlockSpec((1, H_A, tm, HEAD_PAD), lambda b, s: (b, 0, s, 0)),
            pl.BlockSpec((1, H_A, tm, LANES), lambda b, s: (b, 0, s, 0)),
            pl.BlockSpec((tm, Z_COLS), lambda b, s: (b * nS + s, 0)),
        ],
        out_shape=[
            jax.ShapeDtypeStruct((B, H_A, S, HEAD_PAD), bf16),
            jax.ShapeDtypeStruct((B, H_A, S, HEAD_PAD), bf16),
            jax.ShapeDtypeStruct((B, H_A, S, LANES), bf16),
            jax.ShapeDtypeStruct((B * S, Z_COLS), f32),
        ],
        compiler_params=pltpu.CompilerParams(
            dimension_semantics=("parallel", "parallel"), vmem_limit_bytes=VMEM_LIMIT),
        name="in_proj",
    )(x, g1, win, qlg, wqa, wqb, kvlg, wkn, wv, cos, sin, gq, gk)


ATTN_ROWS = 32
ATTN_UNROLL = 16


def _attn_kernel(q_ref, k_ref, v_ref, o_ref, s_ref, p_ref, m_ref, al_ref, acc_ref, *, tk, nk):
    tq = q_ref.shape[2]
    nlt = tk // LANES

    def scores(h, q, j, buf):
        off = pl.multiple_of(jnp.minimum(j, nk - 1) * tk, tk)
        s_ref[buf] = _dot_nt(q, k_ref[0, h, pl.ds(off, tk), :])

    def softmax(buf):
        def chunk(c, carry):
            rows = pl.ds(pl.multiple_of(c * ATTN_ROWS, ATTN_ROWS), ATTN_ROWS)
            tiles = [s_ref[buf, rows, lt * LANES:(lt + 1) * LANES] for lt in range(nlt)]
            best = functools.reduce(jnp.maximum, tiles)
            m_old = m_ref[rows, :]
            m_new = jnp.maximum(m_old, jnp.broadcast_to(jnp.max(best, axis=-1, keepdims=True), m_old.shape))
            for lt in range(nlt):
                p_ref[buf, rows, lt * LANES:(lt + 1) * LANES] = jnp.exp2(tiles[lt] - m_new).astype(bf16)
            al_ref[rows, :] = jnp.exp2(m_old - m_new)
            m_ref[rows, :] = m_new
            return carry

        lax.fori_loop(0, tq // ATTN_ROWS, chunk, 0, unroll=ATTN_UNROLL)

    def accumulate(h, j, buf):
        off = pl.multiple_of(j * tk, tk)
        acc_ref[...] = al_ref[...] * acc_ref[...] + _dot(p_ref[buf], v_ref[0, h, pl.ds(off, tk), :])

    outs = []
    for h in range(2):
        q = q_ref[0, h]
        m_ref[...] = jnp.full(m_ref.shape, -jnp.inf, f32)
        acc_ref[...] = jnp.zeros(acc_ref.shape, f32)
        scores(h, q, 0, 0)

        def body(jj, carry, h=h, q=q):
            j = 2 * jj
            scores(h, q, j + 1, 1)
            softmax(0)
            accumulate(h, j, 0)
            scores(h, q, j + 2, 0)
            softmax(1)
            accumulate(h, j + 1, 1)
            return carry

        lax.fori_loop(0, nk // 2, body, 0)
        acc = acc_ref[...]
        outs.append(acc / acc[:, V_DIM:V_DIM + 1])
    lane = lax.broadcasted_iota(jnp.int32, (tq, LANES), 1)
    o_ref[0, 0] = jnp.where(lane < V_DIM, outs[0], pltpu.roll(outs[1], V_DIM, 1))


def _attention(q, k, v, *, B, S, tq, tk):
    return pl.pallas_call(
        functools.partial(_attn_kernel, tk=tk, nk=S // tk),
        grid=(B, H_A // 2, S // tq),
        in_specs=[
            pl.BlockSpec((1, 2, tq, HEAD_PAD), lambda b, p, i: (b, p, i, 0)),
            pl.BlockSpec((1, 2, S, HEAD_PAD), lambda b, p, i: (b, p, 0, 0)),
            pl.BlockSpec((1, 2, S, LANES), lambda b, p, i: (b, p, 0, 0)),
        ],
        out_specs=pl.BlockSpec((1, 1, tq, LANES), lambda b, p, i: (b, p, i, 0)),
        out_shape=jax.ShapeDtypeStruct((B, H_A // 2, S, LANES), f32),
        scratch_shapes=[pltpu.VMEM((2, tq, tk), f32), pltpu.VMEM((2, tq, tk), bf16), pltpu.VMEM((tq, LANES), f32),
                        pltpu.VMEM((tq, LANES), f32), pltpu.VMEM((tq, LANES), f32)],
        compiler_params=pltpu.CompilerParams(
            dimension_semantics=("parallel", "parallel", "parallel"), vmem_limit_bytes=VMEM_LIMIT),
        name="attention",
    )(q, k, v)


def _softplus(x):
    return jnp.maximum(x, 0.0) + jnp.log1p(jnp.exp(-jnp.abs(x)))


def _prep_kernel(z_ref, zp_ref, zn_ref, mup_ref, mun_ref, wl_ref, w0_ref, a0_ref, gup_ref, kk_ref, ka_ref, jseg_ref,
                 r_ref, v_ref, nkk_ref, g_ref,
                 decf_ref, kkaf_ref, kf_ref, decb_ref, kkab_ref, kb_ref):
    si = pl.program_id(1)
    ns = pl.num_programs(1)
    z = z_ref[...]
    tm = z.shape[0]
    row = lax.broadcasted_iota(jnp.int32, (tm, 1), 0)
    zp_row = jnp.where(si == 0, 0.0, zp_ref[7:8, :])
    zn_row = jnp.where(si == ns - 1, 0.0, zn_ref[0:1, :])
    z_prev = jnp.where(row == 0, zp_row, pltpu.roll(z, 1, 0))
    z_next = jnp.where(row == tm - 1, zn_row, pltpu.roll(z, tm - 1, 0))
    zm = z + mup_ref[...] * (z_prev - z) + mun_ref[...] * (z_next - z)

    o1, o2, o3 = C_R, 2 * C_R, 3 * C_R
    o4 = o3 + W_LORA + A_LORA
    o5 = o4 + G_LORA
    r = zm[:, :o1]
    kr = zm[:, o1:o2]
    vr = zm[:, o2:o3]
    lw = zm[:, o3:o4]
    zg = zm[:, o4:o5]
    r_ref[...] = r.astype(bf16)
    v_ref[...] = vr.astype(bf16)
    g_ref[...] = _dot(jax.nn.sigmoid(zg).astype(bf16), gup_ref[...]).astype(bf16)

    kk = kr * kk_ref[...]
    ssk = _split_dot(kk * kk, jseg_ref[...])
    kk = kk / jnp.maximum(jnp.sqrt(ssk), 1e-12)
    nkk_ref[...] = (-kk).astype(bf16)

    lane = lax.broadcasted_iota(jnp.int32, lw.shape, 1)
    lin = jnp.where(lane < W_LORA, jnp.tanh(lw), lw).astype(bf16)
    ka = ka_ref[...]
    for d, (dec_ref, kka_ref, kd_ref) in enumerate(((decf_ref, kkaf_ref, kf_ref), (decb_ref, kkab_ref, kb_ref))):
        lo = _dot(lin, wl_ref[d])
        w = -_softplus(-(w0_ref[d:d + 1, :] + lo[:, :C_R])) - 0.5
        dec_ref[...] = jnp.exp(-jnp.exp(w))
        a = jax.nn.sigmoid(a0_ref[d:d + 1, :] + lo[:, C_R:])
        kd_ref[...] = (kr * (1.0 + (a - 1.0) * ka)).astype(bf16)
        kka_ref[...] = (kk * a).astype(bf16)


def _rwkv_prep(z, mup, mun, wl, w0, a0, gup, kk, ka, jseg, *, B, S, tm):
    nS = S // tm
    T = B * S
    nb8 = tm // 8
    const = lambda shape: pl.BlockSpec(shape, lambda b, s: (0,) * len(shape))
    row_spec = pl.BlockSpec((tm, C_R), lambda b, s: (b * nS + s, 0))
    return pl.pallas_call(
        _prep_kernel,
        grid=(B, nS),
        in_specs=[
            pl.BlockSpec((tm, Z_COLS), lambda b, s: (b * nS + s, 0)),
            pl.BlockSpec((8, Z_COLS), lambda b, s: (jnp.maximum((b * nS + s) * nb8 - 1, 0), 0)),
            pl.BlockSpec((8, Z_COLS), lambda b, s: (jnp.minimum((b * nS + s + 1) * nb8, T // 8 - 1), 0)),
            const((1, Z_COLS)), const((1, Z_COLS)), const(wl.shape), const(w0.shape), const(a0.shape),
            const(gup.shape), const((1, C_R)), const((1, C_R)), const(jseg.shape),
        ],
        out_specs=[row_spec] * 10,
        out_shape=[jax.ShapeDtypeStruct((T, C_R), dt) for dt in (bf16, bf16, bf16, bf16, f32, bf16, bf16, f32, bf16, bf16)],
        compiler_params=pltpu.CompilerParams(
            dimension_semantics=("parallel", "parallel"), vmem_limit_bytes=VMEM_LIMIT),
        name="rwkv_prep",
    )(z, z, z, mup, mun, wl, w0, a0, gup, kk, ka, jseg)


N_PAIRS = H_R // 2
CHUNK = 64


def _halves(x, low):
    zero = jnp.zeros_like(x)
    return jnp.concatenate([jnp.where(low, x, zero), jnp.where(low, zero, x)], axis=0)


def _chunks_kernel(r_ref, nkk_ref, dec_ref, kka_ref, k_ref, tri_ref,
                   at_ref, rt_ref, aak_ref, tt_ref, arb_ref, ark_ref, bt_ref, kt_ref, wc_ref, *, reverse):
    lane = lax.broadcasted_iota(jnp.int32, (CHUNK, LANES), 1)
    low = lane < HEAD_N
    jj = lax.broadcasted_iota(jnp.int32, (CHUNK, LANES), 0)
    ii = lane & (HEAD_N - 1)
    strict = (ii > jj) if reverse else (ii < jj)
    incl = (ii >= jj) if reverse else (ii <= jj)
    rr = lax.broadcasted_iota(jnp.int32, (LANES, LANES), 0)
    cc = lax.broadcasted_iota(jnp.int32, (LANES, LANES), 1)
    eye = (rr == cc).astype(f32)
    last = 0 if reverse else CHUNK - 1
    swap = lambda x: pltpu.roll(x, HEAD_N, 1)
    zero = jnp.zeros((CHUNK, LANES), f32)
    units = [(slice(q * CHUNK, (q + 1) * CHUNK), slice(p * LANES, (p + 1) * LANES))
             for q in range(r_ref.shape[0] // CHUNK) for p in range(N_PAIRS)]
    cums = []
    for rows, cols in units:
        lw = jnp.log(dec_ref[rows, cols])
        hi = lw.astype(bf16)
        r1 = lw - hi.astype(f32)
        mid = r1.astype(bf16)
        lo = (r1 - mid.astype(f32)).astype(bf16)
        cums.append((lw, _dot(tri_ref[...], jnp.concatenate([hi, mid, lo], axis=0))))
    outs = []
    for (rows, cols), (lw, cum) in zip(units, cums):
        w_in = jnp.exp(cum)
        w_ex = jnp.exp(cum - lw)
        w_inv = jnp.exp(-cum)
        wc = w_in[last:last + 1, :]
        at = nkk_ref[rows, cols] * w_ex
        rt = r_ref[rows, cols] * w_in
        bt = kka_ref[rows, cols] * w_inv
        kt = k_ref[rows, cols] * w_inv
        at_ref[rows, cols] = at.astype(bf16)
        rt_ref[rows, cols] = rt.astype(bf16)
        bt_ref[rows, cols] = (bt * wc).astype(bf16)
        kt_ref[rows, cols] = (kt * wc).astype(bf16)
        wc_ref[rows, cols] = jnp.broadcast_to(wc, (CHUNK, LANES))
        lhs = jnp.concatenate([_halves(at, low), _halves(rt, low)], axis=0).astype(bf16)
        rhs = jnp.concatenate([bt, kt], axis=0).astype(bf16)
        outs.append(_dot_nt(lhs, rhs))
    pws, ts = [], []
    for (rows, cols), o in zip(units, outs):
        o0, o1, o2, o3 = (o[q * CHUNK:(q + 1) * CHUNK, :] for q in range(4))
        aab = jnp.where(strict, jnp.where(low, o0, swap(o1)), zero)
        aak_ref[rows, cols] = jnp.where(strict, jnp.where(low, swap(o0), o1), zero).astype(bf16)
        arb_ref[rows, cols] = jnp.where(incl, jnp.where(low, o2, swap(o3)), zero).astype(bf16)
        ark_ref[rows, cols] = jnp.where(incl, jnp.where(low, swap(o2), o3), zero).astype(bf16)
        pws.append(_halves(aab, low))
        ts.append(eye + pws[-1])
    for _ in range(int(math.log2(CHUNK)) - 1):
        pbs = [pw.astype(bf16) for pw in pws]
        pws = [_dot(pb, pb) for pb in pbs]
        ts = [t + _dot(t.astype(bf16), pw.astype(bf16)) for t, pw in zip(ts, pws)]
    for (rows, cols), t in zip(units, ts):
        tt_ref[rows, cols] = jnp.where(low, t[:CHUNK, :], t[CHUNK:, :]).astype(bf16)


CHUNKS_PER_STEP = 4


def _wkv_chunks(r, nkk, dec, kka, k, *, T, reverse):
    tri_i = jnp.arange(CHUNK)
    tri = (tri_i[None, :] >= tri_i[:, None]) if reverse else (tri_i[None, :] <= tri_i[:, None])
    tri3 = jnp.tile(tri.astype(bf16), (1, 3))
    rows = CHUNK * CHUNKS_PER_STEP
    assert T % rows == 0
    row = pl.BlockSpec((rows, C_R), lambda i: (i, 0))
    return pl.pallas_call(
        functools.partial(_chunks_kernel, reverse=reverse),
        grid=(T // rows,),
        in_specs=[row] * 5 + [pl.BlockSpec(tri3.shape, lambda i: (0, 0))],
        out_specs=[row] * 9,
        out_shape=[jax.ShapeDtypeStruct((T, C_R), bf16)] * 8 + [jax.ShapeDtypeStruct((T, C_R), f32)],
        compiler_params=pltpu.CompilerParams(dimension_semantics=("parallel",), vmem_limit_bytes=VMEM_LIMIT),
        name="wkv_chunks",
    )(r, nkk, dec, kka, k, tri3)


def _chunk_scan_kernel(vf_ref, vb_ref, *refs, nb):
    vs = (vf_ref, vb_ref)
    ins = (refs[0:9], refs[9:18])
    ys = refs[18:20]
    st_ref = refs[20]

    @pl.when(pl.program_id(1) == 0)
    def _():
        st_ref[...] = jnp.zeros_like(st_ref)

    low = lax.broadcasted_iota(jnp.int32, (CHUNK, LANES), 1) < HEAD_N
    rr = lax.broadcasted_iota(jnp.int32, (LANES, LANES), 0) < HEAD_N
    cc = lax.broadcasted_iota(jnp.int32, (LANES, LANES), 1) < HEAD_N
    same_head = rr == cc
    chains = [(bb, d, p) for bb in range(nb) for d in range(2) for p in range(N_PAIRS)]
    part = lambda bb, d, p, which: ins[d][which][bb, :, p * LANES:(p + 1) * LANES]
    s0s = [st_ref[(bb * 2 + d) * N_PAIRS + p] for bb, d, p in chains]
    vals = [vs[d][bb, :, p * LANES:(p + 1) * LANES] for bb, d, p in chains]
    vbds = [_halves(v, low).astype(bf16) for v in vals]
    gs = [_dot(jnp.concatenate([part(*ch, 0), part(*ch, 1)], axis=0), s0.astype(bf16))
          for ch, s0 in zip(chains, s0s)]
    xs = [g[:CHUNK, :] + _dot(part(*ch, 2), vbd) for ch, g, vbd in zip(chains, gs, vbds)]
    sas = [_dot(part(*ch, 3), _halves(x, low).astype(bf16)) for ch, x in zip(chains, xs)]
    for (bb, d, p), g, sa, vbd in zip(chains, gs, sas, vbds):
        ys[d][bb, :, p * LANES:(p + 1) * LANES] = (
            g[CHUNK:, :] + _dot(part(bb, d, p, 4), _halves(sa, low).astype(bf16)) + _dot(part(bb, d, p, 5), vbd))
    for (bb, d, p), s0, sa, v in zip(chains, s0s, sas, vals):
        upd = lax.dot_general(
            jnp.concatenate([part(bb, d, p, 6), part(bb, d, p, 7)], axis=0),
            jnp.concatenate([sa.astype(bf16), v.astype(bf16)], axis=0),
            (((0,), (0,)), ((), ())), preferred_element_type=f32)
        wc = part(bb, d, p, 8)
        wcol = jnp.concatenate([wc, wc], axis=0).T
        st_ref[(bb * 2 + d) * N_PAIRS + p] = jnp.where(same_head, wcol * s0 + upd, jnp.zeros_like(s0))


CHUNK_SCAN_BATCH = 4


def _chunk_scan(v, fwd_parts, bwd_parts, *, B, S):
    nC = S // CHUNK
    nb = CHUNK_SCAN_BATCH if B % CHUNK_SCAN_BATCH == 0 else 1
    as3 = lambda a: a.reshape(B, S, C_R)
    fwd = pl.BlockSpec((nb, CHUNK, C_R), lambda b, i: (b, i, 0))
    bwd = pl.BlockSpec((nb, CHUNK, C_R), lambda b, i: (b, nC - 1 - i, 0))
    yf, yb = pl.pallas_call(
        functools.partial(_chunk_scan_kernel, nb=nb),
        grid=(B // nb, nC),
        in_specs=[fwd, bwd] + [fwd] * 9 + [bwd] * 9,
        out_specs=[fwd, bwd],
        out_shape=[jax.ShapeDtypeStruct((B, S, C_R), f32)] * 2,
        scratch_shapes=[pltpu.VMEM((nb * 2 * N_PAIRS, LANES, LANES), f32)],
        compiler_params=pltpu.CompilerParams(
            dimension_semantics=("parallel", "arbitrary"), vmem_limit_bytes=VMEM_LIMIT),
        name="wkv_scan",
    )(as3(v), as3(v), *[as3(a) for a in fwd_parts], *[as3(a) for a in bwd_parts])
    return yf.reshape(B * S, C_R), yb.reshape(B * S, C_R)

def _mix_kernel(x_ref, attn_ref, yf_ref, yb_ref, r_ref, kf_ref, kb_ref, v_ref, g_ref,
                lng_ref, lnb_ref, rk_ref, aog_ref, wout_ref, jseg_ref, h_ref):
    jseg = jseg_ref[...]
    y = yf_ref[...] + yb_ref[...]
    mu = _split_dot(y, jseg) * (1.0 / HEAD_N)
    dlt = y - mu
    var = _split_dot(dlt * dlt, jseg) * (1.0 / HEAD_N)
    yn = dlt * lax.rsqrt(var + GN_EPS) * lng_ref[...] + lnb_ref[...]
    r = r_ref[...].astype(f32)
    k_mean = 0.5 * (kf_ref[...].astype(f32) + kb_ref[...].astype(f32))
    bonus = _split_dot(r * k_mean * rk_ref[...], jseg) * v_ref[...].astype(f32)
    rw = ((yn + bonus) * g_ref[...].astype(f32)).astype(bf16)
    attn = jnp.concatenate([attn_ref[0, p] for p in range(H_A // 2)], axis=-1)
    an = _rms(attn, aog_ref[...]).astype(bf16)
    ha = H_A * V_DIM
    h_ref[...] = x_ref[...] + _dot(an, wout_ref[0:ha, :]) + _dot(rw, wout_ref[ha:, :])


def _mix(x, attn, yf, yb, r, kf, kb, v, g, lng, lnb, rk, aog, wout, jseg, *, B, S, tm):
    nS = S // tm
    const = lambda shape: pl.BlockSpec(shape, lambda b, s: (0,) * len(shape))
    row = lambda c: pl.BlockSpec((tm, c), lambda b, s: (b * nS + s, 0))
    return pl.pallas_call(
        _mix_kernel,
        grid=(B, nS),
        in_specs=[row(D_MODEL), pl.BlockSpec((1, H_A // 2, tm, LANES), lambda b, s: (b, 0, s, 0))]
                 + [row(C_R)] * 7
                 + [const((1, C_R))] * 3 + [const((1, H_A * V_DIM)), const(wout.shape), const(jseg.shape)],
        out_specs=row(D_MODEL),
        out_shape=jax.ShapeDtypeStruct((B * S, D_MODEL), f32),
        compiler_params=pltpu.CompilerParams(
            dimension_semantics=("parallel", "parallel"), vmem_limit_bytes=VMEM_LIMIT),
        name="mix",
    )(x, attn, yf, yb, r, kf, kb, v, g, lng, lnb, rk, aog, wout, jseg)


def _take_max(s, ids):
    m = jnp.max(s, axis=0, keepdims=True)
    pick = jnp.min(jnp.where(s == m, ids, jnp.int32(2 ** 30)), axis=0, keepdims=True)
    return m, pick, jnp.where(ids == pick, -jnp.inf, s)


CAND_COLS = tuple(P_TOPK // (a + 1) for a in range(P_TOPK))
N_CAND = sum(CAND_COLS)
N_CAND_PAD = -(-N_CAND // 8) * 8


def _route_kernel(h_ref, g2_ref, wpq_ref, sk_ref, pos_ref, idx_ref, gate_ref, v1_ref, i1_ref, v2_ref, i2_ref, cs_ref,
                  ci_ref, bs_ref, bi_ref):
    tm = h_ref.shape[0]
    hn = _rms(h_ref[...], g2_ref[...]).astype(bf16)
    pq = _dot(hn, wpq_ref[...])
    key_ids = lax.broadcasted_iota(jnp.int32, (N_KEYS, tm), 0)
    pos_ids = pos_ref[...]
    for p in range(P_HEADS):
        for side, (vs_ref, is_ref) in enumerate(((v1_ref, i1_ref), (v2_ref, i2_ref))):
            qh = pq[:, p * D_KEY + side * HALF_KEY:p * D_KEY + (side + 1) * HALF_KEY].astype(bf16)
            s = _dot_nt(sk_ref[side], qh)
            for j in range(P_TOPK):
                m, pick, s = _take_max(s, key_ids)
                vs_ref[j:j + 1, :] = m
                is_ref[j:j + 1, :] = pick
        v2 = v2_ref[...]
        i2 = i2_ref[...]
        off = 0
        for a, nb in enumerate(CAND_COLS):
            cs_ref[off:off + nb, :] = v1_ref[a:a + 1, :] + v2[:nb, :]
            ci_ref[off:off + nb, :] = i1_ref[a:a + 1, :] * N_KEYS + i2[:nb, :]
            off += nb
        cs_ref[N_CAND:, :] = jnp.full((N_CAND_PAD - N_CAND, tm), -jnp.inf, f32)
        ci_ref[N_CAND:, :] = jnp.zeros((N_CAND_PAD - N_CAND, tm), jnp.int32)
        s = cs_ref[...]
        ci = ci_ref[...]
        for j in range(P_TOPK):
            m, pick, s = _take_max(s, pos_ids)
            bs_ref[j:j + 1, :] = m
            bi_ref[p * P_TOPK + j:p * P_TOPK + j + 1, :] = jnp.max(
                jnp.where(pos_ids == pick, ci, -1), axis=0, keepdims=True)
        bs = bs_ref[...]
        e = jnp.exp(bs - bs[0:1, :])
        gate_ref[:, p * P_TOPK:(p + 1) * P_TOPK] = (e / jnp.sum(e, axis=0, keepdims=True)).T
    idx_ref[...] = bi_ref[...].T


def _route(h, g2, wpq, sk, *, T, tm):
    const = lambda shape: pl.BlockSpec(shape, lambda i: (0,) * len(shape))
    pk = P_HEADS * P_TOPK
    flat = [a * P_TOPK + b for a, nb in enumerate(CAND_COLS) for b in range(nb)] + [2 ** 20] * (N_CAND_PAD - N_CAND)
    pos = jnp.broadcast_to(jnp.array(flat, jnp.int32)[:, None], (N_CAND_PAD, tm))
    return pl.pallas_call(
        _route_kernel,
        grid=(T // tm,),
        in_specs=[pl.BlockSpec((tm, D_MODEL), lambda i: (i, 0)), const((1, D_MODEL)), const(wpq.shape),
                  const(sk.shape), const((N_CAND_PAD, tm))],
        out_specs=[pl.BlockSpec((tm, pk), lambda i: (i, 0))] * 2,
        out_shape=[jax.ShapeDtypeStruct((T, pk), jnp.int32), jax.ShapeDtypeStruct((T, pk), f32)],
        scratch_shapes=[pltpu.VMEM((P_TOPK, tm), f32), pltpu.VMEM((P_TOPK, tm), jnp.int32),
                        pltpu.VMEM((P_TOPK, tm), f32), pltpu.VMEM((P_TOPK, tm), jnp.int32),
                        pltpu.VMEM((N_CAND_PAD, tm), f32), pltpu.VMEM((N_CAND_PAD, tm), jnp.int32),
                        pltpu.VMEM((P_TOPK, tm), f32), pltpu.VMEM((pk, tm), jnp.int32)],
        compiler_params=pltpu.CompilerParams(dimension_semantics=("parallel",), vmem_limit_bytes=VMEM_LIMIT),
        name="route",
    )(h, g2, wpq, sk, pos)


GATHER_WINDOW = 64
SC_CORES = 2
SC_SUBCORES = 16
SC_WORKERS = SC_CORES * SC_SUBCORES


def _gather_rows(table_a, table_b, idx):
    n = idx.shape[0]
    d = table_a.shape[1]
    assert table_b.shape == table_a.shape and table_b.dtype == table_a.dtype
    win = GATHER_WINDOW
    n_it = n // (SC_WORKERS * win)
    assert n_it * SC_WORKERS * win == n and n_it % 2 == 0, (n, n_it)
    mesh = plsc.VectorSubcoreMesh(core_axis_name="core", subcore_axis_name="subcore")
    rows = jax.ShapeDtypeStruct((n, d), table_a.dtype)

    @functools.partial(
        pl.kernel, out_type=(rows, rows), mesh=mesh,
        scratch_types=[pltpu.VMEM((n_it, win), jnp.int32), pltpu.VMEM((2, win, d), table_a.dtype),
                       pltpu.SemaphoreType.DMA, pltpu.SemaphoreType.DMA,
                       pltpu.SemaphoreType.DMA, pltpu.SemaphoreType.DMA])
    def gather(taba_hbm, tabb_hbm, idx_hbm, outa_hbm, outb_hbm, idx_v, rows_v, gsem0, gsem1, wsem0, wsem1):
        wid = lax.axis_index("subcore") * SC_CORES + lax.axis_index("core")
        base = wid * (n_it * win)
        gsem = (gsem0, gsem1)
        wsem = (wsem0, wsem1)
        pltpu.sync_copy(idx_hbm.at[wid], idx_v)

        for tab_hbm, out_hbm in ((taba_hbm, outa_hbm), (tabb_hbm, outb_hbm)):
            def fetch(j, b):
                return pltpu.make_async_copy(tab_hbm.at[idx_v.at[j]], rows_v.at[b], gsem[b])

            def flush(j, b):
                return pltpu.make_async_copy(rows_v.at[b], out_hbm.at[pl.ds(base + j * win, win)], wsem[b])

            fetch(0, 0).start()

            @pl.loop(0, n_it, step=2)
            def _(j0):
                for b in range(2):
                    j = j0 + b

                    @pl.when(j >= 1)
                    def _():
                        flush(j - 1, 1 - b).wait()

                    @pl.when(j + 1 < n_it)
                    def _():
                        fetch(j + 1, 1 - b).start()

                    fetch(j, b).wait()
                    flush(j, b).start()

            flush(n_it - 1, 1).wait()

    return gather(table_a, table_b, idx.reshape(SC_WORKERS, n_it, win))


HALF_D = D_MODEL // 2


def _pack_rows(table):
    bits = lax.bitcast_convert_type(table.astype(bf16), jnp.uint16).astype(jnp.uint32)
    return bits[:, :HALF_D] | (bits[:, HALF_D:] << 16)


def _unpack_rows(words):
    lo = lax.bitcast_convert_type(words << 16, f32).astype(bf16)
    hi = lax.bitcast_convert_type(words & jnp.uint32(0xFFFF0000), f32).astype(bf16)
    return lo, hi


def _expert_kernel(h_ref, g2_ref, gate_ref, gu_ref, gv_ref, o_ref):
    h = h_ref[...]
    tb = h.shape[0]
    pk = P_HEADS * P_TOPK
    hn = _rms(h, g2_ref[...]).astype(bf16)
    row = lax.broadcasted_iota(jnp.int32, (8, pk), 0)
    for grp in range(tb // 8):
        rows = slice(grp * 8, (grp + 1) * 8)
        hn8 = hn[rows, :]
        act = jnp.zeros((8, pk), f32)
        for j in range(8):
            t = grp * 8 + j
            ulo, uhi = _unpack_rows(gu_ref[t * pk:(t + 1) * pk, :])
            act = jnp.where(row == j, _dot_nt(hn8[:, :HALF_D], ulo) + _dot_nt(hn8[:, HALF_D:], uhi), act)
        gelu = 0.5 * act * (1.0 + lax.erf(act * (2.0 ** -0.5)))
        w = (gate_ref[rows, :] * gelu).astype(bf16)
        out_lo = jnp.zeros((8, HALF_D), f32)
        out_hi = jnp.zeros((8, HALF_D), f32)
        for j in range(8):
            t = grp * 8 + j
            vlo, vhi = _unpack_rows(gv_ref[t * pk:(t + 1) * pk, :])
            wj = jnp.where(row == j, w, jnp.zeros_like(w))
            out_lo = out_lo + _dot(wj, vlo)
            out_hi = out_hi + _dot(wj, vhi)
        o_ref[rows, :HALF_D] = h[rows, :HALF_D] + out_lo
        o_ref[rows, HALF_D:] = h[rows, HALF_D:] + out_hi


def _experts(h, g2, gate, gu, gv, *, tb):
    n = h.shape[0]
    pk = P_HEADS * P_TOPK
    const = lambda shape: pl.BlockSpec(shape, lambda i: (0,) * len(shape))
    return pl.pallas_call(
        _expert_kernel,
        grid=(n // tb,),
        in_specs=[pl.BlockSpec((tb, D_MODEL), lambda i: (i, 0)), const((1, D_MODEL)),
                  pl.BlockSpec((tb, pk), lambda i: (i, 0)),
                  pl.BlockSpec((tb * pk, HALF_D), lambda i: (i, 0)),
                  pl.BlockSpec((tb * pk, HALF_D), lambda i: (i, 0))],
        out_specs=pl.BlockSpec((tb, D_MODEL), lambda i: (i, 0)),
        out_shape=jax.ShapeDtypeStruct((n, D_MODEL), f32),
        compiler_params=pltpu.CompilerParams(dimension_semantics=("parallel",), vmem_limit_bytes=VMEM_LIMIT),
        name="experts",
    )(h, g2, gate, gu, gv)


def _place(cols, width, offset):
    return jnp.pad(cols, ((0, 0), (offset, width - offset - cols.shape[1])))


def _rope_partner():
    half = ROPE_DIM // 2
    return jnp.concatenate([jnp.arange(half, ROPE_DIM), jnp.arange(0, half)])


def _prepare(norm1_g, w_in, q_lat_g, w_uq, kv_lat_g, w_ukv, q_norm_g, k_norm_g, attn_out_g, mu_prev, mu_next,
             w0, w_up, a0, a_up, g_up, k_k, k_a, r_k, ln_x_g, ln_x_b, w_out, norm2_g, w_pq, sub_keys, S):
    partner = _rope_partner()
    wz = w_in[:, OFF_RWKV:]
    wkr = w_in[:, OFF_KR:OFF_RWKV]
    win = jnp.concatenate([
        w_in[:, OFF_Q:OFF_KV], w_in[:, OFF_KV:OFF_KR], wz,
        _place(wkr, LANES, NOPE_DIM), _place(wkr[:, partner], LANES, NOPE_DIM)], axis=1).astype(bf16)

    wq = w_uq.reshape(Q_LORA, H_A, QK_DIM)
    wqa = jnp.pad(wq, ((0, 0), (0, 0), (0, HEAD_PAD - QK_DIM))).reshape(Q_LORA, H_A * HEAD_PAD).astype(bf16)
    wqb = jnp.pad(wq[:, :, NOPE_DIM:][:, :, partner],
                  ((0, 0), (0, 0), (NOPE_DIM, HEAD_PAD - QK_DIM))).reshape(Q_LORA, H_A * HEAD_PAD).astype(bf16)
    wkv = w_ukv.reshape(KV_LORA, H_A, NOPE_DIM + V_DIM)
    wkn = jnp.pad(wkv[:, :, :NOPE_DIM], ((0, 0), (0, 0), (0, HEAD_PAD - NOPE_DIM))).reshape(
        KV_LORA, H_A * HEAD_PAD).astype(bf16)
    wv = jnp.pad(wkv[:, :, NOPE_DIM:], ((0, 0), (0, 0), (0, LANES - V_DIM))).reshape(
        KV_LORA, H_A * LANES).astype(bf16)

    def gain_rows(g):
        ga = _place(g[None, :], LANES, 0)
        gb = _place(g[None, NOPE_DIM:][:, partner], LANES, NOPE_DIM)
        return ga, gb

    gqa, gqb = gain_rows(q_norm_g)
    gq = jnp.concatenate([gqa, gqb, jnp.zeros((6, LANES), f32)], axis=0)
    gkn = _place(k_norm_g[None, :NOPE_DIM], LANES, 0)
    gka = _place(k_norm_g[None, NOPE_DIM:], LANES, NOPE_DIM)
    gkb = _place(k_norm_g[None, NOPE_DIM:][:, partner], LANES, NOPE_DIM)
    gk = jnp.concatenate([gkn, gka, gkb, jnp.zeros((5, LANES), f32)], axis=0)

    half = ROPE_DIM // 2
    inv = 1.0 / (ROPE_THETA ** (jnp.arange(half, dtype=f32) / half))
    ang = jnp.arange(S, dtype=f32)[:, None] * inv[None, :]
    c, s = jnp.cos(ang), jnp.sin(ang)
    cos = jnp.concatenate([jnp.ones((S, NOPE_DIM), f32), c, c, jnp.zeros((S, HEAD_PAD - QK_DIM), f32)], axis=1)
    sin = jnp.concatenate([jnp.zeros((S, NOPE_DIM), f32), -s, s, jnp.zeros((S, HEAD_PAD - QK_DIM), f32)], axis=1)

    zeros = jnp.zeros((W_LORA, C_R), f32)
    wl = jnp.stack([jnp.concatenate([jnp.concatenate([w_up[d], zeros], axis=1),
                                     jnp.concatenate([zeros, a_up[d]], axis=1)], axis=0) for d in range(2)]).astype(bf16)
    seg = jnp.arange(C_R) // HEAD_N
    jseg = (seg[:, None] == seg[None, :]).astype(bf16)

    return dict(
        g1=norm1_g[None, :], win=win, qlg=q_lat_g[None, :], wqa=wqa, wqb=wqb, kvlg=kv_lat_g[None, :], wkn=wkn, wv=wv,
        cos=cos, sin=sin, gq=gq, gk=gk,
        mup=mu_prev[None, :], mun=mu_next[None, :], wl=wl, w0=w0, a0=a0,
        gup=g_up.astype(bf16), kk=k_k[None, :], ka=k_a[None, :], jseg=jseg,
        lng=ln_x_g[None, :], lnb=ln_x_b[None, :], rk=r_k[None, :], aog=attn_out_g[None, :],
        wout=w_out.astype(bf16), g2=norm2_g[None, :], wpq=w_pq.astype(bf16), sk=sub_keys.astype(bf16),
    )


def _tile(n, pref):
    t = min(n, pref)
    assert n % t == 0, (n, t)
    return t


PEER_CHUNK = 4096
EXPERT_TOKENS = 32


def _dense(x, w):
    B, S, _ = x.shape
    T = B * S
    xt = x.reshape(T, D_MODEL)
    tm = _tile(S, 256)
    q, k, v, z = _inproj(xt, w["g1"], w["win"], w["qlg"], w["wqa"], w["wqb"], w["kvlg"], w["wkn"], w["wv"],
                         w["cos"], w["sin"], w["gq"], w["gk"], B=B, S=S, tm=tm)
    attn = _attention(q, k, v, B=B, S=S, tq=_tile(S, 512), tk=_tile(S // 2, 512))
    r, vr, nkk, g, decf, kkaf, kf, decb, kkab, kb = _rwkv_prep(
        z, w["mup"], w["mun"], w["wl"], w["w0"], w["a0"], w["gup"], w["kk"], w["ka"], w["jseg"], B=B, S=S, tm=tm)
    fw = _wkv_chunks(r, nkk, decf, kkaf, kf, T=T, reverse=False)
    bw = _wkv_chunks(r, nkk, decb, kkab, kb, T=T, reverse=True)
    yf, yb = _chunk_scan(vr, fw, bw, B=B, S=S)
    h = _mix(xt, attn, yf, yb, r, kf, kb, vr, g, w["lng"], w["lnb"], w["rk"], w["aog"], w["wout"], w["jseg"],
             B=B, S=S, tm=tm)
    idx, gate = _route(h, w["g2"], w["wpq"], w["sk"], T=T, tm=_tile(T, 256))
    return h, idx, gate


def _peer(h, idx, gate, w, pu, pv, shape):
    T = h.shape[0]
    pk = P_HEADS * P_TOPK
    tc = _tile(T, PEER_CHUNK)
    outs = []
    for c in range(T // tc):
        sl = slice(c * tc, (c + 1) * tc)
        flat = idx[sl].reshape(tc * pk)
        gu, gv = _gather_rows(pu, pv, flat)
        outs.append(_experts(h[sl], w["g2"], gate[sl], gu, gv, tb=_tile(tc, EXPERT_TOKENS)))
    return jnp.concatenate(outs, axis=0).reshape(shape)


def _layer(xs, w, expert_u, expert_v):
    pu = _pack_rows(expert_u)
    pv = _pack_rows(expert_v)
    outs = []
    pending = None
    for x in xs:
        cur = _dense(x, w) + (x.shape,)
        if pending is not None:
            outs.append(_peer(*pending[:3], w, pu, pv, pending[3]))
        pending = cur
    outs.append(_peer(*pending[:3], w, pu, pv, pending[3]))
    return outs


GROUP_BATCH = 1


def kernel(x_prompt, x_sample, norm1_g, w_in, q_lat_g, w_uq, kv_lat_g, w_ukv, q_norm_g, k_norm_g, attn_out_g, mu_prev, mu_next, w0, w_up, a0, a_up, g_up, k_k, k_a, r_k, ln_x_g, ln_x_b, w_out, norm2_g, w_pq, sub_keys, expert_u, expert_v):
    assert x_prompt.shape[1] == x_sample.shape[1]
    S = x_prompt.shape[1]
    groups = []
    for x in (x_prompt, x_sample):
        gb = GROUP_BATCH if x.shape[0] % GROUP_BATCH == 0 else x.shape[0]
        groups.append([x[i:i + gb] for i in range(0, x.shape[0], gb)])
    xs = groups[0] + groups[1]
    for l in range(norm1_g.shape[0]):
        w = _prepare(norm1_g[l], w_in[l], q_lat_g[l], w_uq[l], kv_lat_g[l], w_ukv[l], q_norm_g[l], k_norm_g[l],
                     attn_out_g[l], mu_prev[l], mu_next[l], w0[l], w_up[l], a0[l], a_up[l], g_up[l], k_k[l], k_a[l],
                     r_k[l], ln_x_g[l], ln_x_b[l], w_out[l], norm2_g[l], w_pq[l], sub_keys[l], S)
        xs = _layer(xs, w, expert_u[l], expert_v[l])
    n0 = len(groups[0])
    join = lambda parts: parts[0] if len(parts) == 1 else jnp.concatenate(parts, axis=0)
    return join(xs[:n0]), join(xs[n0:])
```
